```python
import math
import jax, jax.numpy as jnp
from jax import lax
import numpy as np

D_MODEL = 1024
BATCH = 2
SEQ = 8192
DEPTH = 2
DEC_BATCH = 32
DEC_SEQ = 1
PAST_LEN = 8192
PAGE_SIZE = 128

H_M = 4
DH_M = 128
MW_M = H_M * DH_M
MLSTM_CHUNK = 64
H_A = 8
DH_A = 64
MW_A = H_A * DH_A
MOBA_BLOCK = 256
MOBA_TOPK = 3
Q_BLOCK = 64
N_BUCKETS = 32
REL_MAX_DIST = 2048
D_FF = ((8 * D_MODEL + 3 * 256 - 1) // (3 * 256)) * 256
N_IN = 4 * MW_M + 2 * H_M + 3 * MW_A + 2 * D_MODEL
EPS = 1e-6
NEG = -1e30

kernel_name = 'hybrid_mlstm_moba_decode_step'


def _split_points():
    widths = [MW_M] * 4 + [H_M] * 2 + [MW_A] * 3 + [D_MODEL]
    pts, acc = [], 0
    for w_ in widths:
        acc += w_
        pts.append(acc)
    return pts


def rms_norm(x, g):
    xf = x.astype(jnp.float32)
    y = xf * lax.rsqrt(jnp.mean(xf * xf, axis=-1, keepdims=True) + EPS)
    return (y * g.astype(jnp.float32)).astype(x.dtype)


def head_norm(h, g):
    mu = jnp.mean(h, axis=-1, keepdims=True)
    var = jnp.mean(jnp.square(h - mu), axis=-1, keepdims=True)
    return (h - mu) * lax.rsqrt(var + EPS) * g.reshape(H_M, DH_M).astype(jnp.float32)


def mlstm_chunked(q, k, v, i_pre, log_f, c0, n0, m0):
    B, T = q.shape[:2]
    lc = min(MLSTM_CHUNK, T)
    nc = -(-T // lc)
    pad = nc * lc - T

    def prep(a, fill):
        a = jnp.pad(a, [(0, 0), (0, pad)] + [(0, 0)] * (a.ndim - 2), constant_values=fill)
        a = a.reshape((B, nc, lc) + a.shape[2:])
        return jnp.transpose(a, (1, 0, 3, 2) + tuple(range(4, a.ndim)))

    xs = (prep(q, 0.0), prep(k, 0.0), prep(v, 0.0), prep(i_pre, NEG), prep(log_f, 0.0))
    causal = jnp.tril(jnp.ones((lc, lc), dtype=bool))

    def step(carry, xc):
        c, n, m = carry
        qc, kc, vc, ic, fc = xc
        b = jnp.cumsum(fc, axis=-1)
        d = jnp.where(causal, b[..., :, None] - b[..., None, :] + ic[..., None, :], -jnp.inf)
        inter = b + m[..., None]
        mt = jnp.maximum(inter, jnp.max(d, axis=-1))
        a = jnp.exp(inter - mt)
        s = jnp.einsum('bhtk,bhsk->bhts', qc, kc) * jnp.exp(d - mt[..., None])
        num = a[..., None] * jnp.einsum('bhtk,bhkv->bhtv', qc, c) + jnp.einsum('bhts,bhsv->bhtv', s, vc)
        den = a * jnp.einsum('bhtk,bhk->bht', qc, n) + jnp.sum(s, axis=-1)
        h = num / jnp.maximum(jnp.abs(den), jnp.exp(-mt))[..., None]
        g = b[..., -1:] - b + ic
        e = b[..., -1] + m
        m_new = jnp.maximum(e, jnp.max(g, axis=-1))
        w = jnp.exp(g - m_new[..., None])
        ae = jnp.exp(e - m_new)
        c_new = ae[..., None, None] * c + jnp.einsum('bhs,bhsk,bhsv->bhkv', w, kc, vc)
        n_new = ae[..., None] * n + jnp.einsum('bhs,bhsk->bhk', w, kc)
        return (c_new, n_new, m_new), h

    (c, n, m), hs = lax.scan(step, (c0, n0, m0), xs)
    h = jnp.transpose(hs, (1, 0, 3, 2, 4)).reshape(B, nc * lc, H_M, DH_M)[:, :T]
    return h, c, n, m


def t5_bucket(dist):
    n = jnp.maximum(dist, 0)
    max_exact = N_BUCKETS // 2
    nf = jnp.maximum(n, 1).astype(jnp.float32)
    large = max_exact + (jnp.log(nf / max_exact) / math.log(REL_MAX_DIST / max_exact)
                         * (N_BUCKETS - max_exact)).astype(jnp.int32)
    large = jnp.minimum(large, N_BUCKETS - 1)
    return jnp.where(n < max_exact, n, large)


def moba_attend(q, k, v, q_start, rel_bias):
    B, Tq = q.shape[:2]
    L = k.shape[1]
    nb = -(-L // MOBA_BLOCK)
    padk = nb * MOBA_BLOCK - L
    kb = jnp.pad(k, ((0, 0), (0, padk), (0, 0), (0, 0))).reshape(B, nb, MOBA_BLOCK, H_A, DH_A).transpose(0, 3, 1, 2, 4)
    vb = jnp.pad(v, ((0, 0), (0, padk), (0, 0), (0, 0))).reshape(B, nb, MOBA_BLOCK, H_A, DH_A).transpose(0, 3, 1, 2, 4)
    kmean = jnp.mean(kb.astype(jnp.float32), axis=3)
    topk = min(MOBA_TOPK, nb)
    qb = min(Q_BLOCK, Tq)
    nq = -(-Tq // qb)
    padq = nq * qb - Tq
    qs = jnp.pad(q, ((0, 0), (0, padq), (0, 0), (0, 0))).reshape(B, nq, qb, H_A, DH_A).transpose(1, 0, 3, 2, 4)
    pos = (q_start + jnp.arange(nq * qb, dtype=jnp.int32)).reshape(nq, qb)
    b_idx = jnp.arange(B)[:, None, None, None]
    h_idx = jnp.arange(H_A)[None, :, None, None]
    rel_t = rel_bias.T.astype(jnp.float32)
    offs = jnp.arange(MOBA_BLOCK, dtype=jnp.int32)
    scale = DH_A ** -0.5

    def one(args):
        qc, pc = args
        cb = jnp.minimum(pc // MOBA_BLOCK, nb - 1)
        score = jnp.einsum('bhqd,bhnd->bhqn', qc.astype(jnp.float32), kmean)
        past = jnp.arange(nb)[None, :] < cb[:, None]
        score = jnp.where(past, score, -jnp.inf)
        _, sel = lax.top_k(score, topk)
        own = jnp.broadcast_to(cb[None, None, :, None], (B, H_A, qb, 1))
        idx = jnp.concatenate([sel, own], axis=-1)
        valid = jnp.concatenate([sel < cb[None, None, :, None], jnp.ones(own.shape, dtype=bool)], axis=-1)
        ksel = kb[b_idx, h_idx, idx]
        vsel = vb[b_idx, h_idx, idx]
        logits = jnp.einsum('bhqd,bhqsjd->bhqsj', qc, ksel).astype(jnp.float32) * scale
        kpos = idx[..., None] * MOBA_BLOCK + offs
        dist = pc[None, None, :, None, None] - kpos
        bias = rel_t[h_idx[..., None], t5_bucket(dist)]
        mask = valid[..., None] & (dist >= 0)
        logits = jnp.where(mask, logits + bias, NEG)
        p = jax.nn.softmax(logits.reshape(B, H_A, qb, -1), axis=-1).reshape(logits.shape)
        return jnp.einsum('bhqsj,bhqsjd->bqhd', p.astype(v.dtype), vsel)

    out = lax.map(one, (qs, pos))
    return out.transpose(1, 0, 2, 3, 4).reshape(B, nq * qb, H_A, DH_A)[:, :Tq]


def trunk_layer(x, g_mix_pre, g_mix_post, g_ffn_pre, g_ffn_post, w_in, b_igate, b_fgate, g_mlstm,
                w_branch_m, w_branch_a, w_out, w_gate, w_up, w_down, rel_bias,
                c0, n0, m0, k_past, v_past):
    B, T, _ = x.shape
    f32 = jnp.float32
    u = rms_norm(x, g_mix_pre) @ w_in
    q_m, k_m, v_m, o_m, i_m, f_m, q_a, k_a, v_a, gate_m, gate_a = jnp.split(u, _split_points(), axis=-1)
    qm = q_m.reshape(B, T, H_M, DH_M).astype(f32)
    km = k_m.reshape(B, T, H_M, DH_M).astype(f32) * (DH_M ** -0.5)
    vm = v_m.reshape(B, T, H_M, DH_M).astype(f32)
    i_pre = i_m.astype(f32) + b_igate.astype(f32)
    log_f = jax.nn.log_sigmoid(f_m.astype(f32) + b_fgate.astype(f32))
    hm, c, n, m = mlstm_chunked(qm, km, vm, i_pre, log_f, c0.astype(f32), n0.astype(f32), m0.astype(f32))
    hm = head_norm(hm, g_mlstm) * jax.nn.sigmoid(o_m.reshape(B, T, H_M, DH_M).astype(f32))
    hm = hm.reshape(B, T, MW_M).astype(x.dtype)
    qa = q_a.reshape(B, T, H_A, DH_A)
    ka = k_a.reshape(B, T, H_A, DH_A)
    va = v_a.reshape(B, T, H_A, DH_A)
    if k_past is None:
        k_all, v_all, q_start = ka, va, 0
    else:
        k_all = jnp.concatenate([k_past.astype(ka.dtype), ka], axis=1)
        v_all = jnp.concatenate([v_past.astype(va.dtype), va], axis=1)
        q_start = k_past.shape[1]
    ha = moba_attend(qa, k_all, v_all, q_start, rel_bias).reshape(B, T, MW_A)
    mixed = jax.nn.sigmoid(gate_m) * (hm @ w_branch_m) + jax.nn.sigmoid(gate_a) * (ha @ w_branch_a)
    x = x + rms_norm(mixed @ w_out, g_mix_post)
    hf = rms_norm(x, g_ffn_pre)
    ff = (jax.nn.silu(hf @ w_gate) * (hf @ w_up)) @ w_down
    x = x + rms_norm(ff, g_ffn_post)
    return x, ka, va, c, n, m


def setup_inputs(seed: int = 0) -> dict:
    key = jax.random.key(seed)
    ks = jax.random.split(key, 24)
    n_pages = PAST_LEN // PAGE_SIZE
    n_used = DEC_BATCH * n_pages
    n_pool = (5 * n_used + 3) // 4

    def nrm(k, shape, s):
        return s * jax.random.normal(k, shape, jnp.float32)

    page_table = jax.random.permutation(ks[7], n_pool)[:n_used].reshape(DEC_BATCH, n_pages).astype(jnp.int32)
    return {
        'x_prompt': nrm(ks[0], (BATCH, SEQ, D_MODEL), 1.0),
        'x_sample': nrm(ks[1], (DEC_BATCH, DEC_SEQ, D_MODEL), 1.0),
        'cache_k': nrm(ks[2], (DEPTH, n_pool, PAGE_SIZE, H_A, DH_A), 1.0),
        'cache_v': nrm(ks[3], (DEPTH, n_pool, PAGE_SIZE, H_A, DH_A), 1.0),
        'state_C': nrm(ks[4], (DEPTH, DEC_BATCH, H_M, DH_M, DH_M), 0.3),
        'state_n': nrm(ks[5], (DEPTH, DEC_BATCH, H_M, DH_M), 0.3),
        'state_m': nrm(ks[6], (DEPTH, DEC_BATCH, H_M), 1.0),
        'page_table': page_table,
        'norm_mix_pre': 1.0 + nrm(ks[8], (DEPTH, D_MODEL), 0.05),
        'norm_mix_post': 1.0 + nrm(ks[9], (DEPTH, D_MODEL), 0.05),
        'norm_ffn_pre': 1.0 + nrm(ks[10], (DEPTH, D_MODEL), 0.05),
        'norm_ffn_post': 1.0 + nrm(ks[11], (DEPTH, D_MODEL), 0.05),
        'w_in': nrm(ks[12], (DEPTH, D_MODEL, N_IN), D_MODEL ** -0.5),
        'b_igate': nrm(ks[13], (DEPTH, H_M), 0.1),
        'b_fgate': jnp.linspace(3.0, 6.0, H_M, dtype=jnp.float32)[None, :] + nrm(ks[14], (DEPTH, H_M), 0.1),
        'g_mlstm': 1.0 + nrm(ks[15], (DEPTH, MW_M), 0.05),
        'w_branch_m': nrm(ks[16], (DEPTH, MW_M, D_MODEL), MW_M ** -0.5),
        'w_branch_a': nrm(ks[17], (DEPTH, MW_A, D_MODEL), MW_A ** -0.5),
        'w_out': nrm(ks[18], (DEPTH, D_MODEL, D_MODEL), D_MODEL ** -0.5),
        'w_ffn_gate': nrm(ks[19], (DEPTH, D_MODEL, D_FF), D_MODEL ** -0.5),
        'w_ffn_up': nrm(ks[20], (DEPTH, D_MODEL, D_FF), D_MODEL ** -0.5),
        'w_ffn_down': nrm(ks[21], (DEPTH, D_FF, D_MODEL), D_FF ** -0.5),
        'rel_bias': nrm(ks[22], (N_BUCKETS, H_A), 0.5),
    }


def reference(x_prompt, x_sample, cache_k, cache_v, state_C, state_n, state_m, page_table,
              norm_mix_pre, norm_mix_post, norm_ffn_pre, norm_ffn_post, w_in, b_igate, b_fgate, g_mlstm,
              w_branch_m, w_branch_a, w_out, w_ffn_gate, w_ffn_up, w_ffn_down, rel_bias):
    f32 = jnp.float32
    bp = x_prompt.shape[0]
    bs = x_sample.shape[0]
    past_len = page_table.shape[1] * cache_k.shape[2]
    yp, ys = x_prompt, x_sample
    kp_l, vp_l, cp_l, np_l, mp_l = [], [], [], [], []
    ks_l, vs_l, cs_l, ns_l, ms_l = [], [], [], [], []
    for l in range(DEPTH):
        w = (norm_mix_pre[l], norm_mix_post[l], norm_ffn_pre[l], norm_ffn_post[l], w_in[l], b_igate[l],
             b_fgate[l], g_mlstm[l], w_branch_m[l], w_branch_a[l], w_out[l], w_ffn_gate[l], w_ffn_up[l],
             w_ffn_down[l], rel_bias)
        c0 = jnp.zeros((bp, H_M, DH_M, DH_M), f32)
        n0 = jnp.zeros((bp, H_M, DH_M), f32)
        m0 = jnp.zeros((bp, H_M), f32)
        yp, kp, vp, cp, npv, mp = trunk_layer(yp, *w, c0, n0, m0, None, None)
        k_past = cache_k[l][page_table].reshape(bs, past_len, H_A, DH_A)
        v_past = cache_v[l][page_table].reshape(bs, past_len, H_A, DH_A)
        ys, kss, vss, css, nss, mss = trunk_layer(ys, *w, state_C[l], state_n[l], state_m[l], k_past, v_past)
        kp_l.append(kp); vp_l.append(vp); cp_l.append(cp); np_l.append(npv); mp_l.append(mp)
        ks_l.append(kss); vs_l.append(vss); cs_l.append(css); ns_l.append(nss); ms_l.append(mss)
    return (yp, ys, jnp.stack(kp_l), jnp.stack(vp_l), jnp.stack(cp_l), jnp.stack(np_l), jnp.stack(mp_l),
            jnp.stack(ks_l), jnp.stack(vs_l), jnp.stack(cs_l), jnp.stack(ns_l), jnp.stack(ms_l))
```

```python
import functools
import math

import jax
import jax.numpy as jnp
from jax import lax
from jax.experimental import pallas as pl
from jax.experimental.pallas import tpu as pltpu

F32 = jnp.float32
BF16 = jnp.bfloat16
HIGHEST = lax.Precision.HIGHEST

D_MODEL = 1024
H_M, DH_M = 4, 128
MW_M = H_M * DH_M
H_A, DH_A = 8, 64
MW_A = H_A * DH_A
MOBA_BLOCK = 256
MOBA_TOPK = 3
N_BUCKETS = 32
REL_MAX_DIST = 2048
D_FF = 2816
EPS = 1e-6
NEG = -1e30
PAGE_SIZE = 128

N_GATE_COLS = 2 * H_M
COL_M_END = 4 * MW_M
COL_IF_END = COL_M_END + N_GATE_COLS
COL_A_END = COL_IF_END + 3 * MW_A
LOG_K_SCALE = math.log(DH_M ** -0.5)
Q_SCALE = DH_A ** -0.5

N_NEAR = (REL_MAX_DIST + MOBA_BLOCK - 1) // MOBA_BLOCK + 1
assert (N_NEAR * MOBA_BLOCK - (MOBA_BLOCK - 1)) >= REL_MAX_DIST

VMEM_LIMIT = 56 * 1024 * 1024

TM_PROJ = 512
TM_FFN = 256
L_CHUNK = 256
FF_CHUNK = 1408
PAGES_PER_STEP = 16


def _params(*sem):
    return pltpu.CompilerParams(dimension_semantics=sem, vmem_limit_bytes=VMEM_LIMIT)


def _rms(x, g):
    return x * lax.rsqrt(jnp.mean(x * x, axis=-1, keepdims=True) + EPS) * g


def _log_sigmoid(x):
    return jnp.minimum(x, 0.0) - jnp.log(1.0 + jnp.exp(-jnp.abs(x)))


def _dot(a, b):
    return jnp.dot(a, b, preferred_element_type=F32)


def _dot_nt(a, b, precision=None):
    return lax.dot_general(a, b, (((1,), (1,)), ((), ())), precision=precision,
                           preferred_element_type=F32)


def _dot_tn(a, b):
    return lax.dot_general(a, b, (((0,), (0,)), ((), ())), preferred_element_type=F32)


def _inproj_prompt_kernel(x_ref, g_ref, wm_ref, wif_ref, wift_ref, wa_ref, wg_ref,
                          m_ref, gc_ref, gr_ref, q_ref, kt_ref, k32_ref, v32_ref, vb_ref,
                          km_ref, sg_ref):
    xn = _rms(x_ref[...], g_ref[...])
    xb = xn.astype(BF16)
    m_ref[...] = _dot(xb, wm_ref[...]).astype(BF16)
    gc_ref[...] = jnp.dot(xn, wif_ref[...], precision=HIGHEST, preferred_element_type=F32)
    gr_ref[...] = _dot_nt(wift_ref[...], xn, precision=HIGHEST)
    a = _dot(xb, wa_ref[...])
    k = a[:, MW_A:2 * MW_A]
    v = a[:, 2 * MW_A:]
    q_ref[...] = (a[:, :MW_A] * Q_SCALE).astype(BF16)
    k32_ref[...] = k
    v32_ref[...] = v
    kt_ref[...] = k.T.astype(BF16)
    vb_ref[...] = v.astype(BF16)
    tm = k.shape[0]
    km_ref[0] = jnp.mean(k.reshape(tm // MOBA_BLOCK, MOBA_BLOCK, MW_A), axis=1)
    sg_ref[...] = jax.nn.sigmoid(_dot(xb, wg_ref[...])).astype(BF16)


def _inproj_prompt(x, g, wm, wif, wift, wa, wg):
    m = x.shape[0]
    tm = TM_PROJ
    nt = m // tm
    row = lambda w: pl.BlockSpec((tm, w), lambda i: (i, 0))
    full = lambda a: pl.BlockSpec(a.shape, lambda i: (0,) * a.ndim)
    return pl.pallas_call(
        _inproj_prompt_kernel,
        grid=(nt,),
        in_specs=[row(D_MODEL), full(g), full(wm), full(wif), full(wift), full(wa), full(wg)],
        out_specs=[row(COL_M_END), row(N_GATE_COLS),
                   pl.BlockSpec((N_GATE_COLS, tm), lambda i: (0, i)),
                   row(MW_A), pl.BlockSpec((MW_A, tm), lambda i: (0, i)),
                   row(MW_A), row(MW_A), row(MW_A),
                   pl.BlockSpec((1, tm // MOBA_BLOCK, MW_A), lambda i: (i, 0, 0)),
                   row(2 * D_MODEL)],
        out_shape=[jax.ShapeDtypeStruct((m, COL_M_END), BF16),
                   jax.ShapeDtypeStruct((m, N_GATE_COLS), F32),
                   jax.ShapeDtypeStruct((N_GATE_COLS, m), F32),
                   jax.ShapeDtypeStruct((m, MW_A), BF16),
                   jax.ShapeDtypeStruct((MW_A, m), BF16),
                   jax.ShapeDtypeStruct((m, MW_A), F32),
                   jax.ShapeDtypeStruct((m, MW_A), F32),
                   jax.ShapeDtypeStruct((m, MW_A), BF16),
                   jax.ShapeDtypeStruct((nt, tm // MOBA_BLOCK, MW_A), F32),
                   jax.ShapeDtypeStruct((m, 2 * D_MODEL), BF16)],
        compiler_params=_params("parallel"),
        name="inproj_prompt",
    )(x, g, wm, wif, wift, wa, wg)


def _inproj_sample_kernel(x_ref, g_ref, wm_ref, wif_ref, wa_ref, wg_ref,
                          m_ref, mt_ref, gc_ref, a_ref, sg_ref):
    xn = _rms(x_ref[...], g_ref[...])
    xb = xn.astype(BF16)
    mm = _dot(xb, wm_ref[...])
    m_ref[...] = mm
    mt_ref[...] = mm[:, :2 * MW_M].T
    gc_ref[...] = jnp.dot(xn, wif_ref[...], precision=HIGHEST, preferred_element_type=F32)
    a_ref[...] = _dot(xb, wa_ref[...])
    sg_ref[...] = jax.nn.sigmoid(_dot(xb, wg_ref[...])).astype(BF16)


def _inproj_sample(x, g, wm, wif, wa, wg):
    m = x.shape[0]
    return pl.pallas_call(
        _inproj_sample_kernel,
        out_shape=[jax.ShapeDtypeStruct((m, COL_M_END), F32),
                   jax.ShapeDtypeStruct((2 * MW_M, m), F32),
                   jax.ShapeDtypeStruct((m, N_GATE_COLS), F32),
                   jax.ShapeDtypeStruct((m, 3 * MW_A), F32),
                   jax.ShapeDtypeStruct((m, 2 * D_MODEL), BF16)],
        compiler_params=pltpu.CompilerParams(vmem_limit_bytes=VMEM_LIMIT),
        name="inproj_sample",
    )(x, g, wm, wif, wa, wg)


def _mlstm_prompt_kernel(m_ref, gc_ref, gr_ref, bc_ref, br_ref, gm_ref,
                         h_ref, c_ref, ms_ref):
    L = m_ref.shape[0]

    @pl.when(pl.program_id(1) == 0)
    def _():
        c_ref[...] = jnp.zeros_like(c_ref)
        ms_ref[...] = jnp.zeros_like(ms_ref)

    row = lax.broadcasted_iota(jnp.int32, (L, L), 0)
    col = lax.broadcasted_iota(jnp.int32, (L, L), 1)
    causal = col <= row
    lower = causal.astype(F32)
    upper = (row <= col).astype(F32)

    gcol = gc_ref[...] + br_ref[...]
    grow = gr_ref[...] + bc_ref[...]
    bcol = jnp.dot(lower, _log_sigmoid(gcol), precision=HIGHEST, preferred_element_type=F32)
    brow = jnp.dot(_log_sigmoid(grow), upper, precision=HIGHEST, preferred_element_type=F32)

    lane = lax.broadcasted_iota(jnp.int32, (L, DH_M), 1)
    ones_blk = jnp.where(lane == 0, 1.0, 0.0).astype(BF16)

    for h in range(H_M):
        q = m_ref[:, h * DH_M:(h + 1) * DH_M]
        k = m_ref[:, MW_M + h * DH_M:MW_M + (h + 1) * DH_M]
        v = m_ref[:, 2 * MW_M + h * DH_M:2 * MW_M + (h + 1) * DH_M]
        o = m_ref[:, 3 * MW_M + h * DH_M:3 * MW_M + (h + 1) * DH_M]
        i_col = gcol[:, h:h + 1]
        b_col = bcol[:, H_M + h:H_M + h + 1]
        i_row = grow[h:h + 1, :]
        b_row = brow[H_M + h:H_M + h + 1, :]
        b_last = b_col[L - 1:L, :]
        cext = c_ref[0, h]
        m_prev = ms_ref[0, h][0:1, 0:1]

        d = jnp.where(causal, b_col - (b_row - i_row), NEG)
        inter = b_col + m_prev
        mt = jnp.maximum(inter, jnp.max(d, axis=1, keepdims=True))
        a = jnp.exp(inter - mt)
        s = _dot_nt(q, k) * jnp.exp(d - (mt - LOG_K_SCALE))
        v_ext = jnp.concatenate([v, ones_blk], axis=1)
        nd = a * _dot(q, cext.astype(BF16)) + _dot(s.astype(BF16), v_ext)
        den = nd[:, DH_M:DH_M + 1]
        hh = nd[:, :DH_M] / jnp.maximum(jnp.abs(den), jnp.exp(-mt))

        mu = jnp.mean(hh, axis=1, keepdims=True)
        hc = hh - mu
        var = jnp.mean(hc * hc, axis=1, keepdims=True)
        hn = hc * lax.rsqrt(var + EPS) * gm_ref[:, h * DH_M:(h + 1) * DH_M]
        h_ref[:, h * DH_M:(h + 1) * DH_M] = (hn * jax.nn.sigmoid(o.astype(F32))).astype(BF16)

        g_row = b_last - b_row + i_row
        e = b_last + m_prev
        m_new = jnp.maximum(e, jnp.max(g_row, axis=1, keepdims=True))
        w_col = jnp.exp(b_last - b_col + i_col - (m_new - LOG_K_SCALE))
        kw = (k.astype(F32) * w_col).astype(BF16)
        c_ref[0, h] = jnp.exp(e - m_new) * cext + _dot_tn(kw, v_ext)
        ms_ref[0, h] = jnp.broadcast_to(m_new, ms_ref.shape[2:])


def _mlstm_prompt(mproj, gcol, grow, b_if, g_mlstm, batch):
    m = mproj.shape[0]
    L = L_CHUNK
    nc = m // batch // L
    bias_row = b_if.reshape(1, N_GATE_COLS)
    bias_col = b_if.reshape(N_GATE_COLS, 1)
    gm = g_mlstm.reshape(1, MW_M)
    return pl.pallas_call(
        _mlstm_prompt_kernel,
        grid=(batch, nc),
        in_specs=[pl.BlockSpec((L, COL_M_END), lambda b, c: (b * nc + c, 0)),
                  pl.BlockSpec((L, N_GATE_COLS), lambda b, c: (b * nc + c, 0)),
                  pl.BlockSpec((N_GATE_COLS, L), lambda b, c: (0, b * nc + c)),
                  pl.BlockSpec((N_GATE_COLS, 1), lambda b, c: (0, 0)),
                  pl.BlockSpec((1, N_GATE_COLS), lambda b, c: (0, 0)),
                  pl.BlockSpec((1, MW_M), lambda b, c: (0, 0))],
        out_specs=[pl.BlockSpec((L, MW_M), lambda b, c: (b * nc + c, 0)),
                   pl.BlockSpec((1, H_M, DH_M, 2 * DH_M), lambda b, c: (b, 0, 0, 0)),
                   pl.BlockSpec((1, H_M, 8, 128), lambda b, c: (b, 0, 0, 0))],
        out_shape=[jax.ShapeDtypeStruct((m, MW_M), BF16),
                   jax.ShapeDtypeStruct((batch, H_M, DH_M, 2 * DH_M), F32),
                   jax.ShapeDtypeStruct((batch, H_M, 8, 128), F32)],
        compiler_params=_params("parallel", "arbitrary"),
        name="mlstm_prompt",
    )(mproj, gcol, grow, bias_col, bias_row, gm)


def _mlstm_sample_kernel(m_ref, mt_ref, gc_ref, bi_ref, gm_ref, c_ref, n_ref, ms_ref,
                         h_ref, co_ref, no_ref, mo_ref):
    b = pl.program_id(0)
    nb = mt_ref.shape[1]
    onehot = lax.broadcasted_iota(jnp.int32, (1, nb), 1) == b
    g = gc_ref[0] + bi_ref[...]
    for h in range(H_M):
        sl = slice(h * DH_M, (h + 1) * DH_M)
        q_row = m_ref[0, :, h * DH_M:(h + 1) * DH_M]
        k_row = m_ref[0, :, MW_M + h * DH_M:MW_M + (h + 1) * DH_M]
        v_row = m_ref[0, :, 2 * MW_M + h * DH_M:2 * MW_M + (h + 1) * DH_M]
        o_row = m_ref[0, :, 3 * MW_M + h * DH_M:3 * MW_M + (h + 1) * DH_M]
        q_col = jnp.sum(jnp.where(onehot, mt_ref[sl, :], 0.0), axis=1, keepdims=True)
        k_col = jnp.sum(jnp.where(onehot, mt_ref[MW_M + h * DH_M:MW_M + (h + 1) * DH_M, :], 0.0),
                        axis=1, keepdims=True)
        i_pre = g[:, h:h + 1]
        log_f = _log_sigmoid(g[:, H_M + h:H_M + h + 1])
        m_prev = ms_ref[0, :, h:h + 1]
        inter = log_f + m_prev
        m_new = jnp.maximum(inter, i_pre)
        a = jnp.exp(inter - m_new)
        w = jnp.exp(i_pre - (m_new - LOG_K_SCALE))
        c = c_ref[0, h]
        n = n_ref[0, h:h + 1, :]
        s = jnp.sum(q_row * k_row, axis=1, keepdims=True) * w
        num = a * jnp.sum(c * q_col, axis=0, keepdims=True) + s * v_row
        den = a * jnp.sum(q_row * n, axis=1, keepdims=True) + s
        hh = num / jnp.maximum(jnp.abs(den), jnp.exp(-m_new))
        mu = jnp.mean(hh, axis=1, keepdims=True)
        hc = hh - mu
        var = jnp.mean(hc * hc, axis=1, keepdims=True)
        hn = hc * lax.rsqrt(var + EPS) * gm_ref[:, sl]
        h_ref[0, :, sl] = (hn * jax.nn.sigmoid(o_row)).astype(BF16)
        co_ref[0, h] = a * c + (w * k_col) * v_row
        no_ref[0, h:h + 1, :] = a * n + w * k_row
        mo_ref[0, :, h:h + 1] = m_new


def _mlstm_sample(mproj, mproj_t, gcol, b_if, g_mlstm, c0, n0, m0):
    nb = mproj.shape[0]
    return pl.pallas_call(
        _mlstm_sample_kernel,
        grid=(nb,),
        in_specs=[pl.BlockSpec((1, 1, COL_M_END), lambda b: (b, 0, 0)),
                  pl.BlockSpec((2 * MW_M, nb), lambda b: (0, 0)),
                  pl.BlockSpec((1, 1, N_GATE_COLS), lambda b: (b, 0, 0)),
                  pl.BlockSpec((1, N_GATE_COLS), lambda b: (0, 0)),
                  pl.BlockSpec((1, MW_M), lambda b: (0, 0)),
                  pl.BlockSpec((1, H_M, DH_M, DH_M), lambda b: (b, 0, 0, 0)),
                  pl.BlockSpec((1, H_M, DH_M), lambda b: (b, 0, 0)),
                  pl.BlockSpec((1, 1, H_M), lambda b: (b, 0, 0))],
        out_specs=[pl.BlockSpec((1, 1, MW_M), lambda b: (b, 0, 0)),
                   pl.BlockSpec((1, H_M, DH_M, DH_M), lambda b: (b, 0, 0, 0)),
                   pl.BlockSpec((1, H_M, DH_M), lambda b: (b, 0, 0)),
                   pl.BlockSpec((1, 1, H_M), lambda b: (b, 0, 0))],
        out_shape=[jax.ShapeDtypeStruct((nb, 1, MW_M), BF16),
                   jax.ShapeDtypeStruct((nb, H_M, DH_M, DH_M), F32),
                   jax.ShapeDtypeStruct((nb, H_M, DH_M), F32),
                   jax.ShapeDtypeStruct((nb, 1, H_M), F32)],
        compiler_params=_params("parallel"),
        name="mlstm_sample",
    )(mproj.reshape(nb, 1, COL_M_END), mproj_t, gcol.reshape(nb, 1, N_GATE_COLS),
      b_if.reshape(1, N_GATE_COLS), g_mlstm.reshape(1, MW_M), c0, n0, m0.reshape(nb, 1, H_M))


def _bias_table_kernel(rel_ref, o_ref):
    h = pl.program_id(0)
    delta = pl.program_id(1)
    r = lax.broadcasted_iota(jnp.int32, (MOBA_BLOCK, MOBA_BLOCK), 0)
    c = lax.broadcasted_iota(jnp.int32, (MOBA_BLOCK, MOBA_BLOCK), 1)
    dist = delta * MOBA_BLOCK + r - c
    bucket = _t5_bucket(dist)
    bias = jnp.zeros((MOBA_BLOCK, MOBA_BLOCK), F32)
    for kb in range(N_BUCKETS):
        bias = jnp.where(bucket == kb, rel_ref[kb, h], bias)
    o_ref[0, 0] = jnp.where(dist >= 0, bias, NEG)


def _t5_bucket(dist):
    n = jnp.maximum(dist, 0)
    max_exact = N_BUCKETS // 2
    nf = jnp.maximum(n, 1).astype(F32)
    large = max_exact + (jnp.log(nf / max_exact) / math.log(REL_MAX_DIST / max_exact)
                         * (N_BUCKETS - max_exact)).astype(jnp.int32)
    large = jnp.minimum(large, N_BUCKETS - 1)
    return jnp.where(n < max_exact, n, large)


def _bias_table(rel_bias):
    return pl.pallas_call(
        _bias_table_kernel,
        grid=(H_A, N_NEAR),
        in_specs=[pl.BlockSpec(memory_space=pltpu.SMEM)],
        out_specs=pl.BlockSpec((1, 1, MOBA_BLOCK, MOBA_BLOCK), lambda h, d: (h, d, 0, 0)),
        out_shape=jax.ShapeDtypeStruct((H_A, N_NEAR, MOBA_BLOCK, MOBA_BLOCK), F32),
        compiler_params=_params("parallel", "parallel"),
        name="bias_table",
    )(rel_bias)


def _select_topk(score, n_valid_cols, k):
    nb = score.shape[1]
    col = lax.broadcasted_iota(jnp.int32, score.shape, 1)
    past = col < n_valid_cols
    sc = jnp.where(past, score, -jnp.inf)
    sel = jnp.full(score.shape, NEG, F32)
    for _ in range(k):
        mx = jnp.max(sc, axis=1, keepdims=True)
        idx = jnp.min(jnp.where(sc == mx, col, nb), axis=1, keepdims=True)
        hit = col == idx
        sel = jnp.where(hit, 0.0, sel)
        sc = jnp.where(hit, -jnp.inf, sc)
    return jnp.where(past, sel, NEG)


def _moba_prompt_kernel(far_ref, q_ref, kt_ref, v_ref, km_ref, tab_ref, o_ref,
                        sel_scr, m_scr, l_scr, acc_scr):
    hp = pl.program_id(1)
    qi = pl.program_id(2)
    tq = q_ref.shape[0]
    nb = km_ref.shape[1]
    col = lax.broadcasted_iota(jnp.int32, (tq, nb), 1)

    for h in range(2):
        qh = q_ref[:, h * DH_A:(h + 1) * DH_A]
        kmh = km_ref[0, :, h * DH_A:(h + 1) * DH_A]
        score = _dot_nt(qh.astype(F32), kmh, precision=HIGHEST)
        sel = _select_topk(score, qi, MOBA_TOPK)
        sel = jnp.where(col == qi, 0.0, sel)
        sel_scr[h] = sel + jnp.where(qi - col >= N_NEAR, far_ref[2 * hp + h], 0.0)
        m_scr[h] = jnp.full(m_scr.shape[1:], NEG, F32)
        l_scr[h] = jnp.zeros(l_scr.shape[1:], F32)
        acc_scr[h] = jnp.zeros(acc_scr.shape[1:], F32)

    def body(j, carry):
        start = pl.multiple_of(j * MOBA_BLOCK, MOBA_BLOCK)
        delta = jnp.minimum(qi - j, N_NEAR)
        vblk = v_ref[pl.ds(start, MOBA_BLOCK), :]
        for h in range(2):
            qh = q_ref[:, h * DH_A:(h + 1) * DH_A]
            ktb = kt_ref[h * DH_A:(h + 1) * DH_A, pl.ds(start, MOBA_BLOCK)]
            selcol = jnp.sum(jnp.where(col == j, sel_scr[h], 0.0), axis=1, keepdims=True)
            s = _dot(qh, ktb) + selcol + tab_ref[h, delta]
            m_old = m_scr[h]
            m_new = jnp.maximum(m_old, jnp.max(s, axis=1, keepdims=True))
            alpha = jnp.exp(m_old - m_new)
            p = jnp.exp(s - m_new)
            l_scr[h] = alpha * l_scr[h] + jnp.sum(p, axis=1, keepdims=True)
            acc_scr[h] = alpha * acc_scr[h] + _dot(p.astype(BF16), vblk)
            m_scr[h] = m_new
        return carry

    lax.fori_loop(0, qi + 1, body, 0)

    lane = lax.broadcasted_iota(jnp.int32, (tq, 2 * DH_A), 1)
    out = jnp.where(lane < DH_A, acc_scr[0] / l_scr[0], acc_scr[1] / l_scr[1])
    o_ref[...] = out.astype(BF16)


def _moba_prompt(q, kt, vb, kmean, table, far, batch):
    m = q.shape[0]
    seq = m // batch
    nb = seq // MOBA_BLOCK
    tq = MOBA_BLOCK
    hp_n = H_A // 2
    return pl.pallas_call(
        _moba_prompt_kernel,
        grid_spec=pltpu.PrefetchScalarGridSpec(
            num_scalar_prefetch=0,
            grid=(batch, hp_n, nb),
            in_specs=[pl.BlockSpec(memory_space=pltpu.SMEM),
                      pl.BlockSpec((tq, 2 * DH_A), lambda b, hp, qi: (b * nb + qi, hp)),
                      pl.BlockSpec((2 * DH_A, seq), lambda b, hp, qi: (hp, b)),
                      pl.BlockSpec((seq, 2 * DH_A), lambda b, hp, qi: (b, hp)),
                      pl.BlockSpec((1, nb, 2 * DH_A), lambda b, hp, qi: (b, 0, hp)),
                      pl.BlockSpec((2, N_NEAR + 1, tq, tq), lambda b, hp, qi: (hp, 0, 0, 0))],
            out_specs=pl.BlockSpec((tq, 2 * DH_A), lambda b, hp, qi: (b * nb + qi, hp)),
            scratch_shapes=[pltpu.VMEM((2, tq, nb), F32),
                            pltpu.VMEM((2, tq, 1), F32),
                            pltpu.VMEM((2, tq, 1), F32),
                            pltpu.VMEM((2, tq, 2 * DH_A), F32)]),
        out_shape=jax.ShapeDtypeStruct((m, MW_A), BF16),
        compiler_params=_params("parallel", "parallel", "arbitrary"),
        name="moba_prompt",
    )(far, q, kt, vb, kmean.reshape(batch, nb, MW_A), table)


def _kmean_sample_kernel(pt_ref, *refs):
    o_ref = refs[-1]
    per_block = MOBA_BLOCK // PAGE_SIZE
    for r in range(PAGES_PER_STEP // per_block):
        acc = jnp.sum(refs[r * per_block][0, 0], axis=0)
        for jx in range(1, per_block):
            acc = acc + jnp.sum(refs[r * per_block + jx][0, 0], axis=0)
        o_ref[0, r] = acc * (1.0 / MOBA_BLOCK)


def _kmean_sample(cache_k, page_table, layer):
    nb_s, n_pages = page_table.shape
    steps = n_pages // PAGES_PER_STEP
    rows = PAGES_PER_STEP * PAGE_SIZE // MOBA_BLOCK

    def page_spec(i):
        return pl.BlockSpec((1, 1, PAGE_SIZE, H_A, DH_A),
                            lambda b, s, pt: (layer, pt[b, s * PAGES_PER_STEP + i], 0, 0, 0))

    return pl.pallas_call(
        _kmean_sample_kernel,
        grid_spec=pltpu.PrefetchScalarGridSpec(
            num_scalar_prefetch=1,
            grid=(nb_s, steps),
            in_specs=[page_spec(i) for i in range(PAGES_PER_STEP)],
            out_specs=pl.BlockSpec((1, rows, H_A, DH_A), lambda b, s, pt: (b, s, 0, 0))),
        out_shape=jax.ShapeDtypeStruct((nb_s, steps * rows, H_A, DH_A), F32),
        compiler_params=_params("parallel", "arbitrary"),
        name="kmean_sample",
    )(page_table, *([cache_k] * PAGES_PER_STEP))


def _select_sample_kernel(q_ref, km_ref, o_ref):
    nblk = km_ref.shape[1]
    sc = jnp.sum(km_ref[...] * q_ref[...][:, None], axis=-1)
    blk = lax.broadcasted_iota(jnp.int32, sc.shape, 1)
    for t in range(MOBA_TOPK):
        mx = jnp.max(sc, axis=1, keepdims=True)
        idx = jnp.min(jnp.where(sc == mx, blk, nblk), axis=1, keepdims=True)
        o_ref[:, t:t + 1, :] = idx
        sc = jnp.where(blk == idx, -jnp.inf, sc)


def _select_sample(q, kmean):
    nb_s = q.shape[0]
    return pl.pallas_call(
        _select_sample_kernel,
        out_shape=jax.ShapeDtypeStruct((nb_s, MOBA_TOPK, H_A), jnp.int32),
        compiler_params=pltpu.CompilerParams(vmem_limit_bytes=VMEM_LIMIT),
        name="select_sample",
    )(q, kmean)


def _attend_sample_kernel(pt_ref, sel_ref, rel_ref, qkv_ref, *refs):
    o_ref = refs[-1]
    n_sel_pages = (len(refs) - 1) // 2
    k_refs, v_refs = refs[:n_sel_pages], refs[n_sel_pages:2 * n_sel_pages]
    per_block = MOBA_BLOCK // PAGE_SIZE
    b = pl.program_id(0)
    h = pl.program_id(1)
    past_len = pt_ref.shape[1] * PAGE_SIZE
    rows = lax.broadcasted_iota(jnp.int32, (PAGE_SIZE, 1), 0)

    def bias_of(dist):
        bucket = _t5_bucket(dist)
        bias = jnp.zeros(dist.shape, F32)
        for kb in range(N_BUCKETS):
            bias = jnp.where(bucket == kb, rel_ref[kb, h], bias)
        return bias

    def attend(hh):
        q = qkv_ref[0, 0, hh:hh + 1, :] * Q_SCALE
        k_new = qkv_ref[0, 1, hh:hh + 1, :]
        v_new = qkv_ref[0, 2, hh:hh + 1, :]
        logits = []
        for t in range(n_sel_pages):
            blk = sel_ref[b, t // per_block, h]
            kpos = blk * MOBA_BLOCK + (t % per_block) * PAGE_SIZE + rows
            kh = k_refs[t][0, 0, :, hh, :]
            logits.append(jnp.sum(kh * q, axis=1, keepdims=True) + bias_of(past_len - kpos))
        own = jnp.sum(k_new * q, axis=1, keepdims=True) + bias_of(jnp.zeros((1, 1), jnp.int32))
        mx = own
        for lg in logits:
            mx = jnp.maximum(mx, jnp.max(lg, axis=0, keepdims=True))
        p_own = jnp.exp(own - mx)
        den = p_own
        acc = p_own * v_new
        for t in range(n_sel_pages):
            p = jnp.exp(logits[t] - mx)
            den = den + jnp.sum(p, axis=0, keepdims=True)
            acc = acc + jnp.sum(p * v_refs[t][0, 0, :, hh, :], axis=0, keepdims=True)
        o_ref[0, hh:hh + 1, :] = acc / den

    for hh in range(H_A):
        pl.when(h == hh)(functools.partial(attend, hh))


def _attend_sample(cache_k, cache_v, page_table, sel, rel_bias, qkv, layer):
    nb_s = page_table.shape[0]
    per_block = MOBA_BLOCK // PAGE_SIZE
    n_sel_pages = MOBA_TOPK * per_block

    def page_spec(t):
        def imap(b, h, pt, sl, rel):
            page = sl[b, t // per_block, h] * per_block + (t % per_block)
            return (layer, pt[b, page], 0, 0, 0)
        return pl.BlockSpec((1, 1, PAGE_SIZE, H_A, DH_A), imap)

    return pl.pallas_call(
        _attend_sample_kernel,
        grid_spec=pltpu.PrefetchScalarGridSpec(
            num_scalar_prefetch=3,
            grid=(nb_s, H_A),
            in_specs=[pl.BlockSpec((1, 3, H_A, DH_A), lambda b, h, pt, sl, rel: (b, 0, 0, 0))]
                     + [page_spec(t) for t in range(n_sel_pages)] * 2,
            out_specs=pl.BlockSpec((1, H_A, DH_A), lambda b, h, pt, sl, rel: (b, 0, 0))),
        out_shape=jax.ShapeDtypeStruct((nb_s, H_A, DH_A), F32),
        compiler_params=_params("parallel", "arbitrary"),
        name="attend_sample",
    )(page_table, sel, rel_bias, qkv, *([cache_k] * n_sel_pages), *([cache_v] * n_sel_pages))


def _mix_kernel(x_ref, hm_ref, ha_ref, sg_ref, wbm_ref, wba_ref, wo_ref, g_ref, o_ref):
    sg = sg_ref[...].astype(F32)
    mixed = (sg[:, :D_MODEL] * _dot(hm_ref[...], wbm_ref[...])
             + sg[:, D_MODEL:] * _dot(ha_ref[...], wba_ref[...]))
    y = _dot(mixed.astype(BF16), wo_ref[...])
    o_ref[...] = x_ref[...] + _rms(y, g_ref[...])


def _mix(x, hm, ha, sg, wbm, wba, wo, g):
    m = x.shape[0]
    tm = min(TM_PROJ, m)
    row = lambda w: pl.BlockSpec((tm, w), lambda i: (i, 0))
    full = lambda a: pl.BlockSpec(a.shape, lambda i: (0,) * a.ndim)
    return pl.pallas_call(
        _mix_kernel,
        grid=(m // tm,),
        in_specs=[row(D_MODEL), row(MW_M), row(MW_A), row(2 * D_MODEL),
                  full(wbm), full(wba), full(wo), full(g)],
        out_specs=row(D_MODEL),
        out_shape=jax.ShapeDtypeStruct((m, D_MODEL), F32),
        compiler_params=_params("parallel"),
        name="mix",
    )(x, hm, ha, sg, wbm, wba, wo, g)


def _ffn_kernel(x_ref, gpre_ref, gpost_ref, wg_ref, wu_ref, wd_ref, o_ref):
    x = x_ref[...]
    hf = _rms(x, gpre_ref[...]).astype(BF16)
    acc = jnp.zeros(x.shape, F32)
    for c in range(D_FF // FF_CHUNK):
        sl = slice(c * FF_CHUNK, (c + 1) * FF_CHUNK)
        gate = _dot(hf, wg_ref[:, sl])
        up = _dot(hf, wu_ref[:, sl])
        mid = (gate * jax.nn.sigmoid(gate) * up).astype(BF16)
        acc = acc + _dot(mid, wd_ref[sl, :])
    o_ref[...] = x + _rms(acc, gpost_ref[...])


def _ffn(x, gpre, gpost, wg, wu, wd):
    m = x.shape[0]
    tm = min(TM_FFN, m)
    row = pl.BlockSpec((tm, D_MODEL), lambda i: (i, 0))
    full = lambda a: pl.BlockSpec(a.shape, lambda i: (0,) * a.ndim)
    return pl.pallas_call(
        _ffn_kernel,
        grid=(m // tm,),
        in_specs=[row, full(gpre), full(gpost), full(wg), full(wu), full(wd)],
        out_specs=row,
        out_shape=jax.ShapeDtypeStruct((m, D_MODEL), F32),
        compiler_params=_params("parallel"),
        name="ffn",
    )(x, gpre, gpost, wg, wu, wd)


def kernel(x_prompt, x_sample, cache_k, cache_v, state_C, state_n, state_m, page_table,
           norm_mix_pre, norm_mix_post, norm_ffn_pre, norm_ffn_post, w_in, b_igate, b_fgate,
           g_mlstm, w_branch_m, w_branch_a, w_out, w_ffn_gate, w_ffn_up, w_ffn_down, rel_bias):
    bp, seq, _ = x_prompt.shape
    bs = x_sample.shape[0]
    depth = w_in.shape[0]
    xp = x_prompt.reshape(bp * seq, D_MODEL)
    xs = x_sample.reshape(bs, D_MODEL)

    table = _bias_table(rel_bias)
    table = jnp.concatenate([table, jnp.zeros((H_A, 1, MOBA_BLOCK, MOBA_BLOCK), F32)], axis=1)
    far = rel_bias[N_BUCKETS - 1]

    outs = {k: [] for k in ("kp", "vp", "cp", "np", "mp", "ks", "vs", "cs", "ns", "ms")}
    for l in range(depth):
        row = lambda a: a[l].reshape(1, -1)
        wm = w_in[l][:, :COL_M_END].astype(BF16)
        wif = w_in[l][:, COL_M_END:COL_IF_END]
        wa = w_in[l][:, COL_IF_END:COL_A_END].astype(BF16)
        wgt = w_in[l][:, COL_A_END:].astype(BF16)
        b_if = jnp.concatenate([b_igate[l], b_fgate[l]])
        wbm, wba, wo = (w_branch_m[l].astype(BF16), w_branch_a[l].astype(BF16), w_out[l].astype(BF16))
        wfg, wfu, wfd = (w_ffn_gate[l].astype(BF16), w_ffn_up[l].astype(BF16), w_ffn_down[l].astype(BF16))

        mproj, gcol, grow, q, kt, k32, v32, vb, kmean, sg = _inproj_prompt(
            xp, row(norm_mix_pre), wm, wif, wif.T, wa, wgt)
        hm, cext, mstate = _mlstm_prompt(mproj, gcol, grow, b_if, g_mlstm[l], bp)
        ha = _moba_prompt(q, kt, vb, kmean, table, far, bp)
        xp = _mix(xp, hm, ha, sg, wbm, wba, wo, row(norm_mix_post))
        xp = _ffn(xp, row(norm_ffn_pre), row(norm_ffn_post), wfg, wfu, wfd)
        outs["kp"].append(k32.reshape(bp, seq, H_A, DH_A))
        outs["vp"].append(v32.reshape(bp, seq, H_A, DH_A))
        outs["cp"].append(cext[..., :DH_M])
        outs["np"].append(cext[..., DH_M])
        outs["mp"].append(mstate[:, :, 0, 0])

        mproj_s, mproj_t, gcol_s, a_s, sg_s = _inproj_sample(xs, row(norm_mix_pre), wm, wif, wa, wgt)
        hm_s, c_s, n_s, m_s = _mlstm_sample(mproj_s, mproj_t, gcol_s, b_if, g_mlstm[l],
                                            state_C[l], state_n[l], state_m[l])
        qkv_s = a_s.reshape(bs, 3, H_A, DH_A)
        kmean_s = _kmean_sample(cache_k, page_table, l)
        sel = _select_sample(qkv_s[:, 0], kmean_s)
        ha_s = _attend_sample(cache_k, cache_v, page_table, sel, rel_bias, qkv_s, l)
        xs = _mix(xs, hm_s.reshape(bs, MW_M), ha_s.reshape(bs, MW_A).astype(BF16), sg_s,
                  wbm, wba, wo, row(norm_mix_post))
        xs = _ffn(xs, row(norm_ffn_pre), row(norm_ffn_post), wfg, wfu, wfd)
        outs["ks"].append(a_s[:, MW_A:2 * MW_A].reshape(bs, 1, H_A, DH_A))
        outs["vs"].append(a_s[:, 2 * MW_A:].reshape(bs, 1, H_A, DH_A))
        outs["cs"].append(c_s)
        outs["ns"].append(n_s)
        outs["ms"].append(m_s.reshape(bs, H_M))

    st = lambda k: jnp.stack(outs[k])
    return (xp.reshape(bp, seq, D_MODEL), xs.reshape(bs, 1, D_MODEL),
            st("kp"), st("vp"), st("cp"), st("np"), st("mp"),
            st("ks"), st("vs"), st("cs"), st("ns"), st("ms"))
```

```python
import math

import jax
import jax.numpy as jnp
from jax import lax
from jax.experimental import pallas as pl
from jax.experimental.pallas import tpu as pltpu

F32 = jnp.float32
BF16 = jnp.bfloat16
HIGHEST = lax.Precision.HIGHEST

D_MODEL = 1024
H_M, DH_M = 4, 128
MW_M = H_M * DH_M
H_A, DH_A = 8, 64
MW_A = H_A * DH_A
MOBA_BLOCK = 256
MOBA_TOPK = 3
N_BUCKETS = 32
REL_MAX_DIST = 2048
D_FF = 2816
EPS = 1e-6
NEG = -1e30
PAGE_SIZE = 128

N_GATE_COLS = 2 * H_M
COL_M_END = 4 * MW_M
COL_IF_END = COL_M_END + N_GATE_COLS
COL_Q_END = COL_IF_END + MW_A
COL_K_END = COL_Q_END + MW_A
COL_V_END = COL_K_END + MW_A
LOG_K_SCALE = math.log(DH_M ** -0.5)
Q_SCALE = DH_A ** -0.5
LOG2E = math.log2(math.e)

N_NEAR = (REL_MAX_DIST + MOBA_BLOCK - 1) // MOBA_BLOCK + 1
assert (N_NEAR * MOBA_BLOCK - (MOBA_BLOCK - 1)) >= REL_MAX_DIST

PAIR = 2 * DH_A
K_AUG = 2 * PAIR
SEL_HI, SEL_LO = PAIR, PAIR + 32
V_ROWS = PAIR + 16
MAX_BLOCKS = 32

VMEM_LIMIT = 56 * 1024 * 1024

TM_PROJ = 512
TM_FFN = 256
L_CHUNK = 256
FF_CHUNK = 1408
PAGES_PER_STEP = 16


def _params(*sem):
    return pltpu.CompilerParams(dimension_semantics=sem, vmem_limit_bytes=VMEM_LIMIT)


def _rms(x, g):
    return x * lax.rsqrt(jnp.mean(x * x, axis=-1, keepdims=True) + EPS) * g


def _log_sigmoid(x):
    return jnp.minimum(x, 0.0) - jnp.log(1.0 + jnp.exp(-jnp.abs(x)))


def _dot(a, b, precision=None):
    return jnp.dot(a, b, precision=precision, preferred_element_type=F32)


def _dot_nt(a, b, precision=None):
    return lax.dot_general(a, b, (((1,), (1,)), ((), ())), precision=precision,
                           preferred_element_type=F32)


def _dot_tn(a, b):
    return lax.dot_general(a, b, (((0,), (0,)), ((), ())), preferred_element_type=F32)


def _inproj_prompt_kernel(x_ref, g_ref, wm_ref, wif_ref, wq_ref, wk_ref, wv_ref, wg_ref,
                          m_ref, gc_ref, gr_ref, qt_ref, ka_ref, kt32_ref, vt32_ref, vte_ref,
                          km_ref, sg_ref):
    tm = x_ref.shape[0]
    xn = _rms(x_ref[...], g_ref[...])
    xb = xn.astype(BF16)
    m_ref[...] = _dot_nt(xb, wm_ref[...]).astype(BF16)
    gc_ref[...] = _dot_nt(xn, wif_ref[...], HIGHEST)
    gr_ref[...] = _dot_nt(wif_ref[...], xn, HIGHEST)
    qt_ref[...] = (_dot_nt(wq_ref[...], xb) * (Q_SCALE * LOG2E)).astype(BF16)

    k = _dot_nt(xb, wk_ref[...])
    kt32_ref[0] = k.T
    km_ref[0] = jnp.mean(k.reshape(tm // MOBA_BLOCK, MOBA_BLOCK, MW_A), axis=1)
    row = lax.broadcasted_iota(jnp.int32, (tm, PAIR), 0)
    lane = lax.broadcasted_iota(jnp.int32, (tm, PAIR), 1)
    blk = (pl.program_id(1) * tm + row) // MOBA_BLOCK
    onehot = jnp.where(jnp.logical_and(lane < 2 * MAX_BLOCKS, lane % MAX_BLOCKS == blk),
                       1.0, 0.0).astype(BF16)
    kb = k.astype(BF16)
    for p in range(H_A // 2):
        ka_ref[:, p * K_AUG:p * K_AUG + PAIR] = kb[:, p * PAIR:(p + 1) * PAIR]
        ka_ref[:, p * K_AUG + PAIR:(p + 1) * K_AUG] = onehot

    vt = _dot_nt(wv_ref[...], xb)
    vt32_ref[0] = vt
    ones_rows = jnp.where(lax.broadcasted_iota(jnp.int32, (V_ROWS - PAIR, tm), 0) == 0,
                          1.0, 0.0).astype(BF16)
    for p in range(H_A // 2):
        vte_ref[p * V_ROWS:p * V_ROWS + PAIR, :] = vt[p * PAIR:(p + 1) * PAIR].astype(BF16)
        vte_ref[p * V_ROWS + PAIR:(p + 1) * V_ROWS, :] = ones_rows
    sg_ref[...] = jax.nn.sigmoid(_dot_nt(xb, wg_ref[...])).astype(BF16)


def _inproj_prompt(x, g, wm, wif, wq, wk, wv, wg, batch):
    m = x.shape[0]
    seq = m // batch
    tm = TM_PROJ
    nt = seq // tm
    n_pairs = H_A // 2
    row = lambda w: pl.BlockSpec((tm, w), lambda b, t: (b * nt + t, 0))
    col = lambda h: pl.BlockSpec((h, tm), lambda b, t: (0, b * nt + t))
    full = lambda a: pl.BlockSpec(a.shape, lambda b, t: (0,) * a.ndim)
    return pl.pallas_call(
        _inproj_prompt_kernel,
        grid=(batch, nt),
        in_specs=[row(D_MODEL), full(g), full(wm), full(wif), full(wq), full(wk), full(wv), full(wg)],
        out_specs=[row(COL_M_END), row(N_GATE_COLS), col(N_GATE_COLS), col(MW_A),
                   row(n_pairs * K_AUG),
                   pl.BlockSpec((1, MW_A, tm), lambda b, t: (b, 0, t)),
                   pl.BlockSpec((1, MW_A, tm), lambda b, t: (b, 0, t)),
                   col(n_pairs * V_ROWS),
                   pl.BlockSpec((1, tm // MOBA_BLOCK, MW_A), lambda b, t: (b * nt + t, 0, 0)),
                   row(2 * D_MODEL)],
        out_shape=[jax.ShapeDtypeStruct((m, COL_M_END), BF16),
                   jax.ShapeDtypeStruct((m, N_GATE_COLS), F32),
                   jax.ShapeDtypeStruct((N_GATE_COLS, m), F32),
                   jax.ShapeDtypeStruct((MW_A, m), BF16),
                   jax.ShapeDtypeStruct((m, n_pairs * K_AUG), BF16),
                   jax.ShapeDtypeStruct((batch, MW_A, seq), F32),
                   jax.ShapeDtypeStruct((batch, MW_A, seq), F32),
                   jax.ShapeDtypeStruct((n_pairs * V_ROWS, m), BF16),
                   jax.ShapeDtypeStruct((m // tm, tm // MOBA_BLOCK, MW_A), F32),
                   jax.ShapeDtypeStruct((m, 2 * D_MODEL), BF16)],
        compiler_params=_params("parallel", "parallel"),
        name="inproj_prompt",
    )(x, g, wm, wif, wq, wk, wv, wg)


def _inproj_sample_kernel(x_ref, g_ref, wm_ref, wif_ref, wq_ref, wk_ref, wv_ref, wg_ref,
                          m_ref, mt_ref, gc_ref, qkv_ref, qt_ref, sg_ref):
    xn = _rms(x_ref[...], g_ref[...])
    xb = xn.astype(BF16)
    m_ref[...] = _dot_nt(xb, wm_ref[...])
    mt_ref[...] = _dot_nt(wm_ref[:2 * MW_M, :], xb)
    gc_ref[...] = _dot_nt(xn, wif_ref[...], HIGHEST)
    qkv_ref[0] = _dot_nt(xb, wq_ref[...])
    qkv_ref[1] = _dot_nt(xb, wk_ref[...])
    qkv_ref[2] = _dot_nt(xb, wv_ref[...])
    qt_ref[...] = _dot_nt(wq_ref[...], xb)
    sg_ref[...] = jax.nn.sigmoid(_dot_nt(xb, wg_ref[...])).astype(BF16)


def _inproj_sample(x, g, wm, wif, wq, wk, wv, wg):
    m = x.shape[0]
    return pl.pallas_call(
        _inproj_sample_kernel,
        out_shape=[jax.ShapeDtypeStruct((m, COL_M_END), F32),
                   jax.ShapeDtypeStruct((2 * MW_M, m), F32),
                   jax.ShapeDtypeStruct((m, N_GATE_COLS), F32),
                   jax.ShapeDtypeStruct((3, m, MW_A), F32),
                   jax.ShapeDtypeStruct((MW_A, m), F32),
                   jax.ShapeDtypeStruct((m, 2 * D_MODEL), BF16)],
        compiler_params=pltpu.CompilerParams(vmem_limit_bytes=VMEM_LIMIT),
        name="inproj_sample",
    )(x, g, wm, wif, wq, wk, wv, wg)


def _mlstm_prompt_kernel(m_ref, gc_ref, gr_ref, bc_ref, br_ref, gm_ref,
                         h_ref, c_ref, ms_ref):
    L = m_ref.shape[0]

    @pl.when(pl.program_id(1) == 0)
    def _():
        c_ref[...] = jnp.zeros_like(c_ref)
        ms_ref[...] = jnp.zeros_like(ms_ref)

    row = lax.broadcasted_iota(jnp.int32, (L, L), 0)
    col = lax.broadcasted_iota(jnp.int32, (L, L), 1)
    causal = col <= row
    lower = causal.astype(F32)
    upper = (row <= col).astype(F32)

    gcol = gc_ref[...] + br_ref[...]
    grow = gr_ref[...] + bc_ref[...]
    bcol = _dot(lower, _log_sigmoid(gcol), HIGHEST)
    brow = _dot(_log_sigmoid(grow), upper, HIGHEST)

    lane = lax.broadcasted_iota(jnp.int32, (L, DH_M), 1)
    ones_blk = jnp.where(lane == 0, 1.0, 0.0).astype(BF16)

    for h in range(H_M):
        q = m_ref[:, h * DH_M:(h + 1) * DH_M]
        k = m_ref[:, MW_M + h * DH_M:MW_M + (h + 1) * DH_M]
        v = m_ref[:, 2 * MW_M + h * DH_M:2 * MW_M + (h + 1) * DH_M]
        o = m_ref[:, 3 * MW_M + h * DH_M:3 * MW_M + (h + 1) * DH_M]
        i_col = gcol[:, h:h + 1]
        b_col = bcol[:, H_M + h:H_M + h + 1]
        i_row = grow[h:h + 1, :]
        b_row = brow[H_M + h:H_M + h + 1, :]
        b_last = b_col[L - 1:L, :]
        cext = c_ref[0, h]
        m_prev = ms_ref[0, h][0:1, 0:1]

        d = jnp.where(causal, b_col - (b_row - i_row), NEG)
        inter = b_col + m_prev
        mt = jnp.maximum(inter, jnp.max(d, axis=1, keepdims=True))
        a = jnp.exp(inter - mt)
        s = _dot_nt(q, k) * jnp.exp(d - (mt - LOG_K_SCALE))
        v_ext = jnp.concatenate([v, ones_blk], axis=1)
        nd = a * _dot(q, cext.astype(BF16)) + _dot(s.astype(BF16), v_ext)
        den = nd[:, DH_M:DH_M + 1]
        hh = nd[:, :DH_M] / jnp.maximum(jnp.abs(den), jnp.exp(-mt))

        mu = jnp.mean(hh, axis=1, keepdims=True)
        hc = hh - mu
        var = jnp.mean(hc * hc, axis=1, keepdims=True)
        hn = hc * lax.rsqrt(var + EPS) * gm_ref[:, h * DH_M:(h + 1) * DH_M]
        h_ref[:, h * DH_M:(h + 1) * DH_M] = (hn * jax.nn.sigmoid(o.astype(F32))).astype(BF16)

        g_row = b_last - b_row + i_row
        e = b_last + m_prev
        m_new = jnp.maximum(e, jnp.max(g_row, axis=1, keepdims=True))
        w_col = jnp.exp(b_last - b_col + i_col - (m_new - LOG_K_SCALE))
        kw = (k.astype(F32) * w_col).astype(BF16)
        c_ref[0, h] = jnp.exp(e - m_new) * cext + _dot_tn(kw, v_ext)
        ms_ref[0, h] = jnp.broadcast_to(m_new, ms_ref.shape[2:])


def _mlstm_prompt(mproj, gcol, grow, b_if, g_mlstm, batch):
    m = mproj.shape[0]
    L = L_CHUNK
    nc = m // batch // L
    bias_row = b_if.reshape(1, N_GATE_COLS)
    bias_col = b_if.reshape(N_GATE_COLS, 1)
    gm = g_mlstm.reshape(1, MW_M)
    return pl.pallas_call(
        _mlstm_prompt_kernel,
        grid=(batch, nc),
        in_specs=[pl.BlockSpec((L, COL_M_END), lambda b, c: (b * nc + c, 0)),
                  pl.BlockSpec((L, N_GATE_COLS), lambda b, c: (b * nc + c, 0)),
                  pl.BlockSpec((N_GATE_COLS, L), lambda b, c: (0, b * nc + c)),
                  pl.BlockSpec((N_GATE_COLS, 1), lambda b, c: (0, 0)),
                  pl.BlockSpec((1, N_GATE_COLS), lambda b, c: (0, 0)),
                  pl.BlockSpec((1, MW_M), lambda b, c: (0, 0))],
        out_specs=[pl.BlockSpec((L, MW_M), lambda b, c: (b * nc + c, 0)),
                   pl.BlockSpec((1, H_M, DH_M, 2 * DH_M), lambda b, c: (b, 0, 0, 0)),
                   pl.BlockSpec((1, H_M, 8, 128), lambda b, c: (b, 0, 0, 0))],
        out_shape=[jax.ShapeDtypeStruct((m, MW_M), BF16),
                   jax.ShapeDtypeStruct((batch, H_M, DH_M, 2 * DH_M), F32),
                   jax.ShapeDtypeStruct((batch, H_M, 8, 128), F32)],
        compiler_params=_params("parallel", "arbitrary"),
        name="mlstm_prompt",
    )(mproj, gcol, grow, bias_col, bias_row, gm)


def _mlstm_sample_kernel(m_ref, mt_ref, gc_ref, bi_ref, gm_ref, c_ref, n_ref, ms_ref,
                         h_ref, co_ref, no_ref, mo_ref):
    b = pl.program_id(0)
    nb = mt_ref.shape[1]
    onehot = lax.broadcasted_iota(jnp.int32, (1, nb), 1) == b
    g = gc_ref[0] + bi_ref[...]
    for h in range(H_M):
        sl = slice(h * DH_M, (h + 1) * DH_M)
        q_row = m_ref[0, :, h * DH_M:(h + 1) * DH_M]
        k_row = m_ref[0, :, MW_M + h * DH_M:MW_M + (h + 1) * DH_M]
        v_row = m_ref[0, :, 2 * MW_M + h * DH_M:2 * MW_M + (h + 1) * DH_M]
        o_row = m_ref[0, :, 3 * MW_M + h * DH_M:3 * MW_M + (h + 1) * DH_M]
        q_col = jnp.sum(jnp.where(onehot, mt_ref[sl, :], 0.0), axis=1, keepdims=True)
        k_col = jnp.sum(jnp.where(onehot, mt_ref[MW_M + h * DH_M:MW_M + (h + 1) * DH_M, :], 0.0),
                        axis=1, keepdims=True)
        i_pre = g[:, h:h + 1]
        log_f = _log_sigmoid(g[:, H_M + h:H_M + h + 1])
        m_prev = ms_ref[0, :, h:h + 1]
        inter = log_f + m_prev
        m_new = jnp.maximum(inter, i_pre)
        a = jnp.exp(inter - m_new)
        w = jnp.exp(i_pre - (m_new - LOG_K_SCALE))
        c = c_ref[0, h]
        n = n_ref[0, h:h + 1, :]
        s = jnp.sum(q_row * k_row, axis=1, keepdims=True) * w
        num = a * jnp.sum(c * q_col, axis=0, keepdims=True) + s * v_row
        den = a * jnp.sum(q_row * n, axis=1, keepdims=True) + s
        hh = num / jnp.maximum(jnp.abs(den), jnp.exp(-m_new))
        mu = jnp.mean(hh, axis=1, keepdims=True)
        hc = hh - mu
        var = jnp.mean(hc * hc, axis=1, keepdims=True)
        hn = hc * lax.rsqrt(var + EPS) * gm_ref[:, sl]
        h_ref[0, :, sl] = (hn * jax.nn.sigmoid(o_row)).astype(BF16)
        co_ref[0, h] = a * c + (w * k_col) * v_row
        no_ref[0, h:h + 1, :] = a * n + w * k_row
        mo_ref[0, :, h:h + 1] = m_new


def _mlstm_sample(mproj, mproj_t, gcol, b_if, g_mlstm, c0, n0, m0):
    nb = mproj.shape[0]
    return pl.pallas_call(
        _mlstm_sample_kernel,
        grid=(nb,),
        in_specs=[pl.BlockSpec((1, 1, COL_M_END), lambda b: (b, 0, 0)),
                  pl.BlockSpec((2 * MW_M, nb), lambda b: (0, 0)),
                  pl.BlockSpec((1, 1, N_GATE_COLS), lambda b: (b, 0, 0)),
                  pl.BlockSpec((1, N_GATE_COLS), lambda b: (0, 0)),
                  pl.BlockSpec((1, MW_M), lambda b: (0, 0)),
                  pl.BlockSpec((1, H_M, DH_M, DH_M), lambda b: (b, 0, 0, 0)),
                  pl.BlockSpec((1, H_M, DH_M), lambda b: (b, 0, 0)),
                  pl.BlockSpec((1, 1, H_M), lambda b: (b, 0, 0))],
        out_specs=[pl.BlockSpec((1, 1, MW_M), lambda b: (b, 0, 0)),
                   pl.BlockSpec((1, H_M, DH_M, DH_M), lambda b: (b, 0, 0, 0)),
                   pl.BlockSpec((1, H_M, DH_M), lambda b: (b, 0, 0)),
                   pl.BlockSpec((1, 1, H_M), lambda b: (b, 0, 0))],
        out_shape=[jax.ShapeDtypeStruct((nb, 1, MW_M), BF16),
                   jax.ShapeDtypeStruct((nb, H_M, DH_M, DH_M), F32),
                   jax.ShapeDtypeStruct((nb, H_M, DH_M), F32),
                   jax.ShapeDtypeStruct((nb, 1, H_M), F32)],
        compiler_params=_params("parallel"),
        name="mlstm_sample",
    )(mproj.reshape(nb, 1, COL_M_END), mproj_t, gcol.reshape(nb, 1, N_GATE_COLS),
      b_if.reshape(1, N_GATE_COLS), g_mlstm.reshape(1, MW_M), c0, n0, m0.reshape(nb, 1, H_M))


def _t5_bucket(dist):
    n = jnp.maximum(dist, 0)
    max_exact = N_BUCKETS // 2
    nf = jnp.maximum(n, 1).astype(F32)
    large = max_exact + (jnp.log(nf / max_exact) / math.log(REL_MAX_DIST / max_exact)
                         * (N_BUCKETS - max_exact)).astype(jnp.int32)
    large = jnp.minimum(large, N_BUCKETS - 1)
    return jnp.where(n < max_exact, n, large)


def _t5_bias(dist, rel_ref, h):
    bucket = _t5_bucket(dist)
    bias = jnp.zeros(dist.shape, F32)
    for kb in range(N_BUCKETS):
        bias = jnp.where(bucket == kb, rel_ref[kb, h], bias)
    return bias


def _bias_table_kernel(rel_ref, o_ref):
    h = pl.program_id(0)
    delta = pl.program_id(1)
    c = lax.broadcasted_iota(jnp.int32, (MOBA_BLOCK, MOBA_BLOCK), 0)
    r = lax.broadcasted_iota(jnp.int32, (MOBA_BLOCK, MOBA_BLOCK), 1)
    dist = delta * MOBA_BLOCK + r - c
    o_ref[0, 0] = jnp.where(dist >= 0, _t5_bias(dist, rel_ref, h) * LOG2E, NEG)


def _bias_table(rel_bias):
    return pl.pallas_call(
        _bias_table_kernel,
        grid=(H_A, N_NEAR),
        in_specs=[pl.BlockSpec(memory_space=pltpu.SMEM)],
        out_specs=pl.BlockSpec((1, 1, MOBA_BLOCK, MOBA_BLOCK), lambda h, d: (h, d, 0, 0)),
        out_shape=jax.ShapeDtypeStruct((H_A, N_NEAR, MOBA_BLOCK, MOBA_BLOCK), F32),
        compiler_params=_params("parallel", "parallel"),
        name="bias_table",
    )(rel_bias)


def _select_topk_t(score, n_valid_rows, k):
    nb = score.shape[0]
    row = lax.broadcasted_iota(jnp.int32, score.shape, 0)
    past = row < n_valid_rows
    sc = jnp.where(past, score, -jnp.inf)
    sel = jnp.full(score.shape, NEG, F32)
    for _ in range(k):
        mx = jnp.max(sc, axis=0, keepdims=True)
        idx = jnp.min(jnp.where(sc == mx, row, nb), axis=0, keepdims=True)
        hit = row == idx
        sel = jnp.where(hit, 0.0, sel)
        sc = jnp.where(hit, -jnp.inf, sc)
    return jnp.where(past, sel, NEG)


def _moba_prompt_kernel(far_ref, qt_ref, ka_ref, vte_ref, km_ref, tab_ref, o_ref,
                        qp_scr, m_scr, acc_scr):
    hp = pl.program_id(1)
    qi = pl.program_id(2)
    tq = qt_ref.shape[1]
    nb = km_ref.shape[1]
    blk_row = lax.broadcasted_iota(jnp.int32, (nb, tq), 0)
    sub = lax.broadcasted_iota(jnp.int32, (PAIR, tq), 0)
    qpair = qt_ref[...]

    for h in range(2):
        qth = qt_ref[h * DH_A:(h + 1) * DH_A, :].astype(F32)
        score = _dot(km_ref[0, :, h * DH_A:(h + 1) * DH_A], qth, HIGHEST)
        sel = _select_topk_t(score, qi, MOBA_TOPK)
        sel = jnp.where(blk_row == qi, 0.0, sel)
        sel = sel + jnp.where(qi - blk_row >= N_NEAR, far_ref[2 * hp + h], 0.0)
        hi = sel.astype(BF16)
        qp_scr[h, 0:PAIR, :] = jnp.where(sub // DH_A == h, qpair, jnp.zeros_like(qpair))
        qp_scr[h, PAIR:, :] = jnp.zeros((K_AUG - PAIR, tq), BF16)
        qp_scr[h, SEL_HI:SEL_HI + nb, :] = hi
        qp_scr[h, SEL_LO:SEL_LO + nb, :] = (sel - hi.astype(F32)).astype(BF16)
        m_scr[h] = jnp.full(m_scr.shape[1:], NEG, F32)
        acc_scr[h] = jnp.zeros(acc_scr.shape[1:], F32)

    def step(j, near):
        start = pl.multiple_of(j * MOBA_BLOCK, MOBA_BLOCK)
        kblk = ka_ref[pl.ds(start, MOBA_BLOCK), :]
        vblk = vte_ref[:, pl.ds(start, MOBA_BLOCK)]
        scores = [_dot(kblk, qp_scr[h]) for h in range(2)]
        for h in range(2):
            s = scores[h]
            if near:
                s = s + tab_ref[h, qi - j]
            m_old = m_scr[h]
            m_new = jnp.maximum(m_old, jnp.max(s, axis=0, keepdims=True))
            p = jnp.exp2(s - m_new).astype(BF16)
            acc_scr[h] = jnp.exp2(m_old - m_new) * acc_scr[h] + _dot(vblk, p)
            m_scr[h] = m_new

    first_near = jnp.maximum(qi - (N_NEAR - 1), 0)
    lax.fori_loop(0, first_near, lambda j, c: (step(j, False), c)[1], 0)
    lax.fori_loop(first_near, qi + 1, lambda j, c: (step(j, True), c)[1], 0)

    outs = [acc_scr[h, h * DH_A:(h + 1) * DH_A, :] / acc_scr[h, PAIR:PAIR + 1, :]
            for h in range(2)]
    o_ref[...] = jnp.concatenate(outs, axis=0).T.astype(BF16)


def _moba_prompt(qt, ka, vte, kmean, table, far, batch):
    m = qt.shape[1]
    seq = m // batch
    nb = seq // MOBA_BLOCK
    assert nb <= MAX_BLOCKS
    tq = MOBA_BLOCK
    n_pairs = H_A // 2
    return pl.pallas_call(
        _moba_prompt_kernel,
        grid=(batch, n_pairs, nb),
        in_specs=[pl.BlockSpec(memory_space=pltpu.SMEM),
                  pl.BlockSpec((PAIR, tq), lambda b, hp, qi: (hp, b * nb + qi)),
                  pl.BlockSpec((seq, K_AUG), lambda b, hp, qi: (b, hp)),
                  pl.BlockSpec((V_ROWS, seq), lambda b, hp, qi: (hp, b)),
                  pl.BlockSpec((1, nb, PAIR), lambda b, hp, qi: (b, 0, hp)),
                  pl.BlockSpec((2, N_NEAR, tq, tq), lambda b, hp, qi: (hp, 0, 0, 0))],
        out_specs=pl.BlockSpec((tq, PAIR), lambda b, hp, qi: (b * nb + qi, hp)),
        out_shape=jax.ShapeDtypeStruct((m, MW_A), BF16),
        scratch_shapes=[pltpu.VMEM((2, K_AUG, tq), BF16),
                        pltpu.VMEM((2, 1, tq), F32),
                        pltpu.VMEM((2, V_ROWS, tq), F32)],
        compiler_params=_params("parallel", "parallel", "arbitrary"),
        name="moba_prompt",
    )(far, qt, ka, vte, kmean.reshape(batch, nb, MW_A), table)


def _logits_sample_kernel(pt_ref, qt_ref, *refs):
    o_ref, qb_scr = refs[-2], refs[-1]
    b = pl.program_id(0)
    nb_s = qt_ref.shape[1]

    @pl.when(pl.program_id(1) == 0)
    def _():
        onehot = lax.broadcasted_iota(jnp.int32, (1, nb_s), 1) == b
        qcol = jnp.sum(jnp.where(onehot, qt_ref[...], 0.0), axis=1, keepdims=True)
        qb_scr[...] = jnp.broadcast_to(qcol * Q_SCALE, qb_scr.shape)

    per_block = MOBA_BLOCK // PAGE_SIZE
    for i in range(PAGES_PER_STEP):
        for h in range(H_A):
            kt = refs[i][0, 0, h]
            lg = jnp.sum(kt * qb_scr[h * DH_A:(h + 1) * DH_A, :], axis=0, keepdims=True)
            r, half = i // per_block, i % per_block
            o_ref[0, h, r:r + 1, half * PAGE_SIZE:(half + 1) * PAGE_SIZE] = lg


def _logits_sample(cache_kt, page_table, qt, layer):
    nb_s, n_pages = page_table.shape
    steps = n_pages // PAGES_PER_STEP
    rows = PAGES_PER_STEP * PAGE_SIZE // MOBA_BLOCK

    def page_spec(i):
        return pl.BlockSpec((1, 1, H_A, DH_A, PAGE_SIZE),
                            lambda b, s, pt: (layer, pt[b, s * PAGES_PER_STEP + i], 0, 0, 0))

    return pl.pallas_call(
        _logits_sample_kernel,
        grid_spec=pltpu.PrefetchScalarGridSpec(
            num_scalar_prefetch=1,
            grid=(nb_s, steps),
            in_specs=[pl.BlockSpec(qt.shape, lambda b, s, pt: (0, 0))]
                     + [page_spec(i) for i in range(PAGES_PER_STEP)],
            out_specs=pl.BlockSpec((1, H_A, rows, MOBA_BLOCK), lambda b, s, pt: (b, 0, s, 0)),
            scratch_shapes=[pltpu.VMEM((MW_A, PAGE_SIZE), F32)]),
        out_shape=jax.ShapeDtypeStruct((nb_s, H_A, steps * rows, MOBA_BLOCK), F32),
        compiler_params=_params("parallel", "arbitrary"),
        name="logits_sample",
    )(page_table, qt, *([cache_kt] * PAGES_PER_STEP))


def _select_sample_kernel(lg_ref, o_ref):
    sc = jnp.sum(lg_ref[...], axis=-1)
    nblk = sc.shape[-1]
    blk = lax.broadcasted_iota(jnp.int32, sc.shape, 2)
    for t in range(MOBA_TOPK):
        mx = jnp.max(sc, axis=2, keepdims=True)
        idx = jnp.min(jnp.where(sc == mx, blk, nblk), axis=2, keepdims=True)
        o_ref[:, :, t:t + 1] = idx
        sc = jnp.where(blk == idx, -jnp.inf, sc)


def _select_sample(logits):
    nb_s = logits.shape[0]
    return pl.pallas_call(
        _select_sample_kernel,
        out_shape=jax.ShapeDtypeStruct((nb_s, H_A, MOBA_TOPK), jnp.int32),
        compiler_params=pltpu.CompilerParams(vmem_limit_bytes=VMEM_LIMIT),
        name="select_sample",
    )(logits)


def _attend_sample_kernel(pt_ref, sel_ref, rel_ref, lg_ref, qkv_ref, *refs):
    o_ref = refs[-1]
    v_refs = refs[:-1]
    per_block = MOBA_BLOCK // PAGE_SIZE
    b = pl.program_id(0)
    h = pl.program_id(1)
    past_len = pt_ref.shape[1] * PAGE_SIZE
    lane = lax.broadcasted_iota(jnp.int32, (1, MOBA_BLOCK), 1)

    q = qkv_ref[0, 0, pl.ds(h, 1), :] * Q_SCALE
    k_new = qkv_ref[1, 0, pl.ds(h, 1), :]
    v_new = qkv_ref[2, 0, pl.ds(h, 1), :]
    own = (jnp.sum(k_new * q, axis=1, keepdims=True)
           + _t5_bias(jnp.zeros((1, 1), jnp.int32), rel_ref, h))
    logits = []
    mx = own
    for t in range(MOBA_TOPK):
        blk = sel_ref[b, h, t]
        kpos = blk * MOBA_BLOCK + lane
        lg = lg_ref[0, 0, pl.ds(blk, 1), :] + _t5_bias(past_len - kpos, rel_ref, h)
        logits.append(lg)
        mx = jnp.maximum(mx, jnp.max(lg, axis=1, keepdims=True))
    p_own = jnp.exp(own - mx)
    den = p_own
    acc = p_own * v_new
    for t in range(MOBA_TOPK):
        p = jnp.exp(logits[t] - mx)
        den = den + jnp.sum(p, axis=1, keepdims=True)
        for i in range(per_block):
            vt = v_refs[t * per_block + i][0, 0, 0]
            acc = acc + _dot_nt(p[:, i * PAGE_SIZE:(i + 1) * PAGE_SIZE], vt, HIGHEST)
    o_ref[0, pl.ds(h, 1), :] = acc / den


def _attend_sample(cache_vt, page_table, sel, rel_bias, logits, qkv, layer):
    nb_s = page_table.shape[0]
    per_block = MOBA_BLOCK // PAGE_SIZE
    n_sel_pages = MOBA_TOPK * per_block
    nblk = logits.shape[2]

    def page_spec(t):
        def imap(b, h, pt, sl, rel):
            page = sl[b, h, t // per_block] * per_block + (t % per_block)
            return (layer, pt[b, page], h, 0, 0)
        return pl.BlockSpec((1, 1, 1, DH_A, PAGE_SIZE), imap)

    return pl.pallas_call(
        _attend_sample_kernel,
        grid_spec=pltpu.PrefetchScalarGridSpec(
            num_scalar_prefetch=3,
            grid=(nb_s, H_A),
            in_specs=[pl.BlockSpec((1, 1, nblk, MOBA_BLOCK), lambda b, h, pt, sl, rel: (b, h, 0, 0)),
                      pl.BlockSpec((3, 1, H_A, DH_A), lambda b, h, pt, sl, rel: (0, b, 0, 0))]
                     + [page_spec(t) for t in range(n_sel_pages)],
            out_specs=pl.BlockSpec((1, H_A, DH_A), lambda b, h, pt, sl, rel: (b, 0, 0))),
        out_shape=jax.ShapeDtypeStruct((nb_s, H_A, DH_A), F32),
        compiler_params=_params("parallel", "arbitrary"),
        name="attend_sample",
    )(page_table, sel, rel_bias, logits, qkv, *([cache_vt] * n_sel_pages))


def _mix_kernel(x_ref, hm_ref, ha_ref, sg_ref, wbm_ref, wba_ref, wo_ref, g_ref, o_ref):
    sg = sg_ref[...].astype(F32)
    mixed = (sg[:, :D_MODEL] * _dot(hm_ref[...], wbm_ref[...])
             + sg[:, D_MODEL:] * _dot(ha_ref[...], wba_ref[...]))
    y = _dot(mixed.astype(BF16), wo_ref[...])
    o_ref[...] = x_ref[...] + _rms(y, g_ref[...])


def _mix(x, hm, ha, sg, wbm, wba, wo, g):
    m = x.shape[0]
    tm = min(TM_PROJ, m)
    row = lambda w: pl.BlockSpec((tm, w), lambda i: (i, 0))
    full = lambda a: pl.BlockSpec(a.shape, lambda i: (0,) * a.ndim)
    return pl.pallas_call(
        _mix_kernel,
        grid=(m // tm,),
        in_specs=[row(D_MODEL), row(MW_M), row(MW_A), row(2 * D_MODEL),
                  full(wbm), full(wba), full(wo), full(g)],
        out_specs=row(D_MODEL),
        out_shape=jax.ShapeDtypeStruct((m, D_MODEL), F32),
        compiler_params=_params("parallel"),
        name="mix",
    )(x, hm, ha, sg, wbm, wba, wo, g)


def _ffn_kernel(x_ref, gpre_ref, gpost_ref, wg_ref, wu_ref, wd_ref, o_ref):
    x = x_ref[...]
    hf = _rms(x, gpre_ref[...]).astype(BF16)
    acc = jnp.zeros(x.shape, F32)
    for c in range(D_FF // FF_CHUNK):
        sl = slice(c * FF_CHUNK, (c + 1) * FF_CHUNK)
        gate = _dot(hf, wg_ref[:, sl])
        up = _dot(hf, wu_ref[:, sl])
        mid = (gate * jax.nn.sigmoid(gate) * up).astype(BF16)
        acc = acc + _dot(mid, wd_ref[sl, :])
    o_ref[...] = x + _rms(acc, gpost_ref[...])


def _ffn(x, gpre, gpost, wg, wu, wd):
    m = x.shape[0]
    tm = min(TM_FFN, m)
    row = pl.BlockSpec((tm, D_MODEL), lambda i: (i, 0))
    full = lambda a: pl.BlockSpec(a.shape, lambda i: (0,) * a.ndim)
    return pl.pallas_call(
        _ffn_kernel,
        grid=(m // tm,),
        in_specs=[row, full(gpre), full(gpost), full(wg), full(wu), full(wd)],
        out_specs=row,
        out_shape=jax.ShapeDtypeStruct((m, D_MODEL), F32),
        compiler_params=_params("parallel"),
        name="ffn",
    )(x, gpre, gpost, wg, wu, wd)


def kernel(x_prompt, x_sample, cache_k, cache_v, state_C, state_n, state_m, page_table,
           norm_mix_pre, norm_mix_post, norm_ffn_pre, norm_ffn_post, w_in, b_igate, b_fgate,
           g_mlstm, w_branch_m, w_branch_a, w_out, w_ffn_gate, w_ffn_up, w_ffn_down, rel_bias):
    bp, seq, _ = x_prompt.shape
    bs = x_sample.shape[0]
    depth = w_in.shape[0]
    xp = x_prompt.reshape(bp * seq, D_MODEL)
    xs = x_sample.reshape(bs, D_MODEL)
    cache_kt = jnp.transpose(cache_k, (0, 1, 3, 4, 2))
    cache_vt = jnp.transpose(cache_v, (0, 1, 3, 4, 2))
    w_in_t = jnp.swapaxes(w_in, 1, 2)

    table = _bias_table(rel_bias)
    far = rel_bias[N_BUCKETS - 1] * LOG2E

    outs = {k: [] for k in ("kp", "vp", "cp", "np", "mp", "ks", "vs", "cs", "ns", "ms")}
    for l in range(depth):
        row = lambda a: a[l].reshape(1, -1)
        wt = w_in_t[l]
        wm = wt[:COL_M_END].astype(BF16)
        wif = wt[COL_M_END:COL_IF_END]
        wq = wt[COL_IF_END:COL_Q_END].astype(BF16)
        wk = wt[COL_Q_END:COL_K_END].astype(BF16)
        wv = wt[COL_K_END:COL_V_END].astype(BF16)
        wgt = wt[COL_V_END:].astype(BF16)
        b_if = jnp.concatenate([b_igate[l], b_fgate[l]])
        wbm, wba, wo = (w_branch_m[l].astype(BF16), w_branch_a[l].astype(BF16), w_out[l].astype(BF16))
        wfg, wfu, wfd = (w_ffn_gate[l].astype(BF16), w_ffn_up[l].astype(BF16), w_ffn_down[l].astype(BF16))

        mproj, gcol, grow, qt, ka, kt32, vt32, vte, kmean, sg = _inproj_prompt(
            xp, row(norm_mix_pre), wm, wif, wq, wk, wv, wgt, bp)
        hm, cext, mstate = _mlstm_prompt(mproj, gcol, grow, b_if, g_mlstm[l], bp)
        ha = _moba_prompt(qt, ka, vte, kmean, table, far, bp)
        xp = _mix(xp, hm, ha, sg, wbm, wba, wo, row(norm_mix_post))
        xp = _ffn(xp, row(norm_ffn_pre), row(norm_ffn_post), wfg, wfu, wfd)
        outs["kp"].append(kt32)
        outs["vp"].append(vt32)
        outs["cp"].append(cext[..., :DH_M])
        outs["np"].append(cext[..., DH_M])
        outs["mp"].append(mstate[:, :, 0, 0])

        mproj_s, mproj_t, gcol_s, qkv_s, qt_s, sg_s = _inproj_sample(
            xs, row(norm_mix_pre), wm, wif, wq, wk, wv, wgt)
        hm_s, c_s, n_s, m_s = _mlstm_sample(mproj_s, mproj_t, gcol_s, b_if, g_mlstm[l],
                                            state_C[l], state_n[l], state_m[l])
        logits_s = _logits_sample(cache_kt, page_table, qt_s, l)
        sel = _select_sample(logits_s)
        ha_s = _attend_sample(cache_vt, page_table, sel, rel_bias, logits_s,
                              qkv_s.reshape(3, bs, H_A, DH_A), l)
        xs = _mix(xs, hm_s.reshape(bs, MW_M), ha_s.reshape(bs, MW_A).astype(BF16), sg_s,
                  wbm, wba, wo, row(norm_mix_post))
        xs = _ffn(xs, row(norm_ffn_pre), row(norm_ffn_post), wfg, wfu, wfd)
        outs["ks"].append(qkv_s[1].reshape(bs, 1, H_A, DH_A))
        outs["vs"].append(qkv_s[2].reshape(bs, 1, H_A, DH_A))
        outs["cs"].append(c_s)
        outs["ns"].append(n_s)
        outs["ms"].append(m_s.reshape(bs, H_M))

    st = lambda k: jnp.stack(outs[k])
    kv_out = lambda k: jnp.transpose(st(k).reshape(depth, bp, H_A, DH_A, seq), (0, 1, 4, 2, 3))
    return (xp.reshape(bp, seq, D_MODEL), xs.reshape(bs, 1, D_MODEL),
            kv_out("kp"), kv_out("vp"), st("cp"), st("np"), st("mp"),
            st("ks"), st("vs"), st("cs"), st("ns"), st("ms"))
```

```python
import functools
import math

import jax
import jax.numpy as jnp
from jax import lax
from jax.experimental import pallas as pl
from jax.experimental.pallas import tpu as pltpu

F32 = jnp.float32
BF16 = jnp.bfloat16
HIGHEST = lax.Precision.HIGHEST

D_MODEL = 1024
H_M, DH_M = 4, 128
MW_M = H_M * DH_M
H_A, DH_A = 8, 64
MW_A = H_A * DH_A
MOBA_BLOCK = 256
MOBA_TOPK = 3
N_BUCKETS = 32
REL_MAX_DIST = 2048
D_FF = 2816
EPS = 1e-6
NEG = -1e30
PAGE_SIZE = 128

N_GATE_COLS = 2 * H_M
COL_M_END = 4 * MW_M
COL_IF_END = COL_M_END + N_GATE_COLS
COL_Q_END = COL_IF_END + MW_A
COL_K_END = COL_Q_END + MW_A
COL_V_END = COL_K_END + MW_A
LOG_K_SCALE = math.log(DH_M ** -0.5)
Q_SCALE = DH_A ** -0.5
LOG2E = math.log2(math.e)

N_NEAR = (REL_MAX_DIST + MOBA_BLOCK - 1) // MOBA_BLOCK + 1
assert (N_NEAR * MOBA_BLOCK - (MOBA_BLOCK - 1)) >= REL_MAX_DIST
TILE_ZERO = N_NEAR
TILE_MASKED = N_NEAR + 1
N_TILES = N_NEAR + 2
MOBA_GROUP = 4

PAIR = 2 * DH_A
K_AUG = 2 * PAIR
SEL_HI, SEL_LO = PAIR, PAIR + 32
V_ROWS = DH_A + 16
MAX_BLOCKS = 32

VMEM_LIMIT = 56 * 1024 * 1024

TM_PROJ = 512
TM_FFN = 256
L_CHUNK = 256
FF_CHUNK = 1408
PAGES_PER_STEP = 16


def _params(*sem):
    return pltpu.CompilerParams(dimension_semantics=sem, vmem_limit_bytes=VMEM_LIMIT)


def _rms(x, g):
    return x * lax.rsqrt(jnp.mean(x * x, axis=-1, keepdims=True) + EPS) * g


def _log_sigmoid(x):
    return jnp.minimum(x, 0.0) - jnp.log(1.0 + jnp.exp(-jnp.abs(x)))


def _dot(a, b, precision=None):
    return jnp.dot(a, b, precision=precision, preferred_element_type=F32)


def _dot_nt(a, b, precision=None):
    return lax.dot_general(a, b, (((1,), (1,)), ((), ())), precision=precision,
                           preferred_element_type=F32)


def _dot_tn(a, b):
    return lax.dot_general(a, b, (((0,), (0,)), ((), ())), preferred_element_type=F32)


def _inproj_prompt_kernel(x_ref, g_ref, wm_ref, wif_ref, wq_ref, wk_ref, wv_ref, wg_ref,
                          m_ref, gc_ref, gr_ref, qt_ref, ka_ref, kt32_ref, vt32_ref, vte_ref,
                          km_ref, sg_ref):
    tm = x_ref.shape[0]
    xn = _rms(x_ref[...], g_ref[...])
    xb = xn.astype(BF16)
    m_ref[...] = _dot_nt(xb, wm_ref[...]).astype(BF16)
    gates_t = _dot_nt(wif_ref[...], xn, HIGHEST)
    gr_ref[...] = gates_t
    gc_ref[...] = gates_t.T
    qt_ref[...] = (_dot_nt(wq_ref[...], xb) * (Q_SCALE * LOG2E)).astype(BF16)

    k = _dot_nt(xb, wk_ref[...])
    kt32_ref[0] = k.T
    km_ref[0] = jnp.mean(k.reshape(tm // MOBA_BLOCK, MOBA_BLOCK, MW_A), axis=1)
    row = lax.broadcasted_iota(jnp.int32, (tm, PAIR), 0)
    lane = lax.broadcasted_iota(jnp.int32, (tm, PAIR), 1)
    blk = (pl.program_id(1) * tm + row) // MOBA_BLOCK
    onehot = jnp.where(jnp.logical_and(lane < 2 * MAX_BLOCKS, lane % MAX_BLOCKS == blk),
                       1.0, 0.0).astype(BF16)
    kb = k.astype(BF16)
    for p in range(H_A // 2):
        ka_ref[:, p * K_AUG:p * K_AUG + PAIR] = kb[:, p * PAIR:(p + 1) * PAIR]
        ka_ref[:, p * K_AUG + PAIR:(p + 1) * K_AUG] = onehot

    vt = _dot_nt(wv_ref[...], xb)
    vt32_ref[0] = vt
    ones_rows = jnp.where(lax.broadcasted_iota(jnp.int32, (V_ROWS - DH_A, tm), 0) == 0,
                          1.0, 0.0).astype(BF16)
    for h in range(H_A):
        vte_ref[h * V_ROWS:h * V_ROWS + DH_A, :] = vt[h * DH_A:(h + 1) * DH_A].astype(BF16)
        vte_ref[h * V_ROWS + DH_A:(h + 1) * V_ROWS, :] = ones_rows
    sg_ref[...] = jax.nn.sigmoid(_dot_nt(xb, wg_ref[...])).astype(BF16)


def _inproj_prompt(x, g, wm, wif, wq, wk, wv, wg, batch):
    m = x.shape[0]
    seq = m // batch
    tm = TM_PROJ
    nt = seq // tm
    n_pairs = H_A // 2
    row = lambda w: pl.BlockSpec((tm, w), lambda b, t: (b * nt + t, 0))
    col = lambda h: pl.BlockSpec((h, tm), lambda b, t: (0, b * nt + t))
    full = lambda a: pl.BlockSpec(a.shape, lambda b, t: (0,) * a.ndim)
    return pl.pallas_call(
        _inproj_prompt_kernel,
        grid=(batch, nt),
        in_specs=[row(D_MODEL), full(g), full(wm), full(wif), full(wq), full(wk), full(wv), full(wg)],
        out_specs=[row(COL_M_END), row(N_GATE_COLS), col(N_GATE_COLS), col(MW_A),
                   row(n_pairs * K_AUG),
                   pl.BlockSpec((1, MW_A, tm), lambda b, t: (b, 0, t)),
                   pl.BlockSpec((1, MW_A, tm), lambda b, t: (b, 0, t)),
                   col(H_A * V_ROWS),
                   pl.BlockSpec((1, tm // MOBA_BLOCK, MW_A), lambda b, t: (b * nt + t, 0, 0)),
                   row(2 * D_MODEL)],
        out_shape=[jax.ShapeDtypeStruct((m, COL_M_END), BF16),
                   jax.ShapeDtypeStruct((m, N_GATE_COLS), F32),
                   jax.ShapeDtypeStruct((N_GATE_COLS, m), F32),
                   jax.ShapeDtypeStruct((MW_A, m), BF16),
                   jax.ShapeDtypeStruct((m, n_pairs * K_AUG), BF16),
                   jax.ShapeDtypeStruct((batch, MW_A, seq), F32),
                   jax.ShapeDtypeStruct((batch, MW_A, seq), F32),
                   jax.ShapeDtypeStruct((H_A * V_ROWS, m), BF16),
                   jax.ShapeDtypeStruct((m // tm, tm // MOBA_BLOCK, MW_A), F32),
                   jax.ShapeDtypeStruct((m, 2 * D_MODEL), BF16)],
        compiler_params=_params("parallel", "parallel"),
        name="inproj_prompt",
    )(x, g, wm, wif, wq, wk, wv, wg)


def _inproj_sample_kernel(x_ref, g_ref, wm_ref, wif_ref, wq_ref, wk_ref, wv_ref, wg_ref,
                          m_ref, mt_ref, gc_ref, qkv_ref, qt_ref, sg_ref):
    xn = _rms(x_ref[...], g_ref[...])
    xb = xn.astype(BF16)
    m_ref[...] = _dot_nt(xb, wm_ref[...])
    mt_ref[...] = _dot_nt(wm_ref[:2 * MW_M, :], xb)
    gc_ref[...] = _dot_nt(xn, wif_ref[...], HIGHEST)
    qkv_ref[0] = _dot_nt(xb, wq_ref[...])
    qkv_ref[1] = _dot_nt(xb, wk_ref[...])
    qkv_ref[2] = _dot_nt(xb, wv_ref[...])
    qt_ref[...] = _dot_nt(wq_ref[...], xb)
    sg_ref[...] = jax.nn.sigmoid(_dot_nt(xb, wg_ref[...])).astype(BF16)


def _inproj_sample(x, g, wm, wif, wq, wk, wv, wg):
    m = x.shape[0]
    return pl.pallas_call(
        _inproj_sample_kernel,
        out_shape=[jax.ShapeDtypeStruct((m, COL_M_END), F32),
                   jax.ShapeDtypeStruct((2 * MW_M, m), F32),
                   jax.ShapeDtypeStruct((m, N_GATE_COLS), F32),
                   jax.ShapeDtypeStruct((3, m, MW_A), F32),
                   jax.ShapeDtypeStruct((MW_A, m), F32),
                   jax.ShapeDtypeStruct((m, 2 * D_MODEL), BF16)],
        compiler_params=pltpu.CompilerParams(vmem_limit_bytes=VMEM_LIMIT),
        name="inproj_sample",
    )(x, g, wm, wif, wq, wk, wv, wg)


def _mlstm_prompt_kernel(m_ref, gc_ref, gr_ref, bc_ref, br_ref, gm_ref,
                         h_ref, c_ref, ms_ref):
    L = m_ref.shape[0]

    @pl.when(pl.program_id(1) == 0)
    def _():
        c_ref[...] = jnp.zeros_like(c_ref)
        ms_ref[...] = jnp.zeros_like(ms_ref)

    row = lax.broadcasted_iota(jnp.int32, (L, L), 0)
    col = lax.broadcasted_iota(jnp.int32, (L, L), 1)
    causal = col <= row
    lower = causal.astype(F32)
    upper = (row <= col).astype(F32)

    gcol = gc_ref[...] + br_ref[...]
    grow = gr_ref[...] + bc_ref[...]
    bcol = _dot(lower, _log_sigmoid(gcol), HIGHEST)
    brow = _dot(_log_sigmoid(grow), upper, HIGHEST)

    lane = lax.broadcasted_iota(jnp.int32, (L, DH_M), 1)
    ones_blk = jnp.where(lane == 0, 1.0, 0.0).astype(BF16)

    for h in range(H_M):
        q = m_ref[:, h * DH_M:(h + 1) * DH_M]
        k = m_ref[:, MW_M + h * DH_M:MW_M + (h + 1) * DH_M]
        v = m_ref[:, 2 * MW_M + h * DH_M:2 * MW_M + (h + 1) * DH_M]
        o = m_ref[:, 3 * MW_M + h * DH_M:3 * MW_M + (h + 1) * DH_M]
        i_col = gcol[:, h:h + 1]
        b_col = bcol[:, H_M + h:H_M + h + 1]
        i_row = grow[h:h + 1, :]
        b_row = brow[H_M + h:H_M + h + 1, :]
        b_last = b_col[L - 1:L, :]
        cext = c_ref[0, h]
        m_prev = ms_ref[0, h][0:1, 0:1]

        d = jnp.where(causal, b_col - (b_row - i_row), NEG)
        inter = b_col + m_prev
        mt = jnp.maximum(inter, jnp.max(d, axis=1, keepdims=True))
        a = jnp.exp(inter - mt)
        s = _dot_nt(q, k) * jnp.exp(d - (mt - LOG_K_SCALE))
        v_ext = jnp.concatenate([v, ones_blk], axis=1)
        nd = a * _dot(q, cext.astype(BF16)) + _dot(s.astype(BF16), v_ext)
        den = nd[:, DH_M:DH_M + 1]
        hh = nd[:, :DH_M] / jnp.maximum(jnp.abs(den), jnp.exp(-mt))

        mu = jnp.mean(hh, axis=1, keepdims=True)
        hc = hh - mu
        var = jnp.mean(hc * hc, axis=1, keepdims=True)
        hn = hc * lax.rsqrt(var + EPS) * gm_ref[:, h * DH_M:(h + 1) * DH_M]
        h_ref[:, h * DH_M:(h + 1) * DH_M] = (hn * jax.nn.sigmoid(o.astype(F32))).astype(BF16)

        g_row = b_last - b_row + i_row
        e = b_last + m_prev
        m_new = jnp.maximum(e, jnp.max(g_row, axis=1, keepdims=True))
        w_col = jnp.exp(b_last - b_col + i_col - (m_new - LOG_K_SCALE))
        kw = (k.astype(F32) * w_col).astype(BF16)
        c_ref[0, h] = jnp.exp(e - m_new) * cext + _dot_tn(kw, v_ext)
        ms_ref[0, h] = jnp.broadcast_to(m_new, ms_ref.shape[2:])


def _mlstm_prompt(mproj, gcol, grow, b_if, g_mlstm, batch):
    m = mproj.shape[0]
    L = L_CHUNK
    nc = m // batch // L
    bias_row = b_if.reshape(1, N_GATE_COLS)
    bias_col = b_if.reshape(N_GATE_COLS, 1)
    gm = g_mlstm.reshape(1, MW_M)
    return pl.pallas_call(
        _mlstm_prompt_kernel,
        grid=(batch, nc),
        in_specs=[pl.BlockSpec((L, COL_M_END), lambda b, c: (b * nc + c, 0)),
                  pl.BlockSpec((L, N_GATE_COLS), lambda b, c: (b * nc + c, 0)),
                  pl.BlockSpec((N_GATE_COLS, L), lambda b, c: (0, b * nc + c)),
                  pl.BlockSpec((N_GATE_COLS, 1), lambda b, c: (0, 0)),
                  pl.BlockSpec((1, N_GATE_COLS), lambda b, c: (0, 0)),
                  pl.BlockSpec((1, MW_M), lambda b, c: (0, 0))],
        out_specs=[pl.BlockSpec((L, MW_M), lambda b, c: (b * nc + c, 0)),
                   pl.BlockSpec((1, H_M, DH_M, 2 * DH_M), lambda b, c: (b, 0, 0, 0)),
                   pl.BlockSpec((1, H_M, 8, 128), lambda b, c: (b, 0, 0, 0))],
        out_shape=[jax.ShapeDtypeStruct((m, MW_M), BF16),
                   jax.ShapeDtypeStruct((batch, H_M, DH_M, 2 * DH_M), F32),
                   jax.ShapeDtypeStruct((batch, H_M, 8, 128), F32)],
        compiler_params=_params("parallel", "arbitrary"),
        name="mlstm_prompt",
    )(mproj, gcol, grow, bias_col, bias_row, gm)


def _mlstm_sample_kernel(m_ref, mt_ref, gc_ref, bi_ref, gm_ref, c_ref, n_ref, ms_ref,
                         h_ref, co_ref, no_ref, mo_ref):
    b = pl.program_id(0)
    nb = mt_ref.shape[1]
    onehot = lax.broadcasted_iota(jnp.int32, (1, nb), 1) == b
    g = gc_ref[0] + bi_ref[...]
    for h in range(H_M):
        sl = slice(h * DH_M, (h + 1) * DH_M)
        q_row = m_ref[0, :, h * DH_M:(h + 1) * DH_M]
        k_row = m_ref[0, :, MW_M + h * DH_M:MW_M + (h + 1) * DH_M]
        v_row = m_ref[0, :, 2 * MW_M + h * DH_M:2 * MW_M + (h + 1) * DH_M]
        o_row = m_ref[0, :, 3 * MW_M + h * DH_M:3 * MW_M + (h + 1) * DH_M]
        q_col = jnp.sum(jnp.where(onehot, mt_ref[sl, :], 0.0), axis=1, keepdims=True)
        k_col = jnp.sum(jnp.where(onehot, mt_ref[MW_M + h * DH_M:MW_M + (h + 1) * DH_M, :], 0.0),
                        axis=1, keepdims=True)
        i_pre = g[:, h:h + 1]
        log_f = _log_sigmoid(g[:, H_M + h:H_M + h + 1])
        m_prev = ms_ref[0, :, h:h + 1]
        inter = log_f + m_prev
        m_new = jnp.maximum(inter, i_pre)
        a = jnp.exp(inter - m_new)
        w = jnp.exp(i_pre - (m_new - LOG_K_SCALE))
        c = c_ref[0, h]
        n = n_ref[0, h:h + 1, :]
        s = jnp.sum(q_row * k_row, axis=1, keepdims=True) * w
        num = a * jnp.sum(c * q_col, axis=0, keepdims=True) + s * v_row
        den = a * jnp.sum(q_row * n, axis=1, keepdims=True) + s
        hh = num / jnp.maximum(jnp.abs(den), jnp.exp(-m_new))
        mu = jnp.mean(hh, axis=1, keepdims=True)
        hc = hh - mu
        var = jnp.mean(hc * hc, axis=1, keepdims=True)
        hn = hc * lax.rsqrt(var + EPS) * gm_ref[:, sl]
        h_ref[0, :, sl] = (hn * jax.nn.sigmoid(o_row)).astype(BF16)
        co_ref[0, h] = a * c + (w * k_col) * v_row
        no_ref[0, h:h + 1, :] = a * n + w * k_row
        mo_ref[0, :, h:h + 1] = m_new


def _mlstm_sample(mproj, mproj_t, gcol, b_if, g_mlstm, c0, n0, m0):
    nb = mproj.shape[0]
    return pl.pallas_call(
        _mlstm_sample_kernel,
        grid=(nb,),
        in_specs=[pl.BlockSpec((1, 1, COL_M_END), lambda b: (b, 0, 0)),
                  pl.BlockSpec((2 * MW_M, nb), lambda b: (0, 0)),
                  pl.BlockSpec((1, 1, N_GATE_COLS), lambda b: (b, 0, 0)),
                  pl.BlockSpec((1, N_GATE_COLS), lambda b: (0, 0)),
                  pl.BlockSpec((1, MW_M), lambda b: (0, 0)),
                  pl.BlockSpec((1, H_M, DH_M, DH_M), lambda b: (b, 0, 0, 0)),
                  pl.BlockSpec((1, H_M, DH_M), lambda b: (b, 0, 0)),
                  pl.BlockSpec((1, 1, H_M), lambda b: (b, 0, 0))],
        out_specs=[pl.BlockSpec((1, 1, MW_M), lambda b: (b, 0, 0)),
                   pl.BlockSpec((1, H_M, DH_M, DH_M), lambda b: (b, 0, 0, 0)),
                   pl.BlockSpec((1, H_M, DH_M), lambda b: (b, 0, 0)),
                   pl.BlockSpec((1, 1, H_M), lambda b: (b, 0, 0))],
        out_shape=[jax.ShapeDtypeStruct((nb, 1, MW_M), BF16),
                   jax.ShapeDtypeStruct((nb, H_M, DH_M, DH_M), F32),
                   jax.ShapeDtypeStruct((nb, H_M, DH_M), F32),
                   jax.ShapeDtypeStruct((nb, 1, H_M), F32)],
        compiler_params=_params("parallel"),
        name="mlstm_sample",
    )(mproj.reshape(nb, 1, COL_M_END), mproj_t, gcol.reshape(nb, 1, N_GATE_COLS),
      b_if.reshape(1, N_GATE_COLS), g_mlstm.reshape(1, MW_M), c0, n0, m0.reshape(nb, 1, H_M))


def _t5_bucket(dist):
    n = jnp.maximum(dist, 0)
    max_exact = N_BUCKETS // 2
    nf = jnp.maximum(n, 1).astype(F32)
    large = max_exact + (jnp.log(nf / max_exact) / math.log(REL_MAX_DIST / max_exact)
                         * (N_BUCKETS - max_exact)).astype(jnp.int32)
    large = jnp.minimum(large, N_BUCKETS - 1)
    return jnp.where(n < max_exact, n, large)


def _t5_bias(dist, rel_ref, h):
    bucket = _t5_bucket(dist)
    bias = jnp.zeros(dist.shape, F32)
    for kb in range(N_BUCKETS):
        bias = jnp.where(bucket == kb, rel_ref[kb, h], bias)
    return bias


def _bias_table_kernel(rel_ref, o_ref):
    h = pl.program_id(0)
    delta = pl.program_id(1)
    c = lax.broadcasted_iota(jnp.int32, (MOBA_BLOCK, MOBA_BLOCK), 0)
    r = lax.broadcasted_iota(jnp.int32, (MOBA_BLOCK, MOBA_BLOCK), 1)
    dist = delta * MOBA_BLOCK + r - c
    o_ref[0, 0] = jnp.where(dist >= 0, _t5_bias(dist, rel_ref, h) * LOG2E, NEG)


def _bias_table(rel_bias):
    return pl.pallas_call(
        _bias_table_kernel,
        grid=(H_A, N_NEAR),
        in_specs=[pl.BlockSpec(memory_space=pltpu.SMEM)],
        out_specs=pl.BlockSpec((1, 1, MOBA_BLOCK, MOBA_BLOCK), lambda h, d: (h, d, 0, 0)),
        out_shape=jax.ShapeDtypeStruct((H_A, N_NEAR, MOBA_BLOCK, MOBA_BLOCK), F32),
        compiler_params=_params("parallel", "parallel"),
        name="bias_table",
    )(rel_bias)


def _select_topk_t(score, n_valid_rows, k):
    nb = score.shape[0]
    row = lax.broadcasted_iota(jnp.int32, score.shape, 0)
    past = row < n_valid_rows
    sc = jnp.where(past, score, -jnp.inf)
    sel = jnp.full(score.shape, NEG, F32)
    for _ in range(k):
        mx = jnp.max(sc, axis=0, keepdims=True)
        idx = jnp.min(jnp.where(sc == mx, row, nb), axis=0, keepdims=True)
        hit = row == idx
        sel = jnp.where(hit, 0.0, sel)
        sc = jnp.where(hit, -jnp.inf, sc)
    return jnp.where(past, sel, NEG)


def _moba_prompt_kernel(far_ref, qt_ref, ka_ref, vte_ref, km_ref, tab_ref, o_ref,
                        qp_scr, m_scr, acc_scr, sa_scr, sb_scr, gma_scr, gmb_scr):
    hp = pl.program_id(1)
    qi = pl.program_id(2)
    tq = qt_ref.shape[1]
    nb = km_ref.shape[1]
    blk_row = lax.broadcasted_iota(jnp.int32, (nb, tq), 0)
    sub = lax.broadcasted_iota(jnp.int32, (PAIR, tq), 0)
    qpair = qt_ref[...]

    for h in range(2):
        qth = qt_ref[h * DH_A:(h + 1) * DH_A, :].astype(F32)
        score = _dot(km_ref[0, :, h * DH_A:(h + 1) * DH_A], qth, HIGHEST)
        sel = _select_topk_t(score, qi, MOBA_TOPK)
        sel = jnp.where(blk_row == qi, 0.0, sel)
        sel = sel + jnp.where(qi - blk_row >= N_NEAR, far_ref[2 * hp + h], 0.0)
        hi = sel.astype(BF16)
        qp_scr[h, 0:PAIR, :] = jnp.where(sub // DH_A == h, qpair, jnp.zeros_like(qpair))
        qp_scr[h, PAIR:, :] = jnp.zeros((K_AUG - PAIR, tq), BF16)
        qp_scr[h, SEL_HI:SEL_HI + nb, :] = hi
        qp_scr[h, SEL_LO:SEL_LO + nb, :] = (sel - hi.astype(F32)).astype(BF16)
        m_scr[h] = jnp.full(m_scr.shape[1:], NEG, F32)
        acc_scr[h] = jnp.zeros(acc_scr.shape[1:], F32)

    n_groups = qi // MOBA_GROUP + 1
    gk = MOBA_GROUP * MOBA_BLOCK

    def scores(g, s_scr, gm_scr):
        g = jnp.minimum(g, n_groups - 1)
        start = pl.multiple_of(g * gk, gk)
        for h in range(2):
            gmax = None
            for t in range(MOBA_GROUP):
                kblk = ka_ref[pl.ds(start + t * MOBA_BLOCK, MOBA_BLOCK), :]
                delta = qi - (g * MOBA_GROUP + t)
                tile = jnp.where(delta < 0, TILE_MASKED, jnp.minimum(delta, TILE_ZERO))
                s = _dot(kblk, qp_scr[h]) + tab_ref[h, tile]
                s_scr[h, t * MOBA_BLOCK:(t + 1) * MOBA_BLOCK, :] = s
                cmax = jnp.max(s, axis=0, keepdims=True)
                gmax = cmax if gmax is None else jnp.maximum(gmax, cmax)
            gm_scr[h] = gmax

    def attend(g, s_scr, gm_scr):
        start = pl.multiple_of(g * gk, gk)
        for h in range(2):
            m_old = m_scr[h]
            m_new = jnp.maximum(m_old, gm_scr[h])
            pv = None
            for t in range(MOBA_GROUP):
                p = jnp.exp2(s_scr[h, t * MOBA_BLOCK:(t + 1) * MOBA_BLOCK, :] - m_new).astype(BF16)
                vblk = vte_ref[h * V_ROWS:(h + 1) * V_ROWS,
                               pl.ds(start + t * MOBA_BLOCK, MOBA_BLOCK)]
                d = _dot(vblk, p)
                pv = d if pv is None else pv + d
            acc_scr[h] = jnp.exp2(m_old - m_new) * acc_scr[h] + pv
            m_scr[h] = m_new

    scores(0, sa_scr, gma_scr)

    def pair_of_groups(i, carry):
        g = 2 * i
        scores(g + 1, sb_scr, gmb_scr)
        attend(g, sa_scr, gma_scr)

        @pl.when(g + 1 < n_groups)
        def _():
            scores(g + 2, sa_scr, gma_scr)
            attend(g + 1, sb_scr, gmb_scr)

        return carry

    lax.fori_loop(0, (n_groups + 1) // 2, pair_of_groups, 0)

    outs = [acc_scr[h, 0:DH_A, :] / acc_scr[h, DH_A:DH_A + 1, :] for h in range(2)]
    o_ref[...] = jnp.concatenate(outs, axis=0).T.astype(BF16)


def _moba_prompt(qt, ka, vte, kmean, table, far, batch):
    m = qt.shape[1]
    seq = m // batch
    nb = seq // MOBA_BLOCK
    assert nb <= MAX_BLOCKS and nb % MOBA_GROUP == 0
    tq = MOBA_BLOCK
    n_pairs = H_A // 2
    gk = MOBA_GROUP * MOBA_BLOCK
    return pl.pallas_call(
        _moba_prompt_kernel,
        grid=(batch, n_pairs, nb),
        in_specs=[pl.BlockSpec(memory_space=pltpu.SMEM),
                  pl.BlockSpec((PAIR, tq), lambda b, hp, qi: (hp, b * nb + qi)),
                  pl.BlockSpec((seq, K_AUG), lambda b, hp, qi: (b, hp)),
                  pl.BlockSpec((2 * V_ROWS, seq), lambda b, hp, qi: (hp, b)),
                  pl.BlockSpec((1, nb, PAIR), lambda b, hp, qi: (b, 0, hp)),
                  pl.BlockSpec((2, N_TILES, tq, tq), lambda b, hp, qi: (hp, 0, 0, 0))],
        out_specs=pl.BlockSpec((tq, PAIR), lambda b, hp, qi: (b * nb + qi, hp)),
        out_shape=jax.ShapeDtypeStruct((m, MW_A), BF16),
        scratch_shapes=[pltpu.VMEM((2, K_AUG, tq), BF16),
                        pltpu.VMEM((2, 1, tq), F32),
                        pltpu.VMEM((2, V_ROWS, tq), F32),
                        pltpu.VMEM((2, gk, tq), F32),
                        pltpu.VMEM((2, gk, tq), F32),
                        pltpu.VMEM((2, 1, tq), F32),
                        pltpu.VMEM((2, 1, tq), F32)],
        compiler_params=_params("parallel", "parallel", "arbitrary"),
        name="moba_prompt",
    )(far, qt, ka, vte, kmean.reshape(batch, nb, MW_A), table)


def _logits_sample_kernel(pt_ref, qt_ref, *refs):
    o_ref, qb_scr = refs[-2], refs[-1]
    b = pl.program_id(0)
    nb_s = qt_ref.shape[1]

    @pl.when(pl.program_id(1) == 0)
    def _():
        onehot = lax.broadcasted_iota(jnp.int32, (1, nb_s), 1) == b
        qcol = jnp.sum(jnp.where(onehot, qt_ref[...], 0.0), axis=1, keepdims=True)
        qb_scr[...] = jnp.broadcast_to(qcol * Q_SCALE, qb_scr.shape)

    per_block = MOBA_BLOCK // PAGE_SIZE
    for i in range(PAGES_PER_STEP):
        for h in range(H_A):
            kt = refs[i][0, 0, h]
            lg = jnp.sum(kt * qb_scr[h * DH_A:(h + 1) * DH_A, :], axis=0, keepdims=True)
            r, half = i // per_block, i % per_block
            o_ref[0, h, r:r + 1, half * PAGE_SIZE:(half + 1) * PAGE_SIZE] = lg


def _logits_sample(cache_kt, page_table, qt, layer):
    nb_s, n_pages = page_table.shape
    steps = n_pages // PAGES_PER_STEP
    rows = PAGES_PER_STEP * PAGE_SIZE // MOBA_BLOCK

    def page_spec(i):
        return pl.BlockSpec((1, 1, H_A, DH_A, PAGE_SIZE),
                            lambda b, s, pt: (layer, pt[b, s * PAGES_PER_STEP + i], 0, 0, 0))

    return pl.pallas_call(
        _logits_sample_kernel,
        grid_spec=pltpu.PrefetchScalarGridSpec(
            num_scalar_prefetch=1,
            grid=(nb_s, steps),
            in_specs=[pl.BlockSpec(qt.shape, lambda b, s, pt: (0, 0))]
                     + [page_spec(i) for i in range(PAGES_PER_STEP)],
            out_specs=pl.BlockSpec((1, H_A, rows, MOBA_BLOCK), lambda b, s, pt: (b, 0, s, 0)),
            scratch_shapes=[pltpu.VMEM((MW_A, PAGE_SIZE), F32)]),
        out_shape=jax.ShapeDtypeStruct((nb_s, H_A, steps * rows, MOBA_BLOCK), F32),
        compiler_params=_params("parallel", "arbitrary"),
        name="logits_sample",
    )(page_table, qt, *([cache_kt] * PAGES_PER_STEP))


def _select_sample_kernel(lg_ref, o_ref):
    sc = jnp.sum(lg_ref[...], axis=-1)
    nblk = sc.shape[-1]
    blk = lax.broadcasted_iota(jnp.int32, sc.shape, 2)
    for t in range(MOBA_TOPK):
        mx = jnp.max(sc, axis=2, keepdims=True)
        idx = jnp.min(jnp.where(sc == mx, blk, nblk), axis=2, keepdims=True)
        o_ref[:, :, t:t + 1] = idx
        sc = jnp.where(blk == idx, -jnp.inf, sc)


def _select_sample(logits):
    nb_s = logits.shape[0]
    return pl.pallas_call(
        _select_sample_kernel,
        out_shape=jax.ShapeDtypeStruct((nb_s, H_A, MOBA_TOPK), jnp.int32),
        compiler_params=pltpu.CompilerParams(vmem_limit_bytes=VMEM_LIMIT),
        name="select_sample",
    )(logits)


def _bias_sample_kernel(rel_ref, o_ref, *, past_len):
    h = pl.program_id(0)
    nblk = o_ref.shape[1]
    kpos = (lax.broadcasted_iota(jnp.int32, (nblk, MOBA_BLOCK), 0) * MOBA_BLOCK
            + lax.broadcasted_iota(jnp.int32, (nblk, MOBA_BLOCK), 1))
    o_ref[0] = _t5_bias(past_len - kpos, rel_ref, h)


def _bias_sample(rel_bias, past_len):
    nblk = past_len // MOBA_BLOCK
    return pl.pallas_call(
        functools.partial(_bias_sample_kernel, past_len=past_len),
        grid=(H_A,),
        in_specs=[pl.BlockSpec(memory_space=pltpu.SMEM)],
        out_specs=pl.BlockSpec((1, nblk, MOBA_BLOCK), lambda h: (h, 0, 0)),
        out_shape=jax.ShapeDtypeStruct((H_A, nblk, MOBA_BLOCK), F32),
        compiler_params=_params("parallel"),
        name="bias_sample",
    )(rel_bias)


def _attend_sample_kernel(pt_ref, sel_ref, rel_ref, lg_ref, bias_ref, qkv_ref, *refs):
    o_ref = refs[-1]
    v_refs = refs[:-1]
    per_block = MOBA_BLOCK // PAGE_SIZE
    b = pl.program_id(0)
    h = pl.program_id(1)

    q = qkv_ref[0, 0, pl.ds(h, 1), :] * Q_SCALE
    k_new = qkv_ref[1, 0, pl.ds(h, 1), :]
    v_new = qkv_ref[2, 0, pl.ds(h, 1), :]
    own = jnp.sum(k_new * q, axis=1, keepdims=True) + rel_ref[0, h]
    logits = []
    mx = own
    for t in range(MOBA_TOPK):
        blk = sel_ref[b, h, t]
        lg = lg_ref[0, 0, pl.ds(blk, 1), :] + bias_ref[0, pl.ds(blk, 1), :]
        logits.append(lg)
        mx = jnp.maximum(mx, jnp.max(lg, axis=1, keepdims=True))
    p_own = jnp.exp(own - mx)
    den = p_own
    acc = p_own * v_new
    for t in range(MOBA_TOPK):
        p = jnp.exp(logits[t] - mx)
        den = den + jnp.sum(p, axis=1, keepdims=True)
        pb = p.astype(BF16)
        for i in range(per_block):
            vt = v_refs[t * per_block + i][0, 0, 0].astype(BF16)
            acc = acc + _dot_nt(pb[:, i * PAGE_SIZE:(i + 1) * PAGE_SIZE], vt)
    o_ref[0, pl.ds(h, 1), :] = acc / den


def _attend_sample(cache_vt, page_table, sel, rel_bias, bias_s, logits, qkv, layer):
    nb_s = page_table.shape[0]
    per_block = MOBA_BLOCK // PAGE_SIZE
    n_sel_pages = MOBA_TOPK * per_block
    nblk = logits.shape[2]

    def page_spec(t):
        def imap(b, h, pt, sl, rel):
            page = sl[b, h, t // per_block] * per_block + (t % per_block)
            return (layer, pt[b, page], h, 0, 0)
        return pl.BlockSpec((1, 1, 1, DH_A, PAGE_SIZE), imap)

    return pl.pallas_call(
        _attend_sample_kernel,
        grid_spec=pltpu.PrefetchScalarGridSpec(
            num_scalar_prefetch=3,
            grid=(nb_s, H_A),
            in_specs=[pl.BlockSpec((1, 1, nblk, MOBA_BLOCK), lambda b, h, pt, sl, rel: (b, h, 0, 0)),
                      pl.BlockSpec((1, nblk, MOBA_BLOCK), lambda b, h, pt, sl, rel: (h, 0, 0)),
                      pl.BlockSpec((3, 1, H_A, DH_A), lambda b, h, pt, sl, rel: (0, b, 0, 0))]
                     + [page_spec(t) for t in range(n_sel_pages)],
            out_specs=pl.BlockSpec((1, H_A, DH_A), lambda b, h, pt, sl, rel: (b, 0, 0))),
        out_shape=jax.ShapeDtypeStruct((nb_s, H_A, DH_A), F32),
        compiler_params=_params("parallel", "arbitrary"),
        name="attend_sample",
    )(page_table, sel, rel_bias, logits, bias_s, qkv, *([cache_vt] * n_sel_pages))


def _mix_kernel(x_ref, hm_ref, ha_ref, sg_ref, wbm_ref, wba_ref, wo_ref, g_ref, o_ref):
    sg = sg_ref[...].astype(F32)
    mixed = (sg[:, :D_MODEL] * _dot(hm_ref[...], wbm_ref[...])
             + sg[:, D_MODEL:] * _dot(ha_ref[...], wba_ref[...]))
    y = _dot(mixed.astype(BF16), wo_ref[...])
    o_ref[...] = x_ref[...] + _rms(y, g_ref[...])


def _mix(x, hm, ha, sg, wbm, wba, wo, g):
    m = x.shape[0]
    tm = min(TM_PROJ, m)
    row = lambda w: pl.BlockSpec((tm, w), lambda i: (i, 0))
    full = lambda a: pl.BlockSpec(a.shape, lambda i: (0,) * a.ndim)
    return pl.pallas_call(
        _mix_kernel,
        grid=(m // tm,),
        in_specs=[row(D_MODEL), row(MW_M), row(MW_A), row(2 * D_MODEL),
                  full(wbm), full(wba), full(wo), full(g)],
        out_specs=row(D_MODEL),
        out_shape=jax.ShapeDtypeStruct((m, D_MODEL), F32),
        compiler_params=_params("parallel"),
        name="mix",
    )(x, hm, ha, sg, wbm, wba, wo, g)


def _ffn_kernel(x_ref, gpre_ref, gpost_ref, wg_ref, wu_ref, wd_ref, o_ref):
    x = x_ref[...]
    hf = _rms(x, gpre_ref[...]).astype(BF16)
    acc = jnp.zeros(x.shape, F32)
    for c in range(D_FF // FF_CHUNK):
        sl = slice(c * FF_CHUNK, (c + 1) * FF_CHUNK)
        gate = _dot(hf, wg_ref[:, sl])
        up = _dot(hf, wu_ref[:, sl])
        mid = (gate * jax.nn.sigmoid(gate) * up).astype(BF16)
        acc = acc + _dot(mid, wd_ref[sl, :])
    o_ref[...] = x + _rms(acc, gpost_ref[...])


def _ffn(x, gpre, gpost, wg, wu, wd):
    m = x.shape[0]
    tm = min(TM_FFN, m)
    row = pl.BlockSpec((tm, D_MODEL), lambda i: (i, 0))
    full = lambda a: pl.BlockSpec(a.shape, lambda i: (0,) * a.ndim)
    return pl.pallas_call(
        _ffn_kernel,
        grid=(m // tm,),
        in_specs=[row, full(gpre), full(gpost), full(wg), full(wu), full(wd)],
        out_specs=row,
        out_shape=jax.ShapeDtypeStruct((m, D_MODEL), F32),
        compiler_params=_params("parallel"),
        name="ffn",
    )(x, gpre, gpost, wg, wu, wd)


def kernel(x_prompt, x_sample, cache_k, cache_v, state_C, state_n, state_m, page_table,
           norm_mix_pre, norm_mix_post, norm_ffn_pre, norm_ffn_post, w_in, b_igate, b_fgate,
           g_mlstm, w_branch_m, w_branch_a, w_out, w_ffn_gate, w_ffn_up, w_ffn_down, rel_bias):
    bp, seq, _ = x_prompt.shape
    bs = x_sample.shape[0]
    depth = w_in.shape[0]
    xp = x_prompt.reshape(bp * seq, D_MODEL)
    xs = x_sample.reshape(bs, D_MODEL)
    cache_kt = jnp.transpose(cache_k, (0, 1, 3, 4, 2))
    cache_vt = jnp.transpose(cache_v, (0, 1, 3, 4, 2))
    w_in_t = jnp.swapaxes(w_in, 1, 2)

    tile = (H_A, 1, MOBA_BLOCK, MOBA_BLOCK)
    table = jnp.concatenate([_bias_table(rel_bias), jnp.zeros(tile, F32), jnp.full(tile, NEG, F32)],
                            axis=1)
    far = rel_bias[N_BUCKETS - 1] * LOG2E
    bias_s = _bias_sample(rel_bias, page_table.shape[1] * PAGE_SIZE)

    outs = {k: [] for k in ("kp", "vp", "cp", "np", "mp", "ks", "vs", "cs", "ns", "ms")}
    for l in range(depth):
        row = lambda a: a[l].reshape(1, -1)
        wt = w_in_t[l]
        wm = wt[:COL_M_END].astype(BF16)
        wif = wt[COL_M_END:COL_IF_END]
        wq = wt[COL_IF_END:COL_Q_END].astype(BF16)
        wk = wt[COL_Q_END:COL_K_END].astype(BF16)
        wv = wt[COL_K_END:COL_V_END].astype(BF16)
        wgt = wt[COL_V_END:].astype(BF16)
        b_if = jnp.concatenate([b_igate[l], b_fgate[l]])
        wbm, wba, wo = (w_branch_m[l].astype(BF16), w_branch_a[l].astype(BF16), w_out[l].astype(BF16))
        wfg, wfu, wfd = (w_ffn_gate[l].astype(BF16), w_ffn_up[l].astype(BF16), w_ffn_down[l].astype(BF16))

        mproj, gcol, grow, qt, ka, kt32, vt32, vte, kmean, sg = _inproj_prompt(
            xp, row(norm_mix_pre), wm, wif, wq, wk, wv, wgt, bp)
        hm, cext, mstate = _mlstm_prompt(mproj, gcol, grow, b_if, g_mlstm[l], bp)
        ha = _moba_prompt(qt, ka, vte, kmean, table, far, bp)
        xp = _mix(xp, hm, ha, sg, wbm, wba, wo, row(norm_mix_post))
        xp = _ffn(xp, row(norm_ffn_pre), row(norm_ffn_post), wfg, wfu, wfd)
        outs["kp"].append(kt32)
        outs["vp"].append(vt32)
        outs["cp"].append(cext[..., :DH_M])
        outs["np"].append(cext[..., DH_M])
        outs["mp"].append(mstate[:, :, 0, 0])

        mproj_s, mproj_t, gcol_s, qkv_s, qt_s, sg_s = _inproj_sample(
            xs, row(norm_mix_pre), wm, wif, wq, wk, wv, wgt)
        hm_s, c_s, n_s, m_s = _mlstm_sample(mproj_s, mproj_t, gcol_s, b_if, g_mlstm[l],
                                            state_C[l], state_n[l], state_m[l])
        logits_s = _logits_sample(cache_kt, page_table, qt_s, l)
        sel = _select_sample(logits_s)
        ha_s = _attend_sample(cache_vt, page_table, sel, rel_bias, bias_s, logits_s,
                              qkv_s.reshape(3, bs, H_A, DH_A), l)
        xs = _mix(xs, hm_s.reshape(bs, MW_M), ha_s.reshape(bs, MW_A).astype(BF16), sg_s,
                  wbm, wba, wo, row(norm_mix_post))
        xs = _ffn(xs, row(norm_ffn_pre), row(norm_ffn_post), wfg, wfu, wfd)
        outs["ks"].append(qkv_s[1].reshape(bs, 1, H_A, DH_A))
        outs["vs"].append(qkv_s[2].reshape(bs, 1, H_A, DH_A))
        outs["cs"].append(c_s)
        outs["ns"].append(n_s)
        outs["ms"].append(m_s.reshape(bs, H_M))

    st = lambda k: jnp.stack(outs[k])
    kv_out = lambda k: jnp.transpose(st(k).reshape(depth, bp, H_A, DH_A, seq), (0, 1, 4, 2, 3))
    return (xp.reshape(bp, seq, D_MODEL), xs.reshape(bs, 1, D_MODEL),
            kv_out("kp"), kv_out("vp"), st("cp"), st("np"), st("mp"),
            st("ks"), st("vs"), st("cs"), st("ns"), st("ms"))
```

```python
import functools
import math

import jax
import jax.numpy as jnp
from jax import lax
from jax.experimental import pallas as pl
from jax.experimental.pallas import tpu as pltpu

F32 = jnp.float32
BF16 = jnp.bfloat16
HIGHEST = lax.Precision.HIGHEST

D_MODEL = 1024
H_M, DH_M = 4, 128
MW_M = H_M * DH_M
H_A, DH_A = 8, 64
MW_A = H_A * DH_A
MOBA_BLOCK = 256
MOBA_TOPK = 3
N_BUCKETS = 32
REL_MAX_DIST = 2048
D_FF = 2816
EPS = 1e-6
NEG = -1e30
PAGE_SIZE = 128

N_GATE_COLS = 2 * H_M
COL_M_END = 4 * MW_M
COL_IF_END = COL_M_END + N_GATE_COLS
COL_Q_END = COL_IF_END + MW_A
COL_K_END = COL_Q_END + MW_A
COL_V_END = COL_K_END + MW_A
LOG_K_SCALE = math.log(DH_M ** -0.5)
Q_SCALE = DH_A ** -0.5
LOG2E = math.log2(math.e)

N_NEAR = (REL_MAX_DIST + MOBA_BLOCK - 1) // MOBA_BLOCK + 1
assert (N_NEAR * MOBA_BLOCK - (MOBA_BLOCK - 1)) >= REL_MAX_DIST
TILE_ZERO = N_NEAR
TILE_MASKED = N_NEAR + 1
N_TILES = N_NEAR + 2
MOBA_GROUP = 4

PAIR = 2 * DH_A
K_AUG = 2 * PAIR
SEL_HI, SEL_LO = PAIR, PAIR + 32
V_ROWS = DH_A + 16
MAX_BLOCKS = 32

VMEM_LIMIT = 56 * 1024 * 1024

TM_PROJ = 512
TM_FFN = 256
L_CHUNK = 256
FF_CHUNK = 1408
PAGES_PER_STEP = 16


def _params(*sem):
    return pltpu.CompilerParams(dimension_semantics=sem, vmem_limit_bytes=VMEM_LIMIT)


def _rms(x, g):
    return x * lax.rsqrt(jnp.mean(x * x, axis=-1, keepdims=True) + EPS) * g


def _log_sigmoid(x):
    return jnp.minimum(x, 0.0) - jnp.log(1.0 + jnp.exp(-jnp.abs(x)))


def _dot(a, b, precision=None):
    return jnp.dot(a, b, precision=precision, preferred_element_type=F32)


def _dot_nt(a, b, precision=None):
    return lax.dot_general(a, b, (((1,), (1,)), ((), ())), precision=precision,
                           preferred_element_type=F32)


def _dot_tn(a, b):
    return lax.dot_general(a, b, (((0,), (0,)), ((), ())), preferred_element_type=F32)


def _inproj_prompt_kernel(x_ref, g_ref, wm_ref, wif_ref, wq_ref, wk_ref, wv_ref, wg_ref,
                          m_ref, gc_ref, gr_ref, qt_ref, ka_ref, kt32_ref, vt32_ref, vte_ref,
                          km_ref, sg_ref):
    tm = x_ref.shape[0]
    xn = _rms(x_ref[...], g_ref[...])
    xb = xn.astype(BF16)
    m_ref[...] = _dot_nt(xb, wm_ref[...]).astype(BF16)
    gates_t = _dot_nt(wif_ref[...], xn, HIGHEST)
    gr_ref[...] = gates_t
    gc_ref[...] = gates_t.T
    qt_ref[...] = (_dot_nt(wq_ref[...], xb) * (Q_SCALE * LOG2E)).astype(BF16)

    k = _dot_nt(xb, wk_ref[...])
    kt32_ref[0] = k.T
    km_ref[0] = jnp.mean(k.reshape(tm // MOBA_BLOCK, MOBA_BLOCK, MW_A), axis=1)
    row = lax.broadcasted_iota(jnp.int32, (tm, PAIR), 0)
    lane = lax.broadcasted_iota(jnp.int32, (tm, PAIR), 1)
    blk = (pl.program_id(1) * tm + row) // MOBA_BLOCK
    onehot = jnp.where(jnp.logical_and(lane < 2 * MAX_BLOCKS, lane % MAX_BLOCKS == blk),
                       1.0, 0.0).astype(BF16)
    kb = k.astype(BF16)
    for p in range(H_A // 2):
        ka_ref[:, p * K_AUG:p * K_AUG + PAIR] = kb[:, p * PAIR:(p + 1) * PAIR]
        ka_ref[:, p * K_AUG + PAIR:(p + 1) * K_AUG] = onehot

    vt = _dot_nt(wv_ref[...], xb)
    vt32_ref[0] = vt
    ones_rows = jnp.where(lax.broadcasted_iota(jnp.int32, (V_ROWS - DH_A, tm), 0) == 0,
                          1.0, 0.0).astype(BF16)
    for h in range(H_A):
        vte_ref[h * V_ROWS:h * V_ROWS + DH_A, :] = vt[h * DH_A:(h + 1) * DH_A].astype(BF16)
        vte_ref[h * V_ROWS + DH_A:(h + 1) * V_ROWS, :] = ones_rows
    sg_ref[...] = jax.nn.sigmoid(_dot_nt(xb, wg_ref[...])).astype(BF16)


def _inproj_prompt(x, g, wm, wif, wq, wk, wv, wg, batch):
    m = x.shape[0]
    seq = m // batch
    tm = TM_PROJ
    nt = seq // tm
    n_pairs = H_A // 2
    row = lambda w: pl.BlockSpec((tm, w), lambda b, t: (b * nt + t, 0))
    col = lambda h: pl.BlockSpec((h, tm), lambda b, t: (0, b * nt + t))
    full = lambda a: pl.BlockSpec(a.shape, lambda b, t: (0,) * a.ndim)
    return pl.pallas_call(
        _inproj_prompt_kernel,
        grid=(batch, nt),
        in_specs=[row(D_MODEL), full(g), full(wm), full(wif), full(wq), full(wk), full(wv), full(wg)],
        out_specs=[row(COL_M_END), row(N_GATE_COLS), col(N_GATE_COLS), col(MW_A),
                   row(n_pairs * K_AUG),
                   pl.BlockSpec((1, MW_A, tm), lambda b, t: (b, 0, t)),
                   pl.BlockSpec((1, MW_A, tm), lambda b, t: (b, 0, t)),
                   col(H_A * V_ROWS),
                   pl.BlockSpec((1, tm // MOBA_BLOCK, MW_A), lambda b, t: (b * nt + t, 0, 0)),
                   row(2 * D_MODEL)],
        out_shape=[jax.ShapeDtypeStruct((m, COL_M_END), BF16),
                   jax.ShapeDtypeStruct((m, N_GATE_COLS), F32),
                   jax.ShapeDtypeStruct((N_GATE_COLS, m), F32),
                   jax.ShapeDtypeStruct((MW_A, m), BF16),
                   jax.ShapeDtypeStruct((m, n_pairs * K_AUG), BF16),
                   jax.ShapeDtypeStruct((batch, MW_A, seq), F32),
                   jax.ShapeDtypeStruct((batch, MW_A, seq), F32),
                   jax.ShapeDtypeStruct((H_A * V_ROWS, m), BF16),
                   jax.ShapeDtypeStruct((m // tm, tm // MOBA_BLOCK, MW_A), F32),
                   jax.ShapeDtypeStruct((m, 2 * D_MODEL), BF16)],
        compiler_params=_params("parallel", "parallel"),
        name="inproj_prompt",
    )(x, g, wm, wif, wq, wk, wv, wg)


def _inproj_sample_kernel(x_ref, g_ref, wm_ref, wif_ref, wq_ref, wk_ref, wv_ref, wg_ref,
                          m_ref, mt_ref, gc_ref, qkv_ref, qt_ref, sg_ref):
    xn = _rms(x_ref[...], g_ref[...])
    xb = xn.astype(BF16)
    m_ref[...] = _dot_nt(xb, wm_ref[...])
    mt_ref[...] = _dot_nt(wm_ref[:2 * MW_M, :], xb)
    gc_ref[...] = _dot_nt(xn, wif_ref[...], HIGHEST)
    qkv_ref[0] = _dot_nt(xb, wq_ref[...])
    qkv_ref[1] = _dot_nt(xb, wk_ref[...])
    qkv_ref[2] = _dot_nt(xb, wv_ref[...])
    qt_ref[...] = _dot_nt(wq_ref[...], xb)
    sg_ref[...] = jax.nn.sigmoid(_dot_nt(xb, wg_ref[...])).astype(BF16)


def _inproj_sample(x, g, wm, wif, wq, wk, wv, wg):
    m = x.shape[0]
    return pl.pallas_call(
        _inproj_sample_kernel,
        out_shape=[jax.ShapeDtypeStruct((m, COL_M_END), F32),
                   jax.ShapeDtypeStruct((2 * MW_M, m), F32),
                   jax.ShapeDtypeStruct((m, N_GATE_COLS), F32),
                   jax.ShapeDtypeStruct((3, m, MW_A), F32),
                   jax.ShapeDtypeStruct((MW_A, m), F32),
                   jax.ShapeDtypeStruct((m, 2 * D_MODEL), BF16)],
        compiler_params=pltpu.CompilerParams(vmem_limit_bytes=VMEM_LIMIT),
        name="inproj_sample",
    )(x, g, wm, wif, wq, wk, wv, wg)


def _mlstm_prompt_kernel(m_ref, gc_ref, gr_ref, bc_ref, br_ref, gm_ref,
                         h_ref, c_ref, ms_ref):
    L = m_ref.shape[0]

    @pl.when(pl.program_id(1) == 0)
    def _():
        c_ref[...] = jnp.zeros_like(c_ref)
        ms_ref[...] = jnp.zeros_like(ms_ref)

    row = lax.broadcasted_iota(jnp.int32, (L, L), 0)
    col = lax.broadcasted_iota(jnp.int32, (L, L), 1)
    causal = col <= row
    lower = causal.astype(F32)
    upper = (row <= col).astype(F32)

    gcol = gc_ref[...] + br_ref[...]
    grow = gr_ref[...] + bc_ref[...]
    bcol = _dot(lower, _log_sigmoid(gcol), HIGHEST)
    brow = _dot(_log_sigmoid(grow), upper, HIGHEST)

    lane = lax.broadcasted_iota(jnp.int32, (L, DH_M), 1)
    ones_blk = jnp.where(lane == 0, 1.0, 0.0).astype(BF16)

    def operands(h):
        return (m_ref[:, h * DH_M:(h + 1) * DH_M],
                m_ref[:, MW_M + h * DH_M:MW_M + (h + 1) * DH_M],
                m_ref[:, 2 * MW_M + h * DH_M:2 * MW_M + (h + 1) * DH_M])

    qk_all = [_dot_nt(operands(h)[0], operands(h)[1]) for h in range(H_M)]
    qc_all = [_dot(operands(h)[0], c_ref[0, h].astype(BF16)) for h in range(H_M)]

    for h in range(H_M):
        q, k, v = operands(h)
        o = m_ref[:, 3 * MW_M + h * DH_M:3 * MW_M + (h + 1) * DH_M]
        i_col = gcol[:, h:h + 1]
        b_col = bcol[:, H_M + h:H_M + h + 1]
        i_row = grow[h:h + 1, :]
        b_row = brow[H_M + h:H_M + h + 1, :]
        b_last = b_col[L - 1:L, :]
        cext = c_ref[0, h]
        m_prev = ms_ref[0, h][0:1, 0:1]

        d = jnp.where(causal, b_col - (b_row - i_row), NEG)
        inter = b_col + m_prev
        mt = jnp.maximum(inter, jnp.max(d, axis=1, keepdims=True))
        a = jnp.exp(inter - mt)
        s = qk_all[h] * jnp.exp(d - (mt - LOG_K_SCALE))
        v_ext = jnp.concatenate([v, ones_blk], axis=1)
        nd = a * qc_all[h] + _dot(s.astype(BF16), v_ext)
        den = nd[:, DH_M:DH_M + 1]
        hh = nd[:, :DH_M] / jnp.maximum(jnp.abs(den), jnp.exp(-mt))

        mu = jnp.mean(hh, axis=1, keepdims=True)
        hc = hh - mu
        var = jnp.mean(hc * hc, axis=1, keepdims=True)
        hn = hc * lax.rsqrt(var + EPS) * gm_ref[:, h * DH_M:(h + 1) * DH_M]
        h_ref[:, h * DH_M:(h + 1) * DH_M] = (hn * jax.nn.sigmoid(o.astype(F32))).astype(BF16)

        g_row = b_last - b_row + i_row
        e = b_last + m_prev
        m_new = jnp.maximum(e, jnp.max(g_row, axis=1, keepdims=True))
        w_col = jnp.exp(b_last - b_col + i_col - (m_new - LOG_K_SCALE))
        kw = (k.astype(F32) * w_col).astype(BF16)
        c_ref[0, h] = jnp.exp(e - m_new) * cext + _dot_tn(kw, v_ext)
        ms_ref[0, h] = jnp.broadcast_to(m_new, ms_ref.shape[2:])


def _mlstm_prompt(mproj, gcol, grow, b_if, g_mlstm, batch):
    m = mproj.shape[0]
    L = L_CHUNK
    nc = m // batch // L
    bias_row = b_if.reshape(1, N_GATE_COLS)
    bias_col = b_if.reshape(N_GATE_COLS, 1)
    gm = g_mlstm.reshape(1, MW_M)
    return pl.pallas_call(
        _mlstm_prompt_kernel,
        grid=(batch, nc),
        in_specs=[pl.BlockSpec((L, COL_M_END), lambda b, c: (b * nc + c, 0)),
                  pl.BlockSpec((L, N_GATE_COLS), lambda b, c: (b * nc + c, 0)),
                  pl.BlockSpec((N_GATE_COLS, L), lambda b, c: (0, b * nc + c)),
                  pl.BlockSpec((N_GATE_COLS, 1), lambda b, c: (0, 0)),
                  pl.BlockSpec((1, N_GATE_COLS), lambda b, c: (0, 0)),
                  pl.BlockSpec((1, MW_M), lambda b, c: (0, 0))],
        out_specs=[pl.BlockSpec((L, MW_M), lambda b, c: (b * nc + c, 0)),
                   pl.BlockSpec((1, H_M, DH_M, 2 * DH_M), lambda b, c: (b, 0, 0, 0)),
                   pl.BlockSpec((1, H_M, 8, 128), lambda b, c: (b, 0, 0, 0))],
        out_shape=[jax.ShapeDtypeStruct((m, MW_M), BF16),
                   jax.ShapeDtypeStruct((batch, H_M, DH_M, 2 * DH_M), F32),
                   jax.ShapeDtypeStruct((batch, H_M, 8, 128), F32)],
        compiler_params=_params("parallel", "arbitrary"),
        name="mlstm_prompt",
    )(mproj, gcol, grow, bias_col, bias_row, gm)


def _mlstm_sample_kernel(m_ref, mt_ref, gc_ref, bi_ref, gm_ref, c_ref, n_ref, ms_ref,
                         h_ref, co_ref, no_ref, mo_ref):
    b = pl.program_id(0)
    nb = mt_ref.shape[1]
    onehot = lax.broadcasted_iota(jnp.int32, (1, nb), 1) == b
    g = gc_ref[0] + bi_ref[...]
    for h in range(H_M):
        sl = slice(h * DH_M, (h + 1) * DH_M)
        q_row = m_ref[0, :, h * DH_M:(h + 1) * DH_M]
        k_row = m_ref[0, :, MW_M + h * DH_M:MW_M + (h + 1) * DH_M]
        v_row = m_ref[0, :, 2 * MW_M + h * DH_M:2 * MW_M + (h + 1) * DH_M]
        o_row = m_ref[0, :, 3 * MW_M + h * DH_M:3 * MW_M + (h + 1) * DH_M]
        q_col = jnp.sum(jnp.where(onehot, mt_ref[sl, :], 0.0), axis=1, keepdims=True)
        k_col = jnp.sum(jnp.where(onehot, mt_ref[MW_M + h * DH_M:MW_M + (h + 1) * DH_M, :], 0.0),
                        axis=1, keepdims=True)
        i_pre = g[:, h:h + 1]
        log_f = _log_sigmoid(g[:, H_M + h:H_M + h + 1])
        m_prev = ms_ref[0, :, h:h + 1]
        inter = log_f + m_prev
        m_new = jnp.maximum(inter, i_pre)
        a = jnp.exp(inter - m_new)
        w = jnp.exp(i_pre - (m_new - LOG_K_SCALE))
        c = c_ref[0, h]
        n = n_ref[0, h:h + 1, :]
        s = jnp.sum(q_row * k_row, axis=1, keepdims=True) * w
        num = a * jnp.sum(c * q_col, axis=0, keepdims=True) + s * v_row
        den = a * jnp.sum(q_row * n, axis=1, keepdims=True) + s
        hh = num / jnp.maximum(jnp.abs(den), jnp.exp(-m_new))
        mu = jnp.mean(hh, axis=1, keepdims=True)
        hc = hh - mu
        var = jnp.mean(hc * hc, axis=1, keepdims=True)
        hn = hc * lax.rsqrt(var + EPS) * gm_ref[:, sl]
        h_ref[0, :, sl] = (hn * jax.nn.sigmoid(o_row)).astype(BF16)
        co_ref[0, h] = a * c + (w * k_col) * v_row
        no_ref[0, h:h + 1, :] = a * n + w * k_row
        mo_ref[0, :, h:h + 1] = m_new


def _mlstm_sample(mproj, mproj_t, gcol, b_if, g_mlstm, c0, n0, m0):
    nb = mproj.shape[0]
    return pl.pallas_call(
        _mlstm_sample_kernel,
        grid=(nb,),
        in_specs=[pl.BlockSpec((1, 1, COL_M_END), lambda b: (b, 0, 0)),
                  pl.BlockSpec((2 * MW_M, nb), lambda b: (0, 0)),
                  pl.BlockSpec((1, 1, N_GATE_COLS), lambda b: (b, 0, 0)),
                  pl.BlockSpec((1, N_GATE_COLS), lambda b: (0, 0)),
                  pl.BlockSpec((1, MW_M), lambda b: (0, 0)),
                  pl.BlockSpec((1, H_M, DH_M, DH_M), lambda b: (b, 0, 0, 0)),
                  pl.BlockSpec((1, H_M, DH_M), lambda b: (b, 0, 0)),
                  pl.BlockSpec((1, 1, H_M), lambda b: (b, 0, 0))],
        out_specs=[pl.BlockSpec((1, 1, MW_M), lambda b: (b, 0, 0)),
                   pl.BlockSpec((1, H_M, DH_M, DH_M), lambda b: (b, 0, 0, 0)),
                   pl.BlockSpec((1, H_M, DH_M), lambda b: (b, 0, 0)),
                   pl.BlockSpec((1, 1, H_M), lambda b: (b, 0, 0))],
        out_shape=[jax.ShapeDtypeStruct((nb, 1, MW_M), BF16),
                   jax.ShapeDtypeStruct((nb, H_M, DH_M, DH_M), F32),
                   jax.ShapeDtypeStruct((nb, H_M, DH_M), F32),
                   jax.ShapeDtypeStruct((nb, 1, H_M), F32)],
        compiler_params=_params("parallel"),
        name="mlstm_sample",
    )(mproj.reshape(nb, 1, COL_M_END), mproj_t, gcol.reshape(nb, 1, N_GATE_COLS),
      b_if.reshape(1, N_GATE_COLS), g_mlstm.reshape(1, MW_M), c0, n0, m0.reshape(nb, 1, H_M))


def _t5_bucket(dist):
    n = jnp.maximum(dist, 0)
    max_exact = N_BUCKETS // 2
    nf = jnp.maximum(n, 1).astype(F32)
    large = max_exact + (jnp.log(nf / max_exact) / math.log(REL_MAX_DIST / max_exact)
                         * (N_BUCKETS - max_exact)).astype(jnp.int32)
    large = jnp.minimum(large, N_BUCKETS - 1)
    return jnp.where(n < max_exact, n, large)


def _t5_bias(dist, rel_ref, h):
    bucket = _t5_bucket(dist)
    bias = jnp.zeros(dist.shape, F32)
    for kb in range(N_BUCKETS):
        bias = jnp.where(bucket == kb, rel_ref[kb, h], bias)
    return bias


def _bias_table_kernel(rel_ref, o_ref):
    h = pl.program_id(0)
    delta = pl.program_id(1)
    c = lax.broadcasted_iota(jnp.int32, (MOBA_BLOCK, MOBA_BLOCK), 0)
    r = lax.broadcasted_iota(jnp.int32, (MOBA_BLOCK, MOBA_BLOCK), 1)
    dist = delta * MOBA_BLOCK + r - c
    o_ref[0, 0] = jnp.where(dist >= 0, _t5_bias(dist, rel_ref, h) * LOG2E, NEG)


def _bias_table(rel_bias):
    return pl.pallas_call(
        _bias_table_kernel,
        grid=(H_A, N_NEAR),
        in_specs=[pl.BlockSpec(memory_space=pltpu.SMEM)],
        out_specs=pl.BlockSpec((1, 1, MOBA_BLOCK, MOBA_BLOCK), lambda h, d: (h, d, 0, 0)),
        out_shape=jax.ShapeDtypeStruct((H_A, N_NEAR, MOBA_BLOCK, MOBA_BLOCK), F32),
        compiler_params=_params("parallel", "parallel"),
        name="bias_table",
    )(rel_bias)


def _select_topk_t(score, n_valid_rows, k):
    nb = score.shape[0]
    row = lax.broadcasted_iota(jnp.int32, score.shape, 0)
    past = row < n_valid_rows
    sc = jnp.where(past, score, -jnp.inf)
    sel = jnp.full(score.shape, NEG, F32)
    for _ in range(k):
        mx = jnp.max(sc, axis=0, keepdims=True)
        idx = jnp.min(jnp.where(sc == mx, row, nb), axis=0, keepdims=True)
        hit = row == idx
        sel = jnp.where(hit, 0.0, sel)
        sc = jnp.where(hit, -jnp.inf, sc)
    return jnp.where(past, sel, NEG)


def _moba_prompt_kernel(qi_ref, grp_ref, first_ref, last_ref, far_ref,
                        qt_ref, ka_ref, vte_ref, km_ref, tab_ref, o_ref,
                        qp_scr, m_scr, acc_scr, sa_scr, sb_scr, gma_scr, gmb_scr):
    hp = pl.program_id(1)
    seq = qt_ref.shape[1]
    nb = km_ref.shape[1]
    tq = MOBA_BLOCK
    n_items = qi_ref.shape[0] - 1
    setup_w = 4 * MOBA_BLOCK

    def setup(c, carry):
        cols = pl.ds(pl.multiple_of(c * setup_w, setup_w), setup_w)
        blk_row = lax.broadcasted_iota(jnp.int32, (nb, setup_w), 0)
        q_blk = (c * setup_w + lax.broadcasted_iota(jnp.int32, (1, setup_w), 1)) // MOBA_BLOCK
        sub = lax.broadcasted_iota(jnp.int32, (PAIR, setup_w), 0)
        qpair = qt_ref[:, cols]
        for h in range(2):
            qth = qt_ref[h * DH_A:(h + 1) * DH_A, cols].astype(F32)
            score = _dot(km_ref[0, :, h * DH_A:(h + 1) * DH_A], qth, HIGHEST)
            sel = _select_topk_t(score, q_blk, MOBA_TOPK)
            sel = jnp.where(blk_row == q_blk, 0.0, sel)
            sel = sel + jnp.where(q_blk - blk_row >= N_NEAR, far_ref[2 * hp + h], 0.0)
            hi = sel.astype(BF16)
            qp_scr[h, 0:PAIR, cols] = jnp.where(sub // DH_A == h, qpair, jnp.zeros_like(qpair))
            qp_scr[h, PAIR:, cols] = jnp.zeros((K_AUG - PAIR, setup_w), BF16)
            qp_scr[h, SEL_HI:SEL_HI + nb, cols] = hi
            qp_scr[h, SEL_LO:SEL_LO + nb, cols] = (sel - hi.astype(F32)).astype(BF16)
        return carry

    lax.fori_loop(0, seq // setup_w, setup, 0)
    m_scr[...] = jnp.full(m_scr.shape, NEG, F32)
    acc_scr[...] = jnp.zeros(acc_scr.shape, F32)

    gk = MOBA_GROUP * MOBA_BLOCK

    def score_block(w, t, h):
        qi = qi_ref[w]
        g = grp_ref[w]
        kblk = ka_ref[pl.ds(pl.multiple_of((g * MOBA_GROUP + t) * MOBA_BLOCK, MOBA_BLOCK),
                            MOBA_BLOCK), :]
        delta = qi - (g * MOBA_GROUP + t)
        tile = jnp.where(delta < 0, TILE_MASKED, jnp.minimum(delta, TILE_ZERO))
        qcols = pl.ds(pl.multiple_of(qi * tq, tq), tq)
        return _dot(kblk, qp_scr[h, :, qcols]) + tab_ref[h, tile]

    def step(w, s_cur, gm_cur, s_nxt, gm_nxt):
        qi = qi_ref[w]
        start = pl.multiple_of(grp_ref[w] * gk, gk)
        first = first_ref[w] == 1
        m_old = [jnp.where(first, NEG, m_scr[h]) for h in range(2)]
        m_new = [jnp.maximum(m_old[h], gm_cur[h]) for h in range(2)]
        pv = [None, None]
        gmax = [None, None]
        def next_scores(t):
            rows = slice(t * MOBA_BLOCK, (t + 1) * MOBA_BLOCK)
            for h in range(2):
                s = score_block(w + 1, t, h)
                s_nxt[h, rows, :] = s
                cmax = jnp.max(s, axis=0, keepdims=True)
                gmax[h] = cmax if t == 0 else jnp.maximum(gmax[h], cmax)

        def attend(t):
            rows = slice(t * MOBA_BLOCK, (t + 1) * MOBA_BLOCK)
            for h in range(2):
                p = jnp.exp2(s_cur[h, rows, :] - m_new[h]).astype(BF16)
                vblk = vte_ref[h * V_ROWS:(h + 1) * V_ROWS,
                               pl.ds(start + t * MOBA_BLOCK, MOBA_BLOCK)]
                d = _dot(vblk, p)
                pv[h] = d if t == 0 else pv[h] + d

        for t in range(MOBA_GROUP):
            next_scores(t)
            attend(t)
        for h in range(2):
            acc_old = jnp.where(first, 0.0, acc_scr[h])
            acc_scr[h] = jnp.exp2(m_old[h] - m_new[h]) * acc_old + pv[h]
            m_scr[h] = m_new[h]
            gm_nxt[h] = gmax[h]

        @pl.when(last_ref[w] == 1)
        def _():
            outs = [acc_scr[h, 0:DH_A, :] / acc_scr[h, DH_A:DH_A + 1, :] for h in range(2)]
            orows = pl.ds(pl.multiple_of(qi * tq, tq), tq)
            o_ref[orows, :] = jnp.concatenate(outs, axis=0).T.astype(BF16)

    for h in range(2):
        gmax = None
        for t in range(MOBA_GROUP):
            s = score_block(0, t, h)
            sa_scr[h, t * MOBA_BLOCK:(t + 1) * MOBA_BLOCK, :] = s
            cmax = jnp.max(s, axis=0, keepdims=True)
            gmax = cmax if t == 0 else jnp.maximum(gmax, cmax)
        gma_scr[h] = gmax

    def pair_of_items(i, carry):
        w = 2 * i
        step(w, sa_scr, gma_scr, sb_scr, gmb_scr)
        step(w + 1, sb_scr, gmb_scr, sa_scr, gma_scr)
        return carry

    lax.fori_loop(0, n_items // 2, pair_of_items, 0)


def _moba_work_items(nb):
    qi, grp, first, last = [], [], [], []
    for q in range(nb):
        n_groups = q // MOBA_GROUP + 1
        for g in range(n_groups):
            qi.append(q)
            grp.append(g)
            first.append(int(g == 0))
            last.append(int(g == n_groups - 1))
    assert len(qi) % 2 == 0
    tables = [qi + qi[-1:], grp + grp[-1:], first + [0], last + [0]]
    return [jnp.asarray(t, jnp.int32) for t in tables]


def _moba_prompt(qt, ka, vte, kmean, table, far, batch):
    m = qt.shape[1]
    seq = m // batch
    nb = seq // MOBA_BLOCK
    assert nb <= MAX_BLOCKS and nb % MOBA_GROUP == 0
    tq = MOBA_BLOCK
    n_pairs = H_A // 2
    gk = MOBA_GROUP * MOBA_BLOCK
    smem = pl.BlockSpec(memory_space=pltpu.SMEM)
    return pl.pallas_call(
        _moba_prompt_kernel,
        grid=(batch, n_pairs),
        in_specs=[smem, smem, smem, smem, smem,
                  pl.BlockSpec((PAIR, seq), lambda b, hp: (hp, b)),
                  pl.BlockSpec((seq, K_AUG), lambda b, hp: (b, hp)),
                  pl.BlockSpec((2 * V_ROWS, seq), lambda b, hp: (hp, b)),
                  pl.BlockSpec((1, nb, PAIR), lambda b, hp: (b, 0, hp)),
                  pl.BlockSpec((2, N_TILES, tq, tq), lambda b, hp: (hp, 0, 0, 0),
                               pipeline_mode=pl.Buffered(1))],
        out_specs=pl.BlockSpec((seq, PAIR), lambda b, hp: (b, hp)),
        out_shape=jax.ShapeDtypeStruct((m, MW_A), BF16),
        scratch_shapes=[pltpu.VMEM((2, K_AUG, seq), BF16),
                        pltpu.VMEM((2, 1, tq), F32),
                        pltpu.VMEM((2, V_ROWS, tq), F32),
                        pltpu.VMEM((2, gk, tq), F32),
                        pltpu.VMEM((2, gk, tq), F32),
                        pltpu.VMEM((2, 1, tq), F32),
                        pltpu.VMEM((2, 1, tq), F32)],
        compiler_params=_params("parallel", "arbitrary"),
        name="moba_prompt",
    )(*_moba_work_items(nb), far, qt, ka, vte, kmean.reshape(batch, nb, MW_A), table)


def _logits_sample_kernel(pt_ref, qt_ref, *refs):
    o_ref, qb_scr = refs[-2], refs[-1]
    b = pl.program_id(0)
    nb_s = qt_ref.shape[1]

    @pl.when(pl.program_id(1) == 0)
    def _():
        onehot = lax.broadcasted_iota(jnp.int32, (1, nb_s), 1) == b
        qcol = jnp.sum(jnp.where(onehot, qt_ref[...], 0.0), axis=1, keepdims=True)
        qb_scr[...] = jnp.broadcast_to(qcol * Q_SCALE, qb_scr.shape)

    per_block = MOBA_BLOCK // PAGE_SIZE
    for i in range(PAGES_PER_STEP):
        for h in range(H_A):
            kt = refs[i][0, 0, h]
            lg = jnp.sum(kt * qb_scr[h * DH_A:(h + 1) * DH_A, :], axis=0, keepdims=True)
            r, half = i // per_block, i % per_block
            o_ref[0, h, r:r + 1, half * PAGE_SIZE:(half + 1) * PAGE_SIZE] = lg


def _logits_sample(cache_kt, page_table, qt, layer):
    nb_s, n_pages = page_table.shape
    steps = n_pages // PAGES_PER_STEP
    rows = PAGES_PER_STEP * PAGE_SIZE // MOBA_BLOCK

    def page_spec(i):
        return pl.BlockSpec((1, 1, H_A, DH_A, PAGE_SIZE),
                            lambda b, s, pt: (layer, pt[b, s * PAGES_PER_STEP + i], 0, 0, 0))

    return pl.pallas_call(
        _logits_sample_kernel,
        grid_spec=pltpu.PrefetchScalarGridSpec(
            num_scalar_prefetch=1,
            grid=(nb_s, steps),
            in_specs=[pl.BlockSpec(qt.shape, lambda b, s, pt: (0, 0))]
                     + [page_spec(i) for i in range(PAGES_PER_STEP)],
            out_specs=pl.BlockSpec((1, H_A, rows, MOBA_BLOCK), lambda b, s, pt: (b, 0, s, 0)),
            scratch_shapes=[pltpu.VMEM((MW_A, PAGE_SIZE), F32)]),
        out_shape=jax.ShapeDtypeStruct((nb_s, H_A, steps * rows, MOBA_BLOCK), F32),
        compiler_params=_params("parallel", "arbitrary"),
        name="logits_sample",
    )(page_table, qt, *([cache_kt] * PAGES_PER_STEP))


def _select_sample_kernel(lg_ref, o_ref):
    sc = jnp.sum(lg_ref[...], axis=-1)
    nblk = sc.shape[-1]
    blk = lax.broadcasted_iota(jnp.int32, sc.shape, 2)
    for t in range(MOBA_TOPK):
        mx = jnp.max(sc, axis=2, keepdims=True)
        idx = jnp.min(jnp.where(sc == mx, blk, nblk), axis=2, keepdims=True)
        o_ref[:, :, t:t + 1] = idx
        sc = jnp.where(blk == idx, -jnp.inf, sc)


def _select_sample(logits):
    nb_s = logits.shape[0]
    return pl.pallas_call(
        _select_sample_kernel,
        out_shape=jax.ShapeDtypeStruct((nb_s, H_A, MOBA_TOPK), jnp.int32),
        compiler_params=pltpu.CompilerParams(vmem_limit_bytes=VMEM_LIMIT),
        name="select_sample",
    )(logits)


def _bias_sample_kernel(rel_ref, o_ref, *, past_len):
    h = pl.program_id(0)
    nblk = o_ref.shape[1]
    kpos = (lax.broadcasted_iota(jnp.int32, (nblk, MOBA_BLOCK), 0) * MOBA_BLOCK
            + lax.broadcasted_iota(jnp.int32, (nblk, MOBA_BLOCK), 1))
    o_ref[0] = _t5_bias(past_len - kpos, rel_ref, h)


def _bias_sample(rel_bias, past_len):
    nblk = past_len // MOBA_BLOCK
    return pl.pallas_call(
        functools.partial(_bias_sample_kernel, past_len=past_len),
        grid=(H_A,),
        in_specs=[pl.BlockSpec(memory_space=pltpu.SMEM)],
        out_specs=pl.BlockSpec((1, nblk, MOBA_BLOCK), lambda h: (h, 0, 0)),
        out_shape=jax.ShapeDtypeStruct((H_A, nblk, MOBA_BLOCK), F32),
        compiler_params=_params("parallel"),
        name="bias_sample",
    )(rel_bias)


def _attend_sample_kernel(pt_ref, sel_ref, rel_ref, lg_ref, bias_ref, qkv_ref, *refs):
    o_ref = refs[-1]
    v_refs = refs[:-1]
    per_block = MOBA_BLOCK // PAGE_SIZE
    b = pl.program_id(0)
    h = pl.program_id(1)

    q = qkv_ref[0, 0, pl.ds(h, 1), :] * Q_SCALE
    k_new = qkv_ref[1, 0, pl.ds(h, 1), :]
    v_new = qkv_ref[2, 0, pl.ds(h, 1), :]
    own = jnp.sum(k_new * q, axis=1, keepdims=True) + rel_ref[0, h]
    logits = []
    mx = own
    for t in range(MOBA_TOPK):
        blk = sel_ref[b, h, t]
        lg = lg_ref[0, 0, pl.ds(blk, 1), :] + bias_ref[0, pl.ds(blk, 1), :]
        logits.append(lg)
        mx = jnp.maximum(mx, jnp.max(lg, axis=1, keepdims=True))
    p_own = jnp.exp(own - mx)
    den = p_own
    acc = p_own * v_new
    for t in range(MOBA_TOPK):
        p = jnp.exp(logits[t] - mx)
        den = den + jnp.sum(p, axis=1, keepdims=True)
        pb = p.astype(BF16)
        for i in range(per_block):
            vt = v_refs[t * per_block + i][0, 0, 0].astype(BF16)
            acc = acc + _dot_nt(pb[:, i * PAGE_SIZE:(i + 1) * PAGE_SIZE], vt)
    o_ref[0, pl.ds(h, 1), :] = acc / den


def _attend_sample(cache_vt, page_table, sel, rel_bias, bias_s, logits, qkv, layer):
    nb_s = page_table.shape[0]
    per_block = MOBA_BLOCK // PAGE_SIZE
    n_sel_pages = MOBA_TOPK * per_block
    nblk = logits.shape[2]

    def page_spec(t):
        def imap(b, h, pt, sl, rel):
            page = sl[b, h, t // per_block] * per_block + (t % per_block)
            return (layer, pt[b, page], h, 0, 0)
        return pl.BlockSpec((1, 1, 1, DH_A, PAGE_SIZE), imap)

    return pl.pallas_call(
        _attend_sample_kernel,
        grid_spec=pltpu.PrefetchScalarGridSpec(
            num_scalar_prefetch=3,
            grid=(nb_s, H_A),
            in_specs=[pl.BlockSpec((1, 1, nblk, MOBA_BLOCK), lambda b, h, pt, sl, rel: (b, h, 0, 0)),
                      pl.BlockSpec((1, nblk, MOBA_BLOCK), lambda b, h, pt, sl, rel: (h, 0, 0)),
                      pl.BlockSpec((3, 1, H_A, DH_A), lambda b, h, pt, sl, rel: (0, b, 0, 0))]
                     + [page_spec(t) for t in range(n_sel_pages)],
            out_specs=pl.BlockSpec((1, H_A, DH_A), lambda b, h, pt, sl, rel: (b, 0, 0))),
        out_shape=jax.ShapeDtypeStruct((nb_s, H_A, DH_A), F32),
        compiler_params=_params("parallel", "arbitrary"),
        name="attend_sample",
    )(page_table, sel, rel_bias, logits, bias_s, qkv, *([cache_vt] * n_sel_pages))


def _mix_kernel(x_ref, hm_ref, ha_ref, sg_ref, wbm_ref, wba_ref, wo_ref, g_ref, o_ref):
    sg = sg_ref[...].astype(F32)
    mixed = (sg[:, :D_MODEL] * _dot(hm_ref[...], wbm_ref[...])
             + sg[:, D_MODEL:] * _dot(ha_ref[...], wba_ref[...]))
    y = _dot(mixed.astype(BF16), wo_ref[...])
    o_ref[...] = x_ref[...] + _rms(y, g_ref[...])


def _mix(x, hm, ha, sg, wbm, wba, wo, g):
    m = x.shape[0]
    tm = min(TM_PROJ, m)
    row = lambda w: pl.BlockSpec((tm, w), lambda i: (i, 0))
    full = lambda a: pl.BlockSpec(a.shape, lambda i: (0,) * a.ndim)
    return pl.pallas_call(
        _mix_kernel,
        grid=(m // tm,),
        in_specs=[row(D_MODEL), row(MW_M), row(MW_A), row(2 * D_MODEL),
                  full(wbm), full(wba), full(wo), full(g)],
        out_specs=row(D_MODEL),
        out_shape=jax.ShapeDtypeStruct((m, D_MODEL), F32),
        compiler_params=_params("parallel"),
        name="mix",
    )(x, hm, ha, sg, wbm, wba, wo, g)


def _ffn_kernel(x_ref, gpre_ref, gpost_ref, wg_ref, wu_ref, wd_ref, o_ref):
    x = x_ref[...]
    hf = _rms(x, gpre_ref[...]).astype(BF16)
    acc = jnp.zeros(x.shape, F32)
    for c in range(D_FF // FF_CHUNK):
        sl = slice(c * FF_CHUNK, (c + 1) * FF_CHUNK)
        gate = _dot(hf, wg_ref[:, sl])
        up = _dot(hf, wu_ref[:, sl])
        mid = (gate * jax.nn.sigmoid(gate) * up).astype(BF16)
        acc = acc + _dot(mid, wd_ref[sl, :])
    o_ref[...] = x + _rms(acc, gpost_ref[...])


def _ffn(x, gpre, gpost, wg, wu, wd):
    m = x.shape[0]
    tm = min(TM_FFN, m)
    row = pl.BlockSpec((tm, D_MODEL), lambda i: (i, 0))
    full = lambda a: pl.BlockSpec(a.shape, lambda i: (0,) * a.ndim)
    return pl.pallas_call(
        _ffn_kernel,
        grid=(m // tm,),
        in_specs=[row, full(gpre), full(gpost), full(wg), full(wu), full(wd)],
        out_specs=row,
        out_shape=jax.ShapeDtypeStruct((m, D_MODEL), F32),
        compiler_params=_params("parallel"),
        name="ffn",
    )(x, gpre, gpost, wg, wu, wd)


def kernel(x_prompt, x_sample, cache_k, cache_v, state_C, state_n, state_m, page_table,
           norm_mix_pre, norm_mix_post, norm_ffn_pre, norm_ffn_post, w_in, b_igate, b_fgate,
           g_mlstm, w_branch_m, w_branch_a, w_out, w_ffn_gate, w_ffn_up, w_ffn_down, rel_bias):
    bp, seq, _ = x_prompt.shape
    bs = x_sample.shape[0]
    depth = w_in.shape[0]
    xp = x_prompt.reshape(bp * seq, D_MODEL)
    xs = x_sample.reshape(bs, D_MODEL)
    cache_kt = jnp.transpose(cache_k, (0, 1, 3, 4, 2))
    cache_vt = jnp.transpose(cache_v, (0, 1, 3, 4, 2))
    w_in_t = jnp.swapaxes(w_in, 1, 2)

    tile = (H_A, 1, MOBA_BLOCK, MOBA_BLOCK)
    table = jnp.concatenate([_bias_table(rel_bias), jnp.zeros(tile, F32), jnp.full(tile, NEG, F32)],
                            axis=1)
    far = rel_bias[N_BUCKETS - 1] * LOG2E
    bias_s = _bias_sample(rel_bias, page_table.shape[1] * PAGE_SIZE)

    outs = {k: [] for k in ("kp", "vp", "cp", "np", "mp", "ks", "vs", "cs", "ns", "ms")}
    for l in range(depth):
        row = lambda a: a[l].reshape(1, -1)
        wt = w_in_t[l]
        wm = wt[:COL_M_END].astype(BF16)
        wif = wt[COL_M_END:COL_IF_END]
        wq = wt[COL_IF_END:COL_Q_END].astype(BF16)
        wk = wt[COL_Q_END:COL_K_END].astype(BF16)
        wv = wt[COL_K_END:COL_V_END].astype(BF16)
        wgt = wt[COL_V_END:].astype(BF16)
        b_if = jnp.concatenate([b_igate[l], b_fgate[l]])
        wbm, wba, wo = (w_branch_m[l].astype(BF16), w_branch_a[l].astype(BF16), w_out[l].astype(BF16))
        wfg, wfu, wfd = (w_ffn_gate[l].astype(BF16), w_ffn_up[l].astype(BF16), w_ffn_down[l].astype(BF16))

        mproj, gcol, grow, qt, ka, kt32, vt32, vte, kmean, sg = _inproj_prompt(
            xp, row(norm_mix_pre), wm, wif, wq, wk, wv, wgt, bp)
        hm, cext, mstate = _mlstm_prompt(mproj, gcol, grow, b_if, g_mlstm[l], bp)
        ha = _moba_prompt(qt, ka, vte, kmean, table, far, bp)
        xp = _mix(xp, hm, ha, sg, wbm, wba, wo, row(norm_mix_post))
        xp = _ffn(xp, row(norm_ffn_pre), row(norm_ffn_post), wfg, wfu, wfd)
        outs["kp"].append(kt32)
        outs["vp"].append(vt32)
        outs["cp"].append(cext[..., :DH_M])
        outs["np"].append(cext[..., DH_M])
        outs["mp"].append(mstate[:, :, 0, 0])

        mproj_s, mproj_t, gcol_s, qkv_s, qt_s, sg_s = _inproj_sample(
            xs, row(norm_mix_pre), wm, wif, wq, wk, wv, wgt)
        hm_s, c_s, n_s, m_s = _mlstm_sample(mproj_s, mproj_t, gcol_s, b_if, g_mlstm[l],
                                            state_C[l], state_n[l], state_m[l])
        logits_s = _logits_sample(cache_kt, page_table, qt_s, l)
        sel = _select_sample(logits_s)
        ha_s = _attend_sample(cache_vt, page_table, sel, rel_bias, bias_s, logits_s,
                              qkv_s.reshape(3, bs, H_A, DH_A), l)
        xs = _mix(xs, hm_s.reshape(bs, MW_M), ha_s.reshape(bs, MW_A).astype(BF16), sg_s,
                  wbm, wba, wo, row(norm_mix_post))
        xs = _ffn(xs, row(norm_ffn_pre), row(norm_ffn_post), wfg, wfu, wfd)
        outs["ks"].append(qkv_s[1].reshape(bs, 1, H_A, DH_A))
        outs["vs"].append(qkv_s[2].reshape(bs, 1, H_A, DH_A))
        outs["cs"].append(c_s)
        outs["ns"].append(n_s)
        outs["ms"].append(m_s.reshape(bs, H_M))

    st = lambda k: jnp.stack(outs[k])
    kv_out = lambda k: jnp.transpose(st(k).reshape(depth, bp, H_A, DH_A, seq), (0, 1, 4, 2, 3))
    return (xp.reshape(bp, seq, D_MODEL), xs.reshape(bs, 1, D_MODEL),
            kv_out("kp"), kv_out("vp"), st("cp"), st("np"), st("mp"),
            st("ks"), st("vs"), st("cs"), st("ns"), st("ms"))
```

```python
import functools
import math

import jax
import jax.numpy as jnp
from jax import lax
from jax.experimental import pallas as pl
from jax.experimental.pallas import tpu as pltpu

F32 = jnp.float32
BF16 = jnp.bfloat16
HIGHEST = lax.Precision.HIGHEST

D_MODEL = 1024
H_M, DH_M = 4, 128
MW_M = H_M * DH_M
H_A, DH_A = 8, 64
MW_A = H_A * DH_A
MOBA_BLOCK = 256
MOBA_TOPK = 3
N_BUCKETS = 32
REL_MAX_DIST = 2048
D_FF = 2816
EPS = 1e-6
NEG = -1e30
PAGE_SIZE = 128

N_GATE_COLS = 2 * H_M
COL_M_END = 4 * MW_M
COL_IF_END = COL_M_END + N_GATE_COLS
COL_Q_END = COL_IF_END + MW_A
COL_K_END = COL_Q_END + MW_A
COL_V_END = COL_K_END + MW_A
LOG_K_SCALE = math.log(DH_M ** -0.5)
Q_SCALE = DH_A ** -0.5
LOG2E = math.log2(math.e)

N_NEAR = (REL_MAX_DIST + MOBA_BLOCK - 1) // MOBA_BLOCK + 1
assert (N_NEAR * MOBA_BLOCK - (MOBA_BLOCK - 1)) >= REL_MAX_DIST
TILE_ZERO = N_NEAR
TILE_MASKED = N_NEAR + 1
N_TILES = N_NEAR + 2
MOBA_GROUP = 4

PAIR = 2 * DH_A
K_AUG = 2 * PAIR
SEL_HI, SEL_LO = PAIR, PAIR + 32
V_ROWS = DH_A + 16
MAX_BLOCKS = 32

VMEM_LIMIT = 56 * 1024 * 1024

TM_PROJ = 512
TM_FFN = 256
L_CHUNK = 256
FF_CHUNK = 1408
PAGES_PER_STEP = 16


def _params(*sem):
    return pltpu.CompilerParams(dimension_semantics=sem, vmem_limit_bytes=VMEM_LIMIT)


def _rms(x, g):
    return x * lax.rsqrt(jnp.mean(x * x, axis=-1, keepdims=True) + EPS) * g


def _log_sigmoid(x):
    return jnp.minimum(x, 0.0) - jnp.log(1.0 + jnp.exp(-jnp.abs(x)))


def _dot(a, b, precision=None):
    return jnp.dot(a, b, precision=precision, preferred_element_type=F32)


def _dot_nt(a, b, precision=None):
    return lax.dot_general(a, b, (((1,), (1,)), ((), ())), precision=precision,
                           preferred_element_type=F32)


def _dot_tn(a, b):
    return lax.dot_general(a, b, (((0,), (0,)), ((), ())), preferred_element_type=F32)


def _inproj_prompt_kernel(x_ref, g_ref, wm_ref, wif_ref, wq_ref, wk_ref, wv_ref, wg_ref, *refs):
    (m_ref, gc_ref, gr_ref, qt_ref, ka_ref, kt32_ref, vt32_ref, vte_ref, km_ref, sg_ref) = refs[-10:]
    tm = x_ref.shape[0]
    xn = _rms(x_ref[...], g_ref[...])
    xb = xn.astype(BF16)
    m_ref[...] = _dot_nt(xb, wm_ref[...]).astype(BF16)
    gates_t = _dot_nt(wif_ref[...], xn, HIGHEST)
    gr_ref[...] = gates_t
    gc_ref[...] = gates_t.T
    qt_ref[...] = (_dot_nt(wq_ref[...], xb) * (Q_SCALE * LOG2E)).astype(BF16)

    k = _dot_nt(xb, wk_ref[...])
    kt = k.T
    for d in range(kt32_ref.shape[0]):
        kt32_ref[d, 0] = kt
    km_ref[0] = jnp.mean(k.reshape(tm // MOBA_BLOCK, MOBA_BLOCK, MW_A), axis=1)
    row = lax.broadcasted_iota(jnp.int32, (tm, PAIR), 0)
    lane = lax.broadcasted_iota(jnp.int32, (tm, PAIR), 1)
    blk = (pl.program_id(1) * tm + row) // MOBA_BLOCK
    onehot = jnp.where(jnp.logical_and(lane < 2 * MAX_BLOCKS, lane % MAX_BLOCKS == blk),
                       1.0, 0.0).astype(BF16)
    kb = k.astype(BF16)
    for p in range(H_A // 2):
        ka_ref[:, p * K_AUG:p * K_AUG + PAIR] = kb[:, p * PAIR:(p + 1) * PAIR]
        ka_ref[:, p * K_AUG + PAIR:(p + 1) * K_AUG] = onehot

    vt = _dot_nt(wv_ref[...], xb)
    for d in range(vt32_ref.shape[0]):
        vt32_ref[d, 0] = vt
    ones_rows = jnp.where(lax.broadcasted_iota(jnp.int32, (V_ROWS - DH_A, tm), 0) == 0,
                          1.0, 0.0).astype(BF16)
    for h in range(H_A):
        vte_ref[h * V_ROWS:h * V_ROWS + DH_A, :] = vt[h * DH_A:(h + 1) * DH_A].astype(BF16)
        vte_ref[h * V_ROWS + DH_A:(h + 1) * V_ROWS, :] = ones_rows
    sg_ref[...] = jax.nn.sigmoid(_dot_nt(xb, wg_ref[...])).astype(BF16)


def _inproj_prompt(x, g, wm, wif, wq, wk, wv, wg, batch, layer, depth, kv_prev):
    m = x.shape[0]
    seq = m // batch
    tm = TM_PROJ
    nt = seq // tm
    n_pairs = H_A // 2
    row = lambda w: pl.BlockSpec((tm, w), lambda b, t: (b * nt + t, 0))
    col = lambda h: pl.BlockSpec((h, tm), lambda b, t: (0, b * nt + t))
    full = lambda a: pl.BlockSpec(a.shape, lambda b, t: (0,) * a.ndim)
    kv_spec = (pl.BlockSpec((depth, 1, MW_A, tm), lambda b, t: (0, b, 0, t)) if kv_prev is None
               else pl.BlockSpec((1, 1, MW_A, tm), lambda b, t: (layer, b, 0, t)))
    operands = [x, g, wm, wif, wq, wk, wv, wg]
    in_specs = [row(D_MODEL), full(g), full(wm), full(wif), full(wq), full(wk), full(wv), full(wg)]
    aliases = {}
    if kv_prev is not None:
        aliases = {len(operands): 5, len(operands) + 1: 6}
        operands += list(kv_prev)
        in_specs += [pl.BlockSpec(memory_space=pl.ANY)] * 2
    return pl.pallas_call(
        _inproj_prompt_kernel,
        grid=(batch, nt),
        in_specs=in_specs,
        input_output_aliases=aliases,
        out_specs=[row(COL_M_END), row(N_GATE_COLS), col(N_GATE_COLS), col(MW_A),
                   row(n_pairs * K_AUG), kv_spec, kv_spec,
                   col(H_A * V_ROWS),
                   pl.BlockSpec((1, tm // MOBA_BLOCK, MW_A), lambda b, t: (b * nt + t, 0, 0)),
                   row(2 * D_MODEL)],
        out_shape=[jax.ShapeDtypeStruct((m, COL_M_END), BF16),
                   jax.ShapeDtypeStruct((m, N_GATE_COLS), F32),
                   jax.ShapeDtypeStruct((N_GATE_COLS, m), F32),
                   jax.ShapeDtypeStruct((MW_A, m), BF16),
                   jax.ShapeDtypeStruct((m, n_pairs * K_AUG), BF16),
                   jax.ShapeDtypeStruct((depth, batch, MW_A, seq), F32),
                   jax.ShapeDtypeStruct((depth, batch, MW_A, seq), F32),
                   jax.ShapeDtypeStruct((H_A * V_ROWS, m), BF16),
                   jax.ShapeDtypeStruct((m // tm, tm // MOBA_BLOCK, MW_A), F32),
                   jax.ShapeDtypeStruct((m, 2 * D_MODEL), BF16)],
        compiler_params=_params("parallel", "parallel"),
        name="inproj_prompt",
    )(*operands)


def _inproj_sample_kernel(x_ref, g_ref, wm_ref, wif_ref, wq_ref, wk_ref, wv_ref, wg_ref,
                          m_ref, mt_ref, gc_ref, qkv_ref, qt_ref, sg_ref):
    xn = _rms(x_ref[...], g_ref[...])
    xb = xn.astype(BF16)
    m_ref[...] = _dot_nt(xb, wm_ref[...])
    mt_ref[...] = _dot_nt(wm_ref[:2 * MW_M, :], xb)
    gc_ref[...] = _dot_nt(xn, wif_ref[...], HIGHEST)
    qkv_ref[0] = _dot_nt(xb, wq_ref[...])
    qkv_ref[1] = _dot_nt(xb, wk_ref[...])
    qkv_ref[2] = _dot_nt(xb, wv_ref[...])
    qt_ref[...] = _dot_nt(wq_ref[...], xb)
    sg_ref[...] = jax.nn.sigmoid(_dot_nt(xb, wg_ref[...])).astype(BF16)


def _inproj_sample(x, g, wm, wif, wq, wk, wv, wg):
    m = x.shape[0]
    return pl.pallas_call(
        _inproj_sample_kernel,
        out_shape=[jax.ShapeDtypeStruct((m, COL_M_END), F32),
                   jax.ShapeDtypeStruct((2 * MW_M, m), F32),
                   jax.ShapeDtypeStruct((m, N_GATE_COLS), F32),
                   jax.ShapeDtypeStruct((3, m, MW_A), F32),
                   jax.ShapeDtypeStruct((MW_A, m), F32),
                   jax.ShapeDtypeStruct((m, 2 * D_MODEL), BF16)],
        compiler_params=pltpu.CompilerParams(vmem_limit_bytes=VMEM_LIMIT),
        name="inproj_sample",
    )(x, g, wm, wif, wq, wk, wv, wg)


def _mlstm_prompt_kernel(m_ref, gc_ref, gr_ref, bc_ref, br_ref, gm_ref,
                         h_ref, c_ref, ms_ref):
    L = m_ref.shape[0]

    @pl.when(pl.program_id(1) == 0)
    def _():
        c_ref[...] = jnp.zeros_like(c_ref)
        ms_ref[...] = jnp.zeros_like(ms_ref)

    row = lax.broadcasted_iota(jnp.int32, (L, L), 0)
    col = lax.broadcasted_iota(jnp.int32, (L, L), 1)
    causal = col <= row
    lower = causal.astype(F32)
    upper = (row <= col).astype(F32)

    gcol = gc_ref[...] + br_ref[...]
    grow = gr_ref[...] + bc_ref[...]
    bcol = _dot(lower, _log_sigmoid(gcol), HIGHEST)
    brow = _dot(_log_sigmoid(grow), upper, HIGHEST)

    lane = lax.broadcasted_iota(jnp.int32, (L, DH_M), 1)
    ones_blk = jnp.where(lane == 0, 1.0, 0.0).astype(BF16)

    def operands(h):
        return (m_ref[:, h * DH_M:(h + 1) * DH_M],
                m_ref[:, MW_M + h * DH_M:MW_M + (h + 1) * DH_M],
                m_ref[:, 2 * MW_M + h * DH_M:2 * MW_M + (h + 1) * DH_M])

    qk_all = [_dot_nt(operands(h)[0], operands(h)[1]) for h in range(H_M)]
    qc_all = [_dot(operands(h)[0], c_ref[0, h].astype(BF16)) for h in range(H_M)]

    heads = range(H_M)
    i_col = [gcol[:, h:h + 1] for h in heads]
    b_col = [bcol[:, H_M + h:H_M + h + 1] for h in heads]
    i_row = [grow[h:h + 1, :] for h in heads]
    b_row = [brow[H_M + h:H_M + h + 1, :] for h in heads]
    b_last = [b_col[h][L - 1:L, :] for h in heads]
    m_prev = [ms_ref[0, h][0:1, 0:1] for h in heads]
    v_ext = [jnp.concatenate([operands(h)[2], ones_blk], axis=1) for h in heads]

    mt, a, sv = [], [], []
    for h in heads:
        u_row = i_row[h] - b_row[h]
        u_max = jnp.max(jnp.where(causal, u_row, NEG), axis=1, keepdims=True)
        inter = b_col[h] + m_prev[h]
        mt.append(jnp.maximum(inter, b_col[h] + u_max))
        a.append(jnp.exp(inter - mt[h]))
        expo = jnp.where(causal, (b_col[h] - mt[h] + LOG_K_SCALE) + u_row, NEG)
        s = qk_all[h] * jnp.exp(expo)
        sv.append(_dot(s.astype(BF16), v_ext[h]))

    for h in heads:
        k = operands(h)[1]
        g_row = b_last[h] - b_row[h] + i_row[h]
        e = b_last[h] + m_prev[h]
        m_new = jnp.maximum(e, jnp.max(g_row, axis=1, keepdims=True))
        w_col = jnp.exp(b_last[h] - b_col[h] + i_col[h] - (m_new - LOG_K_SCALE))
        kw = (k.astype(F32) * w_col).astype(BF16)
        c_ref[0, h] = jnp.exp(e - m_new) * c_ref[0, h] + _dot_tn(kw, v_ext[h])
        ms_ref[0, h] = jnp.broadcast_to(m_new, ms_ref.shape[2:])

    for h in heads:
        o = m_ref[:, 3 * MW_M + h * DH_M:3 * MW_M + (h + 1) * DH_M]
        nd = a[h] * qc_all[h] + sv[h]
        num = nd[:, :DH_M]
        denom = jnp.maximum(jnp.abs(nd[:, DH_M:DH_M + 1]), jnp.exp(-mt[h]))
        nc = num - jnp.mean(num, axis=1, keepdims=True)
        var = jnp.mean(nc * nc, axis=1, keepdims=True)
        hn = nc * lax.rsqrt(var + EPS * denom * denom) * gm_ref[:, h * DH_M:(h + 1) * DH_M]
        h_ref[:, h * DH_M:(h + 1) * DH_M] = (hn * jax.nn.sigmoid(o.astype(F32))).astype(BF16)


def _mlstm_prompt(mproj, gcol, grow, b_if, g_mlstm, batch):
    m = mproj.shape[0]
    L = L_CHUNK
    nc = m // batch // L
    bias_row = b_if.reshape(1, N_GATE_COLS)
    bias_col = b_if.reshape(N_GATE_COLS, 1)
    gm = g_mlstm.reshape(1, MW_M)
    return pl.pallas_call(
        _mlstm_prompt_kernel,
        grid=(batch, nc),
        in_specs=[pl.BlockSpec((L, COL_M_END), lambda b, c: (b * nc + c, 0)),
                  pl.BlockSpec((L, N_GATE_COLS), lambda b, c: (b * nc + c, 0)),
                  pl.BlockSpec((N_GATE_COLS, L), lambda b, c: (0, b * nc + c)),
                  pl.BlockSpec((N_GATE_COLS, 1), lambda b, c: (0, 0)),
                  pl.BlockSpec((1, N_GATE_COLS), lambda b, c: (0, 0)),
                  pl.BlockSpec((1, MW_M), lambda b, c: (0, 0))],
        out_specs=[pl.BlockSpec((L, MW_M), lambda b, c: (b * nc + c, 0)),
                   pl.BlockSpec((1, H_M, DH_M, 2 * DH_M), lambda b, c: (b, 0, 0, 0)),
                   pl.BlockSpec((1, H_M, 8, 128), lambda b, c: (b, 0, 0, 0))],
        out_shape=[jax.ShapeDtypeStruct((m, MW_M), BF16),
                   jax.ShapeDtypeStruct((batch, H_M, DH_M, 2 * DH_M), F32),
                   jax.ShapeDtypeStruct((batch, H_M, 8, 128), F32)],
        compiler_params=_params("parallel", "arbitrary"),
        name="mlstm_prompt",
    )(mproj, gcol, grow, bias_col, bias_row, gm)


def _mlstm_sample_kernel(m_ref, mt_ref, gc_ref, bi_ref, gm_ref, c_ref, n_ref, ms_ref,
                         h_ref, co_ref, no_ref, mo_ref):
    b = pl.program_id(0)
    nb = mt_ref.shape[1]
    onehot = lax.broadcasted_iota(jnp.int32, (1, nb), 1) == b
    g = gc_ref[0] + bi_ref[...]
    for h in range(H_M):
        sl = slice(h * DH_M, (h + 1) * DH_M)
        q_row = m_ref[0, :, h * DH_M:(h + 1) * DH_M]
        k_row = m_ref[0, :, MW_M + h * DH_M:MW_M + (h + 1) * DH_M]
        v_row = m_ref[0, :, 2 * MW_M + h * DH_M:2 * MW_M + (h + 1) * DH_M]
        o_row = m_ref[0, :, 3 * MW_M + h * DH_M:3 * MW_M + (h + 1) * DH_M]
        q_col = jnp.sum(jnp.where(onehot, mt_ref[sl, :], 0.0), axis=1, keepdims=True)
        k_col = jnp.sum(jnp.where(onehot, mt_ref[MW_M + h * DH_M:MW_M + (h + 1) * DH_M, :], 0.0),
                        axis=1, keepdims=True)
        i_pre = g[:, h:h + 1]
        log_f = _log_sigmoid(g[:, H_M + h:H_M + h + 1])
        m_prev = ms_ref[0, :, h:h + 1]
        inter = log_f + m_prev
        m_new = jnp.maximum(inter, i_pre)
        a = jnp.exp(inter - m_new)
        w = jnp.exp(i_pre - (m_new - LOG_K_SCALE))
        c = c_ref[0, h]
        n = n_ref[0, h:h + 1, :]
        s = jnp.sum(q_row * k_row, axis=1, keepdims=True) * w
        num = a * jnp.sum(c * q_col, axis=0, keepdims=True) + s * v_row
        den = a * jnp.sum(q_row * n, axis=1, keepdims=True) + s
        hh = num / jnp.maximum(jnp.abs(den), jnp.exp(-m_new))
        mu = jnp.mean(hh, axis=1, keepdims=True)
        hc = hh - mu
        var = jnp.mean(hc * hc, axis=1, keepdims=True)
        hn = hc * lax.rsqrt(var + EPS) * gm_ref[:, sl]
        h_ref[0, :, sl] = (hn * jax.nn.sigmoid(o_row)).astype(BF16)
        co_ref[0, h] = a * c + (w * k_col) * v_row
        no_ref[0, h:h + 1, :] = a * n + w * k_row
        mo_ref[0, :, h:h + 1] = m_new


def _mlstm_sample(mproj, mproj_t, gcol, b_if, g_mlstm, c0, n0, m0):
    nb = mproj.shape[0]
    return pl.pallas_call(
        _mlstm_sample_kernel,
        grid=(nb,),
        in_specs=[pl.BlockSpec((1, 1, COL_M_END), lambda b: (b, 0, 0)),
                  pl.BlockSpec((2 * MW_M, nb), lambda b: (0, 0)),
                  pl.BlockSpec((1, 1, N_GATE_COLS), lambda b: (b, 0, 0)),
                  pl.BlockSpec((1, N_GATE_COLS), lambda b: (0, 0)),
                  pl.BlockSpec((1, MW_M), lambda b: (0, 0)),
                  pl.BlockSpec((1, H_M, DH_M, DH_M), lambda b: (b, 0, 0, 0)),
                  pl.BlockSpec((1, H_M, DH_M), lambda b: (b, 0, 0)),
                  pl.BlockSpec((1, 1, H_M), lambda b: (b, 0, 0))],
        out_specs=[pl.BlockSpec((1, 1, MW_M), lambda b: (b, 0, 0)),
                   pl.BlockSpec((1, H_M, DH_M, DH_M), lambda b: (b, 0, 0, 0)),
                   pl.BlockSpec((1, H_M, DH_M), lambda b: (b, 0, 0)),
                   pl.BlockSpec((1, 1, H_M), lambda b: (b, 0, 0))],
        out_shape=[jax.ShapeDtypeStruct((nb, 1, MW_M), BF16),
                   jax.ShapeDtypeStruct((nb, H_M, DH_M, DH_M), F32),
                   jax.ShapeDtypeStruct((nb, H_M, DH_M), F32),
                   jax.ShapeDtypeStruct((nb, 1, H_M), F32)],
        compiler_params=_params("parallel"),
        name="mlstm_sample",
    )(mproj.reshape(nb, 1, COL_M_END), mproj_t, gcol.reshape(nb, 1, N_GATE_COLS),
      b_if.reshape(1, N_GATE_COLS), g_mlstm.reshape(1, MW_M), c0, n0, m0.reshape(nb, 1, H_M))


def _t5_bucket(dist):
    n = jnp.maximum(dist, 0)
    max_exact = N_BUCKETS // 2
    nf = jnp.maximum(n, 1).astype(F32)
    large = max_exact + (jnp.log(nf / max_exact) / math.log(REL_MAX_DIST / max_exact)
                         * (N_BUCKETS - max_exact)).astype(jnp.int32)
    large = jnp.minimum(large, N_BUCKETS - 1)
    return jnp.where(n < max_exact, n, large)


def _t5_bias(dist, rel_ref, h):
    bucket = _t5_bucket(dist)
    bias = jnp.zeros(dist.shape, F32)
    for kb in range(N_BUCKETS):
        bias = jnp.where(bucket == kb, rel_ref[kb, h], bias)
    return bias


def _bias_table_kernel(rel_ref, o_ref):
    delta = pl.program_id(0)
    c = lax.broadcasted_iota(jnp.int32, (MOBA_BLOCK, MOBA_BLOCK), 0)
    r = lax.broadcasted_iota(jnp.int32, (MOBA_BLOCK, MOBA_BLOCK), 1)
    dist = delta * MOBA_BLOCK + r - c
    bucket = _t5_bucket(dist)
    for h in range(H_A):
        bias = jnp.zeros(dist.shape, F32)
        for kb in range(N_BUCKETS):
            bias = jnp.where(bucket == kb, rel_ref[kb, h], bias)
        o_ref[h, 0] = jnp.where(dist >= 0, bias * LOG2E, NEG)


def _bias_table(rel_bias):
    return pl.pallas_call(
        _bias_table_kernel,
        grid=(N_NEAR,),
        in_specs=[pl.BlockSpec(memory_space=pltpu.SMEM)],
        out_specs=pl.BlockSpec((H_A, 1, MOBA_BLOCK, MOBA_BLOCK), lambda d: (0, d, 0, 0)),
        out_shape=jax.ShapeDtypeStruct((H_A, N_NEAR, MOBA_BLOCK, MOBA_BLOCK), F32),
        compiler_params=_params("parallel"),
        name="bias_table",
    )(rel_bias)


def _select_topk_t(score, n_valid_rows, k):
    nb = score.shape[0]
    row = lax.broadcasted_iota(jnp.int32, score.shape, 0)
    past = row < n_valid_rows
    sc = jnp.where(past, score, -jnp.inf)
    sel = jnp.full(score.shape, NEG, F32)
    for _ in range(k):
        mx = jnp.max(sc, axis=0, keepdims=True)
        idx = jnp.min(jnp.where(sc == mx, row, nb), axis=0, keepdims=True)
        hit = row == idx
        sel = jnp.where(hit, 0.0, sel)
        sc = jnp.where(hit, -jnp.inf, sc)
    return jnp.where(past, sel, NEG)


def _moba_prompt_kernel(qi_ref, grp_ref, first_ref, last_ref, far_item_ref, far_ref,
                        qt_ref, ka_ref, vte_ref, km_ref, tab_ref, o_ref,
                        qp_scr, m_scr, acc_scr, sa_scr, sb_scr, gma_scr, gmb_scr):
    hp = pl.program_id(1)
    seq = qt_ref.shape[1]
    nb = km_ref.shape[1]
    tq = MOBA_BLOCK
    n_items = qi_ref.shape[0] - 1
    setup_w = 4 * MOBA_BLOCK

    def setup(c, carry):
        cols = pl.ds(pl.multiple_of(c * setup_w, setup_w), setup_w)
        blk_row = lax.broadcasted_iota(jnp.int32, (nb, setup_w), 0)
        q_blk = (c * setup_w + lax.broadcasted_iota(jnp.int32, (1, setup_w), 1)) // MOBA_BLOCK
        sub = lax.broadcasted_iota(jnp.int32, (PAIR, setup_w), 0)
        qpair = qt_ref[:, cols]
        for h in range(2):
            qth = qt_ref[h * DH_A:(h + 1) * DH_A, cols].astype(F32)
            score = _dot(km_ref[0, :, h * DH_A:(h + 1) * DH_A], qth, HIGHEST)
            sel = _select_topk_t(score, q_blk, MOBA_TOPK)
            sel = jnp.where(blk_row == q_blk, 0.0, sel)
            sel = sel + jnp.where(q_blk - blk_row >= N_NEAR, far_ref[2 * hp + h], 0.0)
            hi = sel.astype(BF16)
            qp_scr[h, 0:PAIR, cols] = jnp.where(sub // DH_A == h, qpair, jnp.zeros_like(qpair))
            qp_scr[h, PAIR:, cols] = jnp.zeros((K_AUG - PAIR, setup_w), BF16)
            qp_scr[h, SEL_HI:SEL_HI + nb, cols] = hi
            qp_scr[h, SEL_LO:SEL_LO + nb, cols] = (sel - hi.astype(F32)).astype(BF16)
        return carry

    lax.fori_loop(0, seq // setup_w, setup, 0)
    m_scr[...] = jnp.full(m_scr.shape, NEG, F32)
    acc_scr[...] = jnp.zeros(acc_scr.shape, F32)

    gk = MOBA_GROUP * MOBA_BLOCK

    def score_block(w, t, h, far):
        qi = qi_ref[w]
        g = grp_ref[w]
        kblk = ka_ref[pl.ds(pl.multiple_of((g * MOBA_GROUP + t) * MOBA_BLOCK, MOBA_BLOCK),
                            MOBA_BLOCK), :]
        qcols = pl.ds(pl.multiple_of(qi * tq, tq), tq)
        s = _dot(kblk, qp_scr[h, :, qcols])
        if far:
            return s
        delta = qi - (g * MOBA_GROUP + t)
        tile = jnp.where(delta < 0, TILE_MASKED, jnp.minimum(delta, TILE_ZERO))
        return s + tab_ref[h, tile]

    def step(w, s_cur, gm_cur, s_nxt, gm_nxt, far):
        qi = qi_ref[w]
        start = pl.multiple_of(grp_ref[w] * gk, gk)
        first = first_ref[w] == 1
        m_old = [jnp.where(first, NEG, m_scr[h]) for h in range(2)]
        m_new = [jnp.maximum(m_old[h], gm_cur[h]) for h in range(2)]
        pv = [None, None]
        gmax = [None, None]

        def next_scores(t):
            rows = slice(t * MOBA_BLOCK, (t + 1) * MOBA_BLOCK)
            for h in range(2):
                s = score_block(w + 1, t, h, far)
                s_nxt[h, rows, :] = s
                cmax = jnp.max(s, axis=0, keepdims=True)
                gmax[h] = cmax if t == 0 else jnp.maximum(gmax[h], cmax)

        def attend(t):
            rows = slice(t * MOBA_BLOCK, (t + 1) * MOBA_BLOCK)
            for h in range(2):
                p = jnp.exp2(s_cur[h, rows, :] - m_new[h]).astype(BF16)
                vblk = vte_ref[h * V_ROWS:(h + 1) * V_ROWS,
                               pl.ds(start + t * MOBA_BLOCK, MOBA_BLOCK)]
                d = _dot(vblk, p)
                pv[h] = d if t == 0 else pv[h] + d

        for t in range(MOBA_GROUP):
            next_scores(t)
            attend(t)
        for h in range(2):
            acc_old = jnp.where(first, 0.0, acc_scr[h])
            acc_scr[h] = jnp.exp2(m_old[h] - m_new[h]) * acc_old + pv[h]
            m_scr[h] = m_new[h]
            gm_nxt[h] = gmax[h]

        @pl.when(last_ref[w] == 1)
        def _():
            outs = [acc_scr[h, 0:DH_A, :] / acc_scr[h, DH_A:DH_A + 1, :] for h in range(2)]
            orows = pl.ds(pl.multiple_of(qi * tq, tq), tq)
            o_ref[orows, :] = jnp.concatenate(outs, axis=0).T.astype(BF16)

    for h in range(2):
        gmax = None
        for t in range(MOBA_GROUP):
            s = score_block(0, t, h, False)
            sa_scr[h, t * MOBA_BLOCK:(t + 1) * MOBA_BLOCK, :] = s
            cmax = jnp.max(s, axis=0, keepdims=True)
            gmax = cmax if t == 0 else jnp.maximum(gmax, cmax)
        gma_scr[h] = gmax

    def step_by_kind(w, *buffers):
        next_is_far = far_item_ref[w + 1] == 1
        pl.when(next_is_far)(lambda: step(w, *buffers, True))
        pl.when(jnp.logical_not(next_is_far))(lambda: step(w, *buffers, False))

    def pair_of_items(i, carry):
        w = 2 * i
        step_by_kind(w, sa_scr, gma_scr, sb_scr, gmb_scr)
        step_by_kind(w + 1, sb_scr, gmb_scr, sa_scr, gma_scr)
        return carry

    lax.fori_loop(0, n_items // 2, pair_of_items, 0)


def _moba_work_items(nb):
    qi, grp, first, last, far = [], [], [], [], []
    for q in range(nb):
        n_groups = q // MOBA_GROUP + 1
        for g in range(n_groups):
            qi.append(q)
            grp.append(g)
            first.append(int(g == 0))
            last.append(int(g == n_groups - 1))
            far.append(int(q - (g * MOBA_GROUP + MOBA_GROUP - 1) >= N_NEAR))
    assert len(qi) % 2 == 0
    tables = [qi + qi[-1:], grp + grp[-1:], first + [0], last + [0], far + [0]]
    return [jnp.asarray(t, jnp.int32) for t in tables]


def _moba_prompt(qt, ka, vte, kmean, table, far, batch):
    m = qt.shape[1]
    seq = m // batch
    nb = seq // MOBA_BLOCK
    assert nb <= MAX_BLOCKS and nb % MOBA_GROUP == 0
    tq = MOBA_BLOCK
    n_pairs = H_A // 2
    gk = MOBA_GROUP * MOBA_BLOCK
    smem = pl.BlockSpec(memory_space=pltpu.SMEM)
    return pl.pallas_call(
        _moba_prompt_kernel,
        grid=(batch, n_pairs),
        in_specs=[smem, smem, smem, smem, smem, smem,
                  pl.BlockSpec((PAIR, seq), lambda b, hp: (hp, b)),
                  pl.BlockSpec((seq, K_AUG), lambda b, hp: (b, hp)),
                  pl.BlockSpec((2 * V_ROWS, seq), lambda b, hp: (hp, b)),
                  pl.BlockSpec((1, nb, PAIR), lambda b, hp: (b, 0, hp)),
                  pl.BlockSpec((2, N_TILES, tq, tq), lambda b, hp: (hp, 0, 0, 0),
                               pipeline_mode=pl.Buffered(1))],
        out_specs=pl.BlockSpec((seq, PAIR), lambda b, hp: (b, hp)),
        out_shape=jax.ShapeDtypeStruct((m, MW_A), BF16),
        scratch_shapes=[pltpu.VMEM((2, K_AUG, seq), BF16),
                        pltpu.VMEM((2, 1, tq), F32),
                        pltpu.VMEM((2, V_ROWS, tq), F32),
                        pltpu.VMEM((2, gk, tq), F32),
                        pltpu.VMEM((2, gk, tq), F32),
                        pltpu.VMEM((2, 1, tq), F32),
                        pltpu.VMEM((2, 1, tq), F32)],
        compiler_params=_params("parallel", "arbitrary"),
        name="moba_prompt",
    )(*_moba_work_items(nb), far, qt, ka, vte, kmean.reshape(batch, nb, MW_A), table)


def _logits_sample_kernel(pt_ref, qt_ref, *refs):
    o_ref, qb_scr = refs[-2], refs[-1]
    b = pl.program_id(0)
    nb_s = qt_ref.shape[1]

    @pl.when(pl.program_id(1) == 0)
    def _():
        onehot = lax.broadcasted_iota(jnp.int32, (1, nb_s), 1) == b
        qcol = jnp.sum(jnp.where(onehot, qt_ref[...], 0.0), axis=1, keepdims=True)
        qb_scr[...] = jnp.broadcast_to(qcol * Q_SCALE, qb_scr.shape)

    per_block = MOBA_BLOCK // PAGE_SIZE
    for i in range(PAGES_PER_STEP):
        for h in range(H_A):
            kt = refs[i][0, 0, h]
            lg = jnp.sum(kt * qb_scr[h * DH_A:(h + 1) * DH_A, :], axis=0, keepdims=True)
            r, half = i // per_block, i % per_block
            o_ref[0, h, r:r + 1, half * PAGE_SIZE:(half + 1) * PAGE_SIZE] = lg


def _logits_sample(cache_kt, page_table, qt, layer):
    nb_s, n_pages = page_table.shape
    steps = n_pages // PAGES_PER_STEP
    rows = PAGES_PER_STEP * PAGE_SIZE // MOBA_BLOCK

    def page_spec(i):
        return pl.BlockSpec((1, 1, H_A, DH_A, PAGE_SIZE),
                            lambda b, s, pt: (layer, pt[b, s * PAGES_PER_STEP + i], 0, 0, 0))

    return pl.pallas_call(
        _logits_sample_kernel,
        grid_spec=pltpu.PrefetchScalarGridSpec(
            num_scalar_prefetch=1,
            grid=(nb_s, steps),
            in_specs=[pl.BlockSpec(qt.shape, lambda b, s, pt: (0, 0))]
                     + [page_spec(i) for i in range(PAGES_PER_STEP)],
            out_specs=pl.BlockSpec((1, H_A, rows, MOBA_BLOCK), lambda b, s, pt: (b, 0, s, 0)),
            scratch_shapes=[pltpu.VMEM((MW_A, PAGE_SIZE), F32)]),
        out_shape=jax.ShapeDtypeStruct((nb_s, H_A, steps * rows, MOBA_BLOCK), F32),
        compiler_params=_params("parallel", "arbitrary"),
        name="logits_sample",
    )(page_table, qt, *([cache_kt] * PAGES_PER_STEP))


def _select_sample_kernel(lg_ref, o_ref):
    sc = jnp.sum(lg_ref[...], axis=-1)
    nblk = sc.shape[-1]
    blk = lax.broadcasted_iota(jnp.int32, sc.shape, 2)
    for t in range(MOBA_TOPK):
        mx = jnp.max(sc, axis=2, keepdims=True)
        idx = jnp.min(jnp.where(sc == mx, blk, nblk), axis=2, keepdims=True)
        o_ref[:, :, t:t + 1] = idx
        sc = jnp.where(blk == idx, -jnp.inf, sc)


def _select_sample(logits):
    nb_s = logits.shape[0]
    return pl.pallas_call(
        _select_sample_kernel,
        out_shape=jax.ShapeDtypeStruct((nb_s, H_A, MOBA_TOPK), jnp.int32),
        compiler_params=pltpu.CompilerParams(vmem_limit_bytes=VMEM_LIMIT),
        name="select_sample",
    )(logits)


def _bias_sample_kernel(rel_ref, o_ref, *, past_len):
    h = pl.program_id(0)
    nblk = o_ref.shape[1]
    kpos = (lax.broadcasted_iota(jnp.int32, (nblk, MOBA_BLOCK), 0) * MOBA_BLOCK
            + lax.broadcasted_iota(jnp.int32, (nblk, MOBA_BLOCK), 1))
    o_ref[0] = _t5_bias(past_len - kpos, rel_ref, h)


def _bias_sample(rel_bias, past_len):
    nblk = past_len // MOBA_BLOCK
    return pl.pallas_call(
        functools.partial(_bias_sample_kernel, past_len=past_len),
        grid=(H_A,),
        in_specs=[pl.BlockSpec(memory_space=pltpu.SMEM)],
        out_specs=pl.BlockSpec((1, nblk, MOBA_BLOCK), lambda h: (h, 0, 0)),
        out_shape=jax.ShapeDtypeStruct((H_A, nblk, MOBA_BLOCK), F32),
        compiler_params=_params("parallel"),
        name="bias_sample",
    )(rel_bias)


def _attend_sample_kernel(pt_ref, sel_ref, rel_ref, lg_ref, bias_ref, qkv_ref, *refs):
    o_ref = refs[-1]
    v_refs = refs[:-1]
    per_block = MOBA_BLOCK // PAGE_SIZE
    b = pl.program_id(0)
    h = pl.program_id(1)

    q = qkv_ref[0, 0, pl.ds(h, 1), :] * Q_SCALE
    k_new = qkv_ref[1, 0, pl.ds(h, 1), :]
    v_new = qkv_ref[2, 0, pl.ds(h, 1), :]
    own = jnp.sum(k_new * q, axis=1, keepdims=True) + rel_ref[0, h]
    logits = []
    mx = own
    for t in range(MOBA_TOPK):
        blk = sel_ref[b, h, t]
        lg = lg_ref[0, 0, pl.ds(blk, 1), :] + bias_ref[0, pl.ds(blk, 1), :]
        logits.append(lg)
        mx = jnp.maximum(mx, jnp.max(lg, axis=1, keepdims=True))
    p_own = jnp.exp(own - mx)
    den = p_own
    acc = p_own * v_new
    for t in range(MOBA_TOPK):
        p = jnp.exp(logits[t] - mx)
        den = den + jnp.sum(p, axis=1, keepdims=True)
        pb = p.astype(BF16)
        for i in range(per_block):
            vt = v_refs[t * per_block + i][0, 0, 0].astype(BF16)
            acc = acc + _dot_nt(pb[:, i * PAGE_SIZE:(i + 1) * PAGE_SIZE], vt)
    o_ref[0, pl.ds(h, 1), :] = acc / den


def _attend_sample(cache_vt, page_table, sel, rel_bias, bias_s, logits, qkv, layer):
    nb_s = page_table.shape[0]
    per_block = MOBA_BLOCK // PAGE_SIZE
    n_sel_pages = MOBA_TOPK * per_block
    nblk = logits.shape[2]

    def page_spec(t):
        def imap(b, h, pt, sl, rel):
            page = sl[b, h, t // per_block] * per_block + (t % per_block)
            return (layer, pt[b, page], h, 0, 0)
        return pl.BlockSpec((1, 1, 1, DH_A, PAGE_SIZE), imap)

    return pl.pallas_call(
        _attend_sample_kernel,
        grid_spec=pltpu.PrefetchScalarGridSpec(
            num_scalar_prefetch=3,
            grid=(nb_s, H_A),
            in_specs=[pl.BlockSpec((1, 1, nblk, MOBA_BLOCK), lambda b, h, pt, sl, rel: (b, h, 0, 0)),
                      pl.BlockSpec((1, nblk, MOBA_BLOCK), lambda b, h, pt, sl, rel: (h, 0, 0)),
                      pl.BlockSpec((3, 1, H_A, DH_A), lambda b, h, pt, sl, rel: (0, b, 0, 0))]
                     + [page_spec(t) for t in range(n_sel_pages)],
            out_specs=pl.BlockSpec((1, H_A, DH_A), lambda b, h, pt, sl, rel: (b, 0, 0))),
        out_shape=jax.ShapeDtypeStruct((nb_s, H_A, DH_A), F32),
        compiler_params=_params("parallel", "arbitrary"),
        name="attend_sample",
    )(page_table, sel, rel_bias, logits, bias_s, qkv, *([cache_vt] * n_sel_pages))


def _mix_kernel(x_ref, hm_ref, ha_ref, sg_ref, wbm_ref, wba_ref, wo_ref, g_ref, o_ref):
    sg = sg_ref[...].astype(F32)
    mixed = (sg[:, :D_MODEL] * _dot(hm_ref[...], wbm_ref[...])
             + sg[:, D_MODEL:] * _dot(ha_ref[...], wba_ref[...]))
    y = _dot(mixed.astype(BF16), wo_ref[...])
    o_ref[...] = x_ref[...] + _rms(y, g_ref[...])


def _mix(x, hm, ha, sg, wbm, wba, wo, g):
    m = x.shape[0]
    tm = min(TM_PROJ, m)
    row = lambda w: pl.BlockSpec((tm, w), lambda i: (i, 0))
    full = lambda a: pl.BlockSpec(a.shape, lambda i: (0,) * a.ndim)
    return pl.pallas_call(
        _mix_kernel,
        grid=(m // tm,),
        in_specs=[row(D_MODEL), row(MW_M), row(MW_A), row(2 * D_MODEL),
                  full(wbm), full(wba), full(wo), full(g)],
        out_specs=row(D_MODEL),
        out_shape=jax.ShapeDtypeStruct((m, D_MODEL), F32),
        compiler_params=_params("parallel"),
        name="mix",
    )(x, hm, ha, sg, wbm, wba, wo, g)


def _ffn_kernel(x_ref, gpre_ref, gpost_ref, wg_ref, wu_ref, wd_ref, o_ref):
    x = x_ref[...]
    hf = _rms(x, gpre_ref[...]).astype(BF16)
    acc = jnp.zeros(x.shape, F32)
    for c in range(D_FF // FF_CHUNK):
        sl = slice(c * FF_CHUNK, (c + 1) * FF_CHUNK)
        gate = _dot(hf, wg_ref[:, sl])
        up = _dot(hf, wu_ref[:, sl])
        mid = (gate * jax.nn.sigmoid(gate) * up).astype(BF16)
        acc = acc + _dot(mid, wd_ref[sl, :])
    o_ref[...] = x + _rms(acc, gpost_ref[...])


def _ffn(x, gpre, gpost, wg, wu, wd):
    m = x.shape[0]
    tm = min(TM_FFN, m)
    row = pl.BlockSpec((tm, D_MODEL), lambda i: (i, 0))
    full = lambda a: pl.BlockSpec(a.shape, lambda i: (0,) * a.ndim)
    return pl.pallas_call(
        _ffn_kernel,
        grid=(m // tm,),
        in_specs=[row, full(gpre), full(gpost), full(wg), full(wu), full(wd)],
        out_specs=row,
        out_shape=jax.ShapeDtypeStruct((m, D_MODEL), F32),
        compiler_params=_params("parallel"),
        name="ffn",
    )(x, gpre, gpost, wg, wu, wd)


def kernel(x_prompt, x_sample, cache_k, cache_v, state_C, state_n, state_m, page_table,
           norm_mix_pre, norm_mix_post, norm_ffn_pre, norm_ffn_post, w_in, b_igate, b_fgate,
           g_mlstm, w_branch_m, w_branch_a, w_out, w_ffn_gate, w_ffn_up, w_ffn_down, rel_bias):
    bp, seq, _ = x_prompt.shape
    bs = x_sample.shape[0]
    depth = w_in.shape[0]
    xp = x_prompt.reshape(bp * seq, D_MODEL)
    xs = x_sample.reshape(bs, D_MODEL)
    cache_kt = jnp.transpose(cache_k, (0, 1, 3, 4, 2))
    cache_vt = jnp.transpose(cache_v, (0, 1, 3, 4, 2))
    w_in_t = jnp.swapaxes(w_in, 1, 2)

    tile = (H_A, 1, MOBA_BLOCK, MOBA_BLOCK)
    table = jnp.concatenate([_bias_table(rel_bias), jnp.zeros(tile, F32), jnp.full(tile, NEG, F32)],
                            axis=1)
    far = rel_bias[N_BUCKETS - 1] * LOG2E
    bias_s = _bias_sample(rel_bias, page_table.shape[1] * PAGE_SIZE)

    outs = {k: [] for k in ("cp", "np", "mp", "ks", "vs", "cs", "ns", "ms")}
    kv_prompt = None
    for l in range(depth):
        row = lambda a: a[l].reshape(1, -1)
        wt = w_in_t[l]
        wm = wt[:COL_M_END].astype(BF16)
        wif = wt[COL_M_END:COL_IF_END]
        wq = wt[COL_IF_END:COL_Q_END].astype(BF16)
        wk = wt[COL_Q_END:COL_K_END].astype(BF16)
        wv = wt[COL_K_END:COL_V_END].astype(BF16)
        wgt = wt[COL_V_END:].astype(BF16)
        b_if = jnp.concatenate([b_igate[l], b_fgate[l]])
        wbm, wba, wo = (w_branch_m[l].astype(BF16), w_branch_a[l].astype(BF16), w_out[l].astype(BF16))
        wfg, wfu, wfd = (w_ffn_gate[l].astype(BF16), w_ffn_up[l].astype(BF16), w_ffn_down[l].astype(BF16))

        mproj, gcol, grow, qt, ka, kt32, vt32, vte, kmean, sg = _inproj_prompt(
            xp, row(norm_mix_pre), wm, wif, wq, wk, wv, wgt, bp, l, depth, kv_prompt)
        kv_prompt = (kt32, vt32)
        hm, cext, mstate = _mlstm_prompt(mproj, gcol, grow, b_if, g_mlstm[l], bp)
        ha = _moba_prompt(qt, ka, vte, kmean, table, far, bp)
        xp = _mix(xp, hm, ha, sg, wbm, wba, wo, row(norm_mix_post))
        xp = _ffn(xp, row(norm_ffn_pre), row(norm_ffn_post), wfg, wfu, wfd)
        outs["cp"].append(cext[..., :DH_M])
        outs["np"].append(cext[..., DH_M])
        outs["mp"].append(mstate[:, :, 0, 0])

        mproj_s, mproj_t, gcol_s, qkv_s, qt_s, sg_s = _inproj_sample(
            xs, row(norm_mix_pre), wm, wif, wq, wk, wv, wgt)
        hm_s, c_s, n_s, m_s = _mlstm_sample(mproj_s, mproj_t, gcol_s, b_if, g_mlstm[l],
                                            state_C[l], state_n[l], state_m[l])
        logits_s = _logits_sample(cache_kt, page_table, qt_s, l)
        sel = _select_sample(logits_s)
        ha_s = _attend_sample(cache_vt, page_table, sel, rel_bias, bias_s, logits_s,
                              qkv_s.reshape(3, bs, H_A, DH_A), l)
        xs = _mix(xs, hm_s.reshape(bs, MW_M), ha_s.reshape(bs, MW_A).astype(BF16), sg_s,
                  wbm, wba, wo, row(norm_mix_post))
        xs = _ffn(xs, row(norm_ffn_pre), row(norm_ffn_post), wfg, wfu, wfd)
        outs["ks"].append(qkv_s[1].reshape(bs, 1, H_A, DH_A))
        outs["vs"].append(qkv_s[2].reshape(bs, 1, H_A, DH_A))
        outs["cs"].append(c_s)
        outs["ns"].append(n_s)
        outs["ms"].append(m_s.reshape(bs, H_M))

    st = lambda k: jnp.stack(outs[k])
    kv_out = lambda a: jnp.transpose(a.reshape(depth, bp, H_A, DH_A, seq), (0, 1, 4, 2, 3))
    return (xp.reshape(bp, seq, D_MODEL), xs.reshape(bs, 1, D_MODEL),
            kv_out(kv_prompt[0]), kv_out(kv_prompt[1]), st("cp"), st("np"), st("mp"),
            st("ks"), st("vs"), st("cs"), st("ns"), st("ms"))
```

```python
import functools
import math

import jax
import jax.numpy as jnp
from jax import lax
from jax.experimental import pallas as pl
from jax.experimental.pallas import tpu as pltpu

F32 = jnp.float32
BF16 = jnp.bfloat16
HIGHEST = lax.Precision.HIGHEST

D_MODEL = 1024
H_M, DH_M = 4, 128
MW_M = H_M * DH_M
H_A, DH_A = 8, 64
MW_A = H_A * DH_A
MOBA_BLOCK = 256
MOBA_TOPK = 3
N_BUCKETS = 32
REL_MAX_DIST = 2048
D_FF = 2816
EPS = 1e-6
NEG = -1e30
PAGE_SIZE = 128

N_GATE_COLS = 2 * H_M
COL_M_END = 4 * MW_M
COL_IF_END = COL_M_END + N_GATE_COLS
COL_Q_END = COL_IF_END + MW_A
COL_K_END = COL_Q_END + MW_A
COL_V_END = COL_K_END + MW_A
LOG_K_SCALE = math.log(DH_M ** -0.5)
Q_SCALE = DH_A ** -0.5
LOG2E = math.log2(math.e)

N_NEAR = (REL_MAX_DIST + MOBA_BLOCK - 1) // MOBA_BLOCK + 1
assert (N_NEAR * MOBA_BLOCK - (MOBA_BLOCK - 1)) >= REL_MAX_DIST
TILE_ZERO = N_NEAR
TILE_MASKED = N_NEAR + 1
N_TILES = N_NEAR + 2
MOBA_GROUP = 4

PAIR = 2 * DH_A
K_AUG = 2 * PAIR
SEL_HI, SEL_LO = PAIR, PAIR + 32
V_ROWS = DH_A + 16
MAX_BLOCKS = 32

VMEM_LIMIT = 56 * 1024 * 1024

TM_PROJ = 512
TM_FFN = 256
L_CHUNK = 256
FF_CHUNK = 1408
PAGES_PER_STEP = 16
ATTEND_HEADS = 2


def _params(*sem):
    return pltpu.CompilerParams(dimension_semantics=sem, vmem_limit_bytes=VMEM_LIMIT)


def _rms(x, g):
    return x * lax.rsqrt(jnp.mean(x * x, axis=-1, keepdims=True) + EPS) * g


def _log_sigmoid(x):
    return jnp.minimum(x, 0.0) - jnp.log(1.0 + jnp.exp(-jnp.abs(x)))


def _dot(a, b, precision=None):
    return jnp.dot(a, b, precision=precision, preferred_element_type=F32)


def _dot_nt(a, b, precision=None):
    return lax.dot_general(a, b, (((1,), (1,)), ((), ())), precision=precision,
                           preferred_element_type=F32)


def _dot_tn(a, b):
    return lax.dot_general(a, b, (((0,), (0,)), ((), ())), preferred_element_type=F32)


def _inproj_prompt_kernel(x_ref, g_ref, wm_ref, wif_ref, wq_ref, wk_ref, wv_ref, wg_ref, *refs):
    (m_ref, gc_ref, gr_ref, qt_ref, ka_ref, kt32_ref, vt32_ref, vte_ref, km_ref, sg_ref) = refs[-10:]
    tm = x_ref.shape[0]
    xn = _rms(x_ref[...], g_ref[...])
    xb = xn.astype(BF16)
    m_ref[...] = _dot_nt(xb, wm_ref[...]).astype(BF16)
    gates = jnp.concatenate(
        [jnp.sum(xn * wif_ref[c:c + 1, :], axis=1, keepdims=True) for c in range(N_GATE_COLS)],
        axis=1)
    gc_ref[...] = gates
    gr_ref[...] = gates.T
    qt_ref[...] = (_dot_nt(wq_ref[...], xb) * (Q_SCALE * LOG2E)).astype(BF16)

    k = _dot_nt(xb, wk_ref[...])
    kt = k.T
    for d in range(kt32_ref.shape[0]):
        kt32_ref[d, 0] = kt
    km_ref[0] = jnp.mean(k.reshape(tm // MOBA_BLOCK, MOBA_BLOCK, MW_A), axis=1)
    row = lax.broadcasted_iota(jnp.int32, (tm, PAIR), 0)
    lane = lax.broadcasted_iota(jnp.int32, (tm, PAIR), 1)
    blk = (pl.program_id(1) * tm + row) // MOBA_BLOCK
    onehot = jnp.where(jnp.logical_and(lane < 2 * MAX_BLOCKS, lane % MAX_BLOCKS == blk),
                       1.0, 0.0).astype(BF16)
    kb = k.astype(BF16)
    for p in range(H_A // 2):
        ka_ref[:, p * K_AUG:p * K_AUG + PAIR] = kb[:, p * PAIR:(p + 1) * PAIR]
        ka_ref[:, p * K_AUG + PAIR:(p + 1) * K_AUG] = onehot

    vt = _dot_nt(wv_ref[...], xb)
    for d in range(vt32_ref.shape[0]):
        vt32_ref[d, 0] = vt
    ones_rows = jnp.where(lax.broadcasted_iota(jnp.int32, (V_ROWS - DH_A, tm), 0) == 0,
                          1.0, 0.0).astype(BF16)
    for h in range(H_A):
        vte_ref[h * V_ROWS:h * V_ROWS + DH_A, :] = vt[h * DH_A:(h + 1) * DH_A].astype(BF16)
        vte_ref[h * V_ROWS + DH_A:(h + 1) * V_ROWS, :] = ones_rows
    sg_ref[...] = jax.nn.sigmoid(_dot_nt(xb, wg_ref[...])).astype(BF16)


def _inproj_prompt(x, g, wm, wif, wq, wk, wv, wg, batch, layer, depth, kv_prev):
    m = x.shape[0]
    seq = m // batch
    tm = TM_PROJ
    nt = seq // tm
    n_pairs = H_A // 2
    row = lambda w: pl.BlockSpec((tm, w), lambda b, t: (b * nt + t, 0))
    col = lambda h: pl.BlockSpec((h, tm), lambda b, t: (0, b * nt + t))
    full = lambda a: pl.BlockSpec(a.shape, lambda b, t: (0,) * a.ndim)
    kv_spec = (pl.BlockSpec((depth, 1, MW_A, tm), lambda b, t: (0, b, 0, t)) if kv_prev is None
               else pl.BlockSpec((1, 1, MW_A, tm), lambda b, t: (layer, b, 0, t)))
    operands = [x, g, wm, wif, wq, wk, wv, wg]
    in_specs = [row(D_MODEL), full(g), full(wm), full(wif), full(wq), full(wk), full(wv), full(wg)]
    aliases = {}
    if kv_prev is not None:
        aliases = {len(operands): 5, len(operands) + 1: 6}
        operands += list(kv_prev)
        in_specs += [pl.BlockSpec(memory_space=pl.ANY)] * 2
    return pl.pallas_call(
        _inproj_prompt_kernel,
        grid=(batch, nt),
        in_specs=in_specs,
        input_output_aliases=aliases,
        out_specs=[row(COL_M_END), row(N_GATE_COLS), col(N_GATE_COLS), col(MW_A),
                   row(n_pairs * K_AUG), kv_spec, kv_spec,
                   col(H_A * V_ROWS),
                   pl.BlockSpec((1, tm // MOBA_BLOCK, MW_A), lambda b, t: (b * nt + t, 0, 0)),
                   row(2 * D_MODEL)],
        out_shape=[jax.ShapeDtypeStruct((m, COL_M_END), BF16),
                   jax.ShapeDtypeStruct((m, N_GATE_COLS), F32),
                   jax.ShapeDtypeStruct((N_GATE_COLS, m), F32),
                   jax.ShapeDtypeStruct((MW_A, m), BF16),
                   jax.ShapeDtypeStruct((m, n_pairs * K_AUG), BF16),
                   jax.ShapeDtypeStruct((depth, batch, MW_A, seq), F32),
                   jax.ShapeDtypeStruct((depth, batch, MW_A, seq), F32),
                   jax.ShapeDtypeStruct((H_A * V_ROWS, m), BF16),
                   jax.ShapeDtypeStruct((m // tm, tm // MOBA_BLOCK, MW_A), F32),
                   jax.ShapeDtypeStruct((m, 2 * D_MODEL), BF16)],
        compiler_params=_params("parallel", "parallel"),
        name="inproj_prompt",
    )(*operands)


def _inproj_sample_kernel(x_ref, g_ref, wm_ref, wif_ref, wq_ref, wk_ref, wv_ref, wg_ref,
                          m_ref, mt_ref, gc_ref, qkv_ref, qt_ref, sg_ref):
    xn = _rms(x_ref[...], g_ref[...])
    xb = xn.astype(BF16)
    m_ref[...] = _dot_nt(xb, wm_ref[...])
    mt_ref[...] = _dot_nt(wm_ref[:2 * MW_M, :], xb)
    gc_ref[...] = _dot_nt(xn, wif_ref[...], HIGHEST)
    qkv_ref[0] = _dot_nt(xb, wq_ref[...])
    qkv_ref[1] = _dot_nt(xb, wk_ref[...])
    qkv_ref[2] = _dot_nt(xb, wv_ref[...])
    qt_ref[...] = _dot_nt(wq_ref[...], xb)
    sg_ref[...] = jax.nn.sigmoid(_dot_nt(xb, wg_ref[...])).astype(BF16)


def _inproj_sample(x, g, wm, wif, wq, wk, wv, wg):
    m = x.shape[0]
    return pl.pallas_call(
        _inproj_sample_kernel,
        out_shape=[jax.ShapeDtypeStruct((m, COL_M_END), F32),
                   jax.ShapeDtypeStruct((2 * MW_M, m), F32),
                   jax.ShapeDtypeStruct((m, N_GATE_COLS), F32),
                   jax.ShapeDtypeStruct((3, m, MW_A), F32),
                   jax.ShapeDtypeStruct((MW_A, m), F32),
                   jax.ShapeDtypeStruct((m, 2 * D_MODEL), BF16)],
        compiler_params=pltpu.CompilerParams(vmem_limit_bytes=VMEM_LIMIT),
        name="inproj_sample",
    )(x, g, wm, wif, wq, wk, wv, wg)


def _mlstm_prompt_kernel(m_ref, gc_ref, gr_ref, bc_ref, br_ref, gm_ref,
                         h_ref, c_ref, ms_ref):
    L = m_ref.shape[0]

    @pl.when(pl.program_id(1) == 0)
    def _():
        c_ref[...] = jnp.zeros_like(c_ref)
        ms_ref[...] = jnp.zeros_like(ms_ref)

    row = lax.broadcasted_iota(jnp.int32, (L, L), 0)
    col = lax.broadcasted_iota(jnp.int32, (L, L), 1)
    causal = col <= row
    lower = causal.astype(F32)
    upper = (row <= col).astype(F32)

    gcol = gc_ref[...] + br_ref[...]
    grow = gr_ref[...] + bc_ref[...]
    bcol = _dot(lower, _log_sigmoid(gcol), HIGHEST)
    brow = _dot(_log_sigmoid(grow), upper, HIGHEST)

    lane = lax.broadcasted_iota(jnp.int32, (L, DH_M), 1)
    ones_blk = jnp.where(lane == 0, 1.0, 0.0).astype(BF16)

    def operands(h):
        return (m_ref[:, h * DH_M:(h + 1) * DH_M],
                m_ref[:, MW_M + h * DH_M:MW_M + (h + 1) * DH_M],
                m_ref[:, 2 * MW_M + h * DH_M:2 * MW_M + (h + 1) * DH_M])

    qk_all = [_dot_nt(operands(h)[0], operands(h)[1]) for h in range(H_M)]
    qc_all = [_dot(operands(h)[0], c_ref[0, h].astype(BF16)) for h in range(H_M)]

    heads = range(H_M)
    i_col = [gcol[:, h:h + 1] for h in heads]
    b_col = [bcol[:, H_M + h:H_M + h + 1] for h in heads]
    i_row = [grow[h:h + 1, :] for h in heads]
    b_row = [brow[H_M + h:H_M + h + 1, :] for h in heads]
    b_last = [b_col[h][L - 1:L, :] for h in heads]
    m_prev = [ms_ref[0, h][0:1, 0:1] for h in heads]
    v_ext = [jnp.concatenate([operands(h)[2], ones_blk], axis=1) for h in heads]

    mt, a, sv = [], [], []
    for h in heads:
        u_row = i_row[h] - b_row[h]
        u_max = jnp.max(jnp.where(causal, u_row, NEG), axis=1, keepdims=True)
        inter = b_col[h] + m_prev[h]
        mt.append(jnp.maximum(inter, b_col[h] + u_max))
        a.append(jnp.exp(inter - mt[h]))
        expo = jnp.where(causal, (b_col[h] - mt[h] + LOG_K_SCALE) + u_row, NEG)
        s = qk_all[h] * jnp.exp(expo)
        sv.append(_dot(s.astype(BF16), v_ext[h]))

    for h in heads:
        k = operands(h)[1]
        g_row = b_last[h] - b_row[h] + i_row[h]
        e = b_last[h] + m_prev[h]
        m_new = jnp.maximum(e, jnp.max(g_row, axis=1, keepdims=True))
        w_col = jnp.exp(b_last[h] - b_col[h] + i_col[h] - (m_new - LOG_K_SCALE))
        kw = (k.astype(F32) * w_col).astype(BF16)
        c_ref[0, h] = jnp.exp(e - m_new) * c_ref[0, h] + _dot_tn(kw, v_ext[h])
        ms_ref[0, h] = jnp.broadcast_to(m_new, ms_ref.shape[2:])

    for h in heads:
        o = m_ref[:, 3 * MW_M + h * DH_M:3 * MW_M + (h + 1) * DH_M]
        nd = a[h] * qc_all[h] + sv[h]
        num = nd[:, :DH_M]
        denom = jnp.maximum(jnp.abs(nd[:, DH_M:DH_M + 1]), jnp.exp(-mt[h]))
        nc = num - jnp.mean(num, axis=1, keepdims=True)
        var = jnp.mean(nc * nc, axis=1, keepdims=True)
        hn = nc * lax.rsqrt(var + EPS * denom * denom) * gm_ref[:, h * DH_M:(h + 1) * DH_M]
        h_ref[:, h * DH_M:(h + 1) * DH_M] = (hn * jax.nn.sigmoid(o.astype(F32))).astype(BF16)


def _mlstm_prompt(mproj, gcol, grow, b_if, g_mlstm, batch):
    m = mproj.shape[0]
    L = L_CHUNK
    nc = m // batch // L
    bias_row = b_if.reshape(1, N_GATE_COLS)
    bias_col = b_if.reshape(N_GATE_COLS, 1)
    gm = g_mlstm.reshape(1, MW_M)
    return pl.pallas_call(
        _mlstm_prompt_kernel,
        grid=(batch, nc),
        in_specs=[pl.BlockSpec((L, COL_M_END), lambda b, c: (b * nc + c, 0)),
                  pl.BlockSpec((L, N_GATE_COLS), lambda b, c: (b * nc + c, 0)),
                  pl.BlockSpec((N_GATE_COLS, L), lambda b, c: (0, b * nc + c)),
                  pl.BlockSpec((N_GATE_COLS, 1), lambda b, c: (0, 0)),
                  pl.BlockSpec((1, N_GATE_COLS), lambda b, c: (0, 0)),
                  pl.BlockSpec((1, MW_M), lambda b, c: (0, 0))],
        out_specs=[pl.BlockSpec((L, MW_M), lambda b, c: (b * nc + c, 0)),
                   pl.BlockSpec((1, H_M, DH_M, 2 * DH_M), lambda b, c: (b, 0, 0, 0)),
                   pl.BlockSpec((1, H_M, 8, 128), lambda b, c: (b, 0, 0, 0))],
        out_shape=[jax.ShapeDtypeStruct((m, MW_M), BF16),
                   jax.ShapeDtypeStruct((batch, H_M, DH_M, 2 * DH_M), F32),
                   jax.ShapeDtypeStruct((batch, H_M, 8, 128), F32)],
        compiler_params=_params("parallel", "arbitrary"),
        name="mlstm_prompt",
    )(mproj, gcol, grow, bias_col, bias_row, gm)


def _mlstm_sample_kernel(m_ref, mt_ref, gc_ref, bi_ref, gm_ref, c_ref, n_ref, ms_ref,
                         h_ref, co_ref, no_ref, mo_ref):
    b = pl.program_id(0)
    nb = mt_ref.shape[1]
    onehot = lax.broadcasted_iota(jnp.int32, (1, nb), 1) == b
    g = gc_ref[0] + bi_ref[...]
    for h in range(H_M):
        sl = slice(h * DH_M, (h + 1) * DH_M)
        q_row = m_ref[0, :, h * DH_M:(h + 1) * DH_M]
        k_row = m_ref[0, :, MW_M + h * DH_M:MW_M + (h + 1) * DH_M]
        v_row = m_ref[0, :, 2 * MW_M + h * DH_M:2 * MW_M + (h + 1) * DH_M]
        o_row = m_ref[0, :, 3 * MW_M + h * DH_M:3 * MW_M + (h + 1) * DH_M]
        q_col = jnp.sum(jnp.where(onehot, mt_ref[sl, :], 0.0), axis=1, keepdims=True)
        k_col = jnp.sum(jnp.where(onehot, mt_ref[MW_M + h * DH_M:MW_M + (h + 1) * DH_M, :], 0.0),
                        axis=1, keepdims=True)
        i_pre = g[:, h:h + 1]
        log_f = _log_sigmoid(g[:, H_M + h:H_M + h + 1])
        m_prev = ms_ref[0, :, h:h + 1]
        inter = log_f + m_prev
        m_new = jnp.maximum(inter, i_pre)
        a = jnp.exp(inter - m_new)
        w = jnp.exp(i_pre - (m_new - LOG_K_SCALE))
        c = c_ref[0, h]
        n = n_ref[0, h:h + 1, :]
        s = jnp.sum(q_row * k_row, axis=1, keepdims=True) * w
        num = a * jnp.sum(c * q_col, axis=0, keepdims=True) + s * v_row
        den = a * jnp.sum(q_row * n, axis=1, keepdims=True) + s
        hh = num / jnp.maximum(jnp.abs(den), jnp.exp(-m_new))
        mu = jnp.mean(hh, axis=1, keepdims=True)
        hc = hh - mu
        var = jnp.mean(hc * hc, axis=1, keepdims=True)
        hn = hc * lax.rsqrt(var + EPS) * gm_ref[:, sl]
        h_ref[0, :, sl] = (hn * jax.nn.sigmoid(o_row)).astype(BF16)
        co_ref[0, h] = a * c + (w * k_col) * v_row
        no_ref[0, h:h + 1, :] = a * n + w * k_row
        mo_ref[0, :, h:h + 1] = m_new


def _mlstm_sample(mproj, mproj_t, gcol, b_if, g_mlstm, c0, n0, m0):
    nb = mproj.shape[0]
    return pl.pallas_call(
        _mlstm_sample_kernel,
        grid=(nb,),
        in_specs=[pl.BlockSpec((1, 1, COL_M_END), lambda b: (b, 0, 0)),
                  pl.BlockSpec((2 * MW_M, nb), lambda b: (0, 0)),
                  pl.BlockSpec((1, 1, N_GATE_COLS), lambda b: (b, 0, 0)),
                  pl.BlockSpec((1, N_GATE_COLS), lambda b: (0, 0)),
                  pl.BlockSpec((1, MW_M), lambda b: (0, 0)),
                  pl.BlockSpec((1, H_M, DH_M, DH_M), lambda b: (b, 0, 0, 0)),
                  pl.BlockSpec((1, H_M, DH_M), lambda b: (b, 0, 0)),
                  pl.BlockSpec((1, 1, H_M), lambda b: (b, 0, 0))],
        out_specs=[pl.BlockSpec((1, 1, MW_M), lambda b: (b, 0, 0)),
                   pl.BlockSpec((1, H_M, DH_M, DH_M), lambda b: (b, 0, 0, 0)),
                   pl.BlockSpec((1, H_M, DH_M), lambda b: (b, 0, 0)),
                   pl.BlockSpec((1, 1, H_M), lambda b: (b, 0, 0))],
        out_shape=[jax.ShapeDtypeStruct((nb, 1, MW_M), BF16),
                   jax.ShapeDtypeStruct((nb, H_M, DH_M, DH_M), F32),
                   jax.ShapeDtypeStruct((nb, H_M, DH_M), F32),
                   jax.ShapeDtypeStruct((nb, 1, H_M), F32)],
        compiler_params=_params("parallel"),
        name="mlstm_sample",
    )(mproj.reshape(nb, 1, COL_M_END), mproj_t, gcol.reshape(nb, 1, N_GATE_COLS),
      b_if.reshape(1, N_GATE_COLS), g_mlstm.reshape(1, MW_M), c0, n0, m0.reshape(nb, 1, H_M))


def _t5_bucket(dist):
    n = jnp.maximum(dist, 0)
    max_exact = N_BUCKETS // 2
    nf = jnp.maximum(n, 1).astype(F32)
    large = max_exact + (jnp.log(nf / max_exact) / math.log(REL_MAX_DIST / max_exact)
                         * (N_BUCKETS - max_exact)).astype(jnp.int32)
    large = jnp.minimum(large, N_BUCKETS - 1)
    return jnp.where(n < max_exact, n, large)


def _t5_bias(dist, rel_ref, h):
    bucket = _t5_bucket(dist)
    bias = jnp.zeros(dist.shape, F32)
    for kb in range(N_BUCKETS):
        bias = jnp.where(bucket == kb, rel_ref[kb, h], bias)
    return bias


def _bias_table_kernel(rel_ref, o_ref):
    delta = pl.program_id(0)
    c = lax.broadcasted_iota(jnp.int32, (MOBA_BLOCK, MOBA_BLOCK), 0)
    r = lax.broadcasted_iota(jnp.int32, (MOBA_BLOCK, MOBA_BLOCK), 1)
    dist = delta * MOBA_BLOCK + r - c
    bucket = _t5_bucket(dist)
    for h in range(H_A):
        bias = jnp.zeros(dist.shape, F32)
        for kb in range(N_BUCKETS):
            bias = jnp.where(bucket == kb, rel_ref[kb, h], bias)
        o_ref[h, 0] = jnp.where(dist >= 0, bias * LOG2E, NEG)


def _bias_table(rel_bias):
    return pl.pallas_call(
        _bias_table_kernel,
        grid=(N_NEAR,),
        in_specs=[pl.BlockSpec(memory_space=pltpu.SMEM)],
        out_specs=pl.BlockSpec((H_A, 1, MOBA_BLOCK, MOBA_BLOCK), lambda d: (0, d, 0, 0)),
        out_shape=jax.ShapeDtypeStruct((H_A, N_NEAR, MOBA_BLOCK, MOBA_BLOCK), F32),
        compiler_params=_params("parallel"),
        name="bias_table",
    )(rel_bias)


def _select_topk_t(score, n_valid_rows, k):
    nb = score.shape[0]
    row = lax.broadcasted_iota(jnp.int32, score.shape, 0)
    past = row < n_valid_rows
    sc = jnp.where(past, score, -jnp.inf)
    sel = jnp.full(score.shape, NEG, F32)
    for _ in range(k):
        mx = jnp.max(sc, axis=0, keepdims=True)
        idx = jnp.min(jnp.where(sc == mx, row, nb), axis=0, keepdims=True)
        hit = row == idx
        sel = jnp.where(hit, 0.0, sel)
        sc = jnp.where(hit, -jnp.inf, sc)
    return jnp.where(past, sel, NEG)


def _moba_prompt_kernel(qi_ref, grp_ref, first_ref, last_ref, far_item_ref, far_ref,
                        qt_ref, ka_ref, vte_ref, km_ref, tab_ref, o_ref,
                        qp_scr, m_scr, acc_scr, sa_scr, sb_scr, gma_scr, gmb_scr):
    hp = pl.program_id(1)
    seq = qt_ref.shape[1]
    nb = km_ref.shape[1]
    tq = MOBA_BLOCK
    n_items = qi_ref.shape[0] - 1
    setup_w = 4 * MOBA_BLOCK

    def setup(c, carry):
        cols = pl.ds(pl.multiple_of(c * setup_w, setup_w), setup_w)
        blk_row = lax.broadcasted_iota(jnp.int32, (nb, setup_w), 0)
        q_blk = (c * setup_w + lax.broadcasted_iota(jnp.int32, (1, setup_w), 1)) // MOBA_BLOCK
        sub = lax.broadcasted_iota(jnp.int32, (PAIR, setup_w), 0)
        qpair = qt_ref[:, cols]
        for h in range(2):
            qth = qt_ref[h * DH_A:(h + 1) * DH_A, cols].astype(F32)
            score = _dot(km_ref[0, :, h * DH_A:(h + 1) * DH_A], qth, HIGHEST)
            sel = _select_topk_t(score, q_blk, MOBA_TOPK)
            sel = jnp.where(blk_row == q_blk, 0.0, sel)
            sel = sel + jnp.where(q_blk - blk_row >= N_NEAR, far_ref[2 * hp + h], 0.0)
            hi = sel.astype(BF16)
            qp_scr[h, 0:PAIR, cols] = jnp.where(sub // DH_A == h, qpair, jnp.zeros_like(qpair))
            qp_scr[h, PAIR:, cols] = jnp.zeros((K_AUG - PAIR, setup_w), BF16)
            qp_scr[h, SEL_HI:SEL_HI + nb, cols] = hi
            qp_scr[h, SEL_LO:SEL_LO + nb, cols] = (sel - hi.astype(F32)).astype(BF16)
        return carry

    lax.fori_loop(0, seq // setup_w, setup, 0)
    m_scr[...] = jnp.full(m_scr.shape, NEG, F32)
    acc_scr[...] = jnp.zeros(acc_scr.shape, F32)

    gk = MOBA_GROUP * MOBA_BLOCK

    def score_block(w, t, h, far):
        qi = qi_ref[w]
        g = grp_ref[w]
        kblk = ka_ref[pl.ds(pl.multiple_of((g * MOBA_GROUP + t) * MOBA_BLOCK, MOBA_BLOCK),
                            MOBA_BLOCK), :]
        qcols = pl.ds(pl.multiple_of(qi * tq, tq), tq)
        s = _dot(kblk, qp_scr[h, :, qcols])
        if far:
            return s
        delta = qi - (g * MOBA_GROUP + t)
        tile = jnp.where(delta < 0, TILE_MASKED, jnp.minimum(delta, TILE_ZERO))
        return s + tab_ref[h, tile]

    def step(w, s_cur, gm_cur, s_nxt, gm_nxt, far):
        qi = qi_ref[w]
        start = pl.multiple_of(grp_ref[w] * gk, gk)
        first = first_ref[w] == 1
        m_old = [jnp.where(first, NEG, m_scr[h]) for h in range(2)]
        m_new = [jnp.maximum(m_old[h], gm_cur[h]) for h in range(2)]
        pv = [None, None]
        gmax = [None, None]

        def next_scores(t):
            rows = slice(t * MOBA_BLOCK, (t + 1) * MOBA_BLOCK)
            for h in range(2):
                s = score_block(w + 1, t, h, far)
                s_nxt[h, rows, :] = s
                cmax = jnp.max(s, axis=0, keepdims=True)
                gmax[h] = cmax if t == 0 else jnp.maximum(gmax[h], cmax)

        def attend(t):
            rows = slice(t * MOBA_BLOCK, (t + 1) * MOBA_BLOCK)
            for h in range(2):
                p = jnp.exp2(s_cur[h, rows, :] - m_new[h]).astype(BF16)
                vblk = vte_ref[h * V_ROWS:(h + 1) * V_ROWS,
                               pl.ds(start + t * MOBA_BLOCK, MOBA_BLOCK)]
                d = _dot(vblk, p)
                pv[h] = d if t == 0 else pv[h] + d

        for t in range(MOBA_GROUP):
            next_scores(t)
            attend(t)
        for h in range(2):
            acc_old = jnp.where(first, 0.0, acc_scr[h])
            acc_scr[h] = jnp.exp2(m_old[h] - m_new[h]) * acc_old + pv[h]
            m_scr[h] = m_new[h]
            gm_nxt[h] = gmax[h]

        @pl.when(last_ref[w] == 1)
        def _():
            outs = [acc_scr[h, 0:DH_A, :] / acc_scr[h, DH_A:DH_A + 1, :] for h in range(2)]
            orows = pl.ds(pl.multiple_of(qi * tq, tq), tq)
            o_ref[orows, :] = jnp.concatenate(outs, axis=0).T.astype(BF16)

    for h in range(2):
        gmax = None
        for t in range(MOBA_GROUP):
            s = score_block(0, t, h, False)
            sa_scr[h, t * MOBA_BLOCK:(t + 1) * MOBA_BLOCK, :] = s
            cmax = jnp.max(s, axis=0, keepdims=True)
            gmax = cmax if t == 0 else jnp.maximum(gmax, cmax)
        gma_scr[h] = gmax

    def step_by_kind(w, *buffers):
        next_is_far = far_item_ref[w + 1] == 1
        pl.when(next_is_far)(lambda: step(w, *buffers, True))
        pl.when(jnp.logical_not(next_is_far))(lambda: step(w, *buffers, False))

    def pair_of_items(i, carry):
        w = 2 * i
        step_by_kind(w, sa_scr, gma_scr, sb_scr, gmb_scr)
        step_by_kind(w + 1, sb_scr, gmb_scr, sa_scr, gma_scr)
        return carry

    lax.fori_loop(0, n_items // 2, pair_of_items, 0)


def _moba_work_items(nb):
    qi, grp, first, last, far = [], [], [], [], []
    for q in range(nb):
        n_groups = q // MOBA_GROUP + 1
        for g in range(n_groups):
            qi.append(q)
            grp.append(g)
            first.append(int(g == 0))
            last.append(int(g == n_groups - 1))
            far.append(int(q - (g * MOBA_GROUP + MOBA_GROUP - 1) >= N_NEAR))
    assert len(qi) % 2 == 0
    tables = [qi + qi[-1:], grp + grp[-1:], first + [0], last + [0], far + [0]]
    return [jnp.asarray(t, jnp.int32) for t in tables]


def _moba_prompt(qt, ka, vte, kmean, table, far, batch):
    m = qt.shape[1]
    seq = m // batch
    nb = seq // MOBA_BLOCK
    assert nb <= MAX_BLOCKS and nb % MOBA_GROUP == 0
    tq = MOBA_BLOCK
    n_pairs = H_A // 2
    gk = MOBA_GROUP * MOBA_BLOCK
    smem = pl.BlockSpec(memory_space=pltpu.SMEM)
    return pl.pallas_call(
        _moba_prompt_kernel,
        grid=(batch, n_pairs),
        in_specs=[smem, smem, smem, smem, smem, smem,
                  pl.BlockSpec((PAIR, seq), lambda b, hp: (hp, b)),
                  pl.BlockSpec((seq, K_AUG), lambda b, hp: (b, hp)),
                  pl.BlockSpec((2 * V_ROWS, seq), lambda b, hp: (hp, b)),
                  pl.BlockSpec((1, nb, PAIR), lambda b, hp: (b, 0, hp)),
                  pl.BlockSpec((2, N_TILES, tq, tq), lambda b, hp: (hp, 0, 0, 0),
                               pipeline_mode=pl.Buffered(1))],
        out_specs=pl.BlockSpec((seq, PAIR), lambda b, hp: (b, hp)),
        out_shape=jax.ShapeDtypeStruct((m, MW_A), BF16),
        scratch_shapes=[pltpu.VMEM((2, K_AUG, seq), BF16),
                        pltpu.VMEM((2, 1, tq), F32),
                        pltpu.VMEM((2, V_ROWS, tq), F32),
                        pltpu.VMEM((2, gk, tq), F32),
                        pltpu.VMEM((2, gk, tq), F32),
                        pltpu.VMEM((2, 1, tq), F32),
                        pltpu.VMEM((2, 1, tq), F32)],
        compiler_params=_params("parallel", "arbitrary"),
        name="moba_prompt",
    )(*_moba_work_items(nb), far, qt, ka, vte, kmean.reshape(batch, nb, MW_A), table)


def _logits_sample_kernel(pt_ref, qt_ref, *refs):
    o_ref, qb_scr = refs[-2], refs[-1]
    b = pl.program_id(0)
    nb_s = qt_ref.shape[1]

    @pl.when(pl.program_id(1) == 0)
    def _():
        onehot = lax.broadcasted_iota(jnp.int32, (1, nb_s), 1) == b
        qcol = jnp.sum(jnp.where(onehot, qt_ref[...], 0.0), axis=1, keepdims=True)
        qb_scr[...] = jnp.broadcast_to(qcol * Q_SCALE, qb_scr.shape)

    per_block = MOBA_BLOCK // PAGE_SIZE
    for i in range(PAGES_PER_STEP):
        for h in range(H_A):
            kt = refs[i][0, 0, h]
            lg = jnp.sum(kt * qb_scr[h * DH_A:(h + 1) * DH_A, :], axis=0, keepdims=True)
            r, half = i // per_block, i % per_block
            o_ref[0, h, r:r + 1, half * PAGE_SIZE:(half + 1) * PAGE_SIZE] = lg


def _logits_sample(cache_kt, page_table, qt, layer):
    nb_s, n_pages = page_table.shape
    steps = n_pages // PAGES_PER_STEP
    rows = PAGES_PER_STEP * PAGE_SIZE // MOBA_BLOCK

    def page_spec(i):
        return pl.BlockSpec((1, 1, H_A, DH_A, PAGE_SIZE),
                            lambda b, s, pt: (layer, pt[b, s * PAGES_PER_STEP + i], 0, 0, 0))

    return pl.pallas_call(
        _logits_sample_kernel,
        grid_spec=pltpu.PrefetchScalarGridSpec(
            num_scalar_prefetch=1,
            grid=(nb_s, steps),
            in_specs=[pl.BlockSpec(qt.shape, lambda b, s, pt: (0, 0))]
                     + [page_spec(i) for i in range(PAGES_PER_STEP)],
            out_specs=pl.BlockSpec((1, H_A, rows, MOBA_BLOCK), lambda b, s, pt: (b, 0, s, 0)),
            scratch_shapes=[pltpu.VMEM((MW_A, PAGE_SIZE), F32)]),
        out_shape=jax.ShapeDtypeStruct((nb_s, H_A, steps * rows, MOBA_BLOCK), F32),
        compiler_params=_params("parallel", "arbitrary"),
        name="logits_sample",
    )(page_table, qt, *([cache_kt] * PAGES_PER_STEP))


def _select_sample_kernel(lg_ref, o_ref):
    sc = jnp.sum(lg_ref[...], axis=-1)
    nblk = sc.shape[-1]
    blk = lax.broadcasted_iota(jnp.int32, sc.shape, 2)
    for t in range(MOBA_TOPK):
        mx = jnp.max(sc, axis=2, keepdims=True)
        idx = jnp.min(jnp.where(sc == mx, blk, nblk), axis=2, keepdims=True)
        o_ref[:, :, t:t + 1] = idx
        sc = jnp.where(blk == idx, -jnp.inf, sc)


def _select_sample(logits):
    nb_s = logits.shape[0]
    return pl.pallas_call(
        _select_sample_kernel,
        out_shape=jax.ShapeDtypeStruct((nb_s, H_A, MOBA_TOPK), jnp.int32),
        compiler_params=pltpu.CompilerParams(vmem_limit_bytes=VMEM_LIMIT),
        name="select_sample",
    )(logits)


def _bias_sample_kernel(rel_ref, o_ref, *, past_len):
    h = pl.program_id(0)
    nblk = o_ref.shape[1]
    kpos = (lax.broadcasted_iota(jnp.int32, (nblk, MOBA_BLOCK), 0) * MOBA_BLOCK
            + lax.broadcasted_iota(jnp.int32, (nblk, MOBA_BLOCK), 1))
    o_ref[0] = _t5_bias(past_len - kpos, rel_ref, h)


def _bias_sample(rel_bias, past_len):
    nblk = past_len // MOBA_BLOCK
    return pl.pallas_call(
        functools.partial(_bias_sample_kernel, past_len=past_len),
        grid=(H_A,),
        in_specs=[pl.BlockSpec(memory_space=pltpu.SMEM)],
        out_specs=pl.BlockSpec((1, nblk, MOBA_BLOCK), lambda h: (h, 0, 0)),
        out_shape=jax.ShapeDtypeStruct((H_A, nblk, MOBA_BLOCK), F32),
        compiler_params=_params("parallel"),
        name="bias_sample",
    )(rel_bias)


def _attend_sample_kernel(pt_ref, sel_ref, rel_ref, lg_ref, bias_ref, qkv_ref, *refs):
    o_ref = refs[-1]
    v_refs = refs[:-1]
    per_block = MOBA_BLOCK // PAGE_SIZE
    n_sel_pages = MOBA_TOPK * per_block
    b = pl.program_id(0)

    for j in range(ATTEND_HEADS):
        h = pl.program_id(1) * ATTEND_HEADS + j
        q = qkv_ref[0, 0, pl.ds(h, 1), :] * Q_SCALE
        k_new = qkv_ref[1, 0, pl.ds(h, 1), :]
        v_new = qkv_ref[2, 0, pl.ds(h, 1), :]
        own = jnp.sum(k_new * q, axis=1, keepdims=True) + rel_ref[0, h]
        logits = []
        mx = own
        for t in range(MOBA_TOPK):
            blk = sel_ref[b, h, t]
            lg = lg_ref[0, j, pl.ds(blk, 1), :] + bias_ref[j, pl.ds(blk, 1), :]
            logits.append(lg)
            mx = jnp.maximum(mx, jnp.max(lg, axis=1, keepdims=True))
        p_own = jnp.exp(own - mx)
        den = p_own
        acc = p_own * v_new
        for t in range(MOBA_TOPK):
            p = jnp.exp(logits[t] - mx)
            den = den + jnp.sum(p, axis=1, keepdims=True)
            pb = p.astype(BF16)
            for i in range(per_block):
                vt = v_refs[j * n_sel_pages + t * per_block + i][0, 0, 0].astype(BF16)
                acc = acc + _dot_nt(pb[:, i * PAGE_SIZE:(i + 1) * PAGE_SIZE], vt)
        o_ref[0, pl.ds(h, 1), :] = acc / den


def _attend_sample(cache_vt, page_table, sel, rel_bias, bias_s, logits, qkv, layer):
    nb_s = page_table.shape[0]
    per_block = MOBA_BLOCK // PAGE_SIZE
    n_sel_pages = MOBA_TOPK * per_block
    nblk = logits.shape[2]
    nh = ATTEND_HEADS

    def page_spec(j, t):
        def imap(b, hg, pt, sl, rel):
            h = hg * nh + j
            page = sl[b, h, t // per_block] * per_block + (t % per_block)
            return (layer, pt[b, page], h, 0, 0)
        return pl.BlockSpec((1, 1, 1, DH_A, PAGE_SIZE), imap)

    pages = [page_spec(j, t) for j in range(nh) for t in range(n_sel_pages)]
    return pl.pallas_call(
        _attend_sample_kernel,
        grid_spec=pltpu.PrefetchScalarGridSpec(
            num_scalar_prefetch=3,
            grid=(nb_s, H_A // nh),
            in_specs=[pl.BlockSpec((1, nh, nblk, MOBA_BLOCK), lambda b, hg, pt, sl, rel: (b, hg, 0, 0)),
                      pl.BlockSpec((nh, nblk, MOBA_BLOCK), lambda b, hg, pt, sl, rel: (hg, 0, 0)),
                      pl.BlockSpec((3, 1, H_A, DH_A), lambda b, hg, pt, sl, rel: (0, b, 0, 0))]
                     + pages,
            out_specs=pl.BlockSpec((1, H_A, DH_A), lambda b, hg, pt, sl, rel: (b, 0, 0))),
        out_shape=jax.ShapeDtypeStruct((nb_s, H_A, DH_A), F32),
        compiler_params=_params("parallel", "arbitrary"),
        name="attend_sample",
    )(page_table, sel, rel_bias, logits, bias_s, qkv, *([cache_vt] * len(pages)))


def _mix_ffn_kernel(x_ref, hm_ref, ha_ref, sg_ref, wbm_ref, wba_ref, wo_ref, gmix_ref,
                    gpre_ref, gpost_ref, wg_ref, wu_ref, wd_ref, o_ref):
    sg = sg_ref[...].astype(F32)
    mixed = (sg[:, :D_MODEL] * _dot(hm_ref[...], wbm_ref[...])
             + sg[:, D_MODEL:] * _dot(ha_ref[...], wba_ref[...]))
    x = x_ref[...] + _rms(_dot(mixed.astype(BF16), wo_ref[...]), gmix_ref[...])
    hf = _rms(x, gpre_ref[...]).astype(BF16)
    acc = jnp.zeros(x.shape, F32)
    for c in range(D_FF // FF_CHUNK):
        sl = slice(c * FF_CHUNK, (c + 1) * FF_CHUNK)
        gate = _dot(hf, wg_ref[:, sl])
        up = _dot(hf, wu_ref[:, sl])
        mid = (gate * jax.nn.sigmoid(gate) * up).astype(BF16)
        acc = acc + _dot(mid, wd_ref[sl, :])
    o_ref[...] = x + _rms(acc, gpost_ref[...])


def _mix_ffn(x, hm, ha, sg, wbm, wba, wo, gmix, gpre, gpost, wg, wu, wd):
    m = x.shape[0]
    tm = min(TM_FFN, m)
    row = lambda w: pl.BlockSpec((tm, w), lambda i: (i, 0))
    full = lambda a: pl.BlockSpec(a.shape, lambda i: (0,) * a.ndim, pipeline_mode=pl.Buffered(1))
    weights = (wbm, wba, wo, gmix, gpre, gpost, wg, wu, wd)
    return pl.pallas_call(
        _mix_ffn_kernel,
        grid=(m // tm,),
        in_specs=[row(D_MODEL), row(MW_M), row(MW_A), row(2 * D_MODEL)] + [full(w) for w in weights],
        out_specs=row(D_MODEL),
        out_shape=jax.ShapeDtypeStruct((m, D_MODEL), F32),
        compiler_params=_params("parallel"),
        name="mix_ffn",
    )(x, hm, ha, sg, *weights)


def kernel(x_prompt, x_sample, cache_k, cache_v, state_C, state_n, state_m, page_table,
           norm_mix_pre, norm_mix_post, norm_ffn_pre, norm_ffn_post, w_in, b_igate, b_fgate,
           g_mlstm, w_branch_m, w_branch_a, w_out, w_ffn_gate, w_ffn_up, w_ffn_down, rel_bias):
    bp, seq, _ = x_prompt.shape
    bs = x_sample.shape[0]
    depth = w_in.shape[0]
    xp = x_prompt.reshape(bp * seq, D_MODEL)
    xs = x_sample.reshape(bs, D_MODEL)
    cache_kt = jnp.transpose(cache_k, (0, 1, 3, 4, 2))
    cache_vt = jnp.transpose(cache_v, (0, 1, 3, 4, 2))
    w_in_t = jnp.swapaxes(w_in, 1, 2)

    tile = (H_A, 1, MOBA_BLOCK, MOBA_BLOCK)
    table = jnp.concatenate([_bias_table(rel_bias), jnp.zeros(tile, F32), jnp.full(tile, NEG, F32)],
                            axis=1)
    far = rel_bias[N_BUCKETS - 1] * LOG2E
    bias_s = _bias_sample(rel_bias, page_table.shape[1] * PAGE_SIZE)

    outs = {k: [] for k in ("cp", "np", "mp", "ks", "vs", "cs", "ns", "ms")}
    kv_prompt = None
    for l in range(depth):
        row = lambda a: a[l].reshape(1, -1)
        wt = w_in_t[l]
        wm = wt[:COL_M_END].astype(BF16)
        wif = wt[COL_M_END:COL_IF_END]
        wq = wt[COL_IF_END:COL_Q_END].astype(BF16)
        wk = wt[COL_Q_END:COL_K_END].astype(BF16)
        wv = wt[COL_K_END:COL_V_END].astype(BF16)
        wgt = wt[COL_V_END:].astype(BF16)
        b_if = jnp.concatenate([b_igate[l], b_fgate[l]])
        wbm, wba, wo = (w_branch_m[l].astype(BF16), w_branch_a[l].astype(BF16), w_out[l].astype(BF16))
        wfg, wfu, wfd = (w_ffn_gate[l].astype(BF16), w_ffn_up[l].astype(BF16), w_ffn_down[l].astype(BF16))

        mproj, gcol, grow, qt, ka, kt32, vt32, vte, kmean, sg = _inproj_prompt(
            xp, row(norm_mix_pre), wm, wif, wq, wk, wv, wgt, bp, l, depth, kv_prompt)
        kv_prompt = (kt32, vt32)
        hm, cext, mstate = _mlstm_prompt(mproj, gcol, grow, b_if, g_mlstm[l], bp)
        ha = _moba_prompt(qt, ka, vte, kmean, table, far, bp)
        post = (wbm, wba, wo, row(norm_mix_post), row(norm_ffn_pre), row(norm_ffn_post),
                wfg, wfu, wfd)
        xp = _mix_ffn(xp, hm, ha, sg, *post)
        outs["cp"].append(cext[..., :DH_M])
        outs["np"].append(cext[..., DH_M])
        outs["mp"].append(mstate[:, :, 0, 0])

        mproj_s, mproj_t, gcol_s, qkv_s, qt_s, sg_s = _inproj_sample(
            xs, row(norm_mix_pre), wm, wif, wq, wk, wv, wgt)
        hm_s, c_s, n_s, m_s = _mlstm_sample(mproj_s, mproj_t, gcol_s, b_if, g_mlstm[l],
                                            state_C[l], state_n[l], state_m[l])
        logits_s = _logits_sample(cache_kt, page_table, qt_s, l)
        sel = _select_sample(logits_s)
        ha_s = _attend_sample(cache_vt, page_table, sel, rel_bias, bias_s, logits_s,
                              qkv_s.reshape(3, bs, H_A, DH_A), l)
        xs = _mix_ffn(xs, hm_s.reshape(bs, MW_M), ha_s.reshape(bs, MW_A).astype(BF16), sg_s, *post)
        outs["ks"].append(qkv_s[1].reshape(bs, 1, H_A, DH_A))
        outs["vs"].append(qkv_s[2].reshape(bs, 1, H_A, DH_A))
        outs["cs"].append(c_s)
        outs["ns"].append(n_s)
        outs["ms"].append(m_s.reshape(bs, H_M))

    st = lambda k: jnp.stack(outs[k])
    kv_out = lambda a: jnp.transpose(a.reshape(depth, bp, H_A, DH_A, seq), (0, 1, 4, 2, 3))
    return (xp.reshape(bp, seq, D_MODEL), xs.reshape(bs, 1, D_MODEL),
            kv_out(kv_prompt[0]), kv_out(kv_prompt[1]), st("cp"), st("np"), st("mp"),
            st("ks"), st("vs"), st("cs"), st("ns"), st("ms"))
```

```python
import functools
import math

import jax
import jax.numpy as jnp
from jax import lax
from jax.experimental import pallas as pl
from jax.experimental.pallas import tpu as pltpu

F32 = jnp.float32
BF16 = jnp.bfloat16
HIGHEST = lax.Precision.HIGHEST

D_MODEL = 1024
H_M, DH_M = 4, 128
MW_M = H_M * DH_M
H_A, DH_A = 8, 64
MW_A = H_A * DH_A
MOBA_BLOCK = 256
MOBA_TOPK = 3
N_BUCKETS = 32
REL_MAX_DIST = 2048
D_FF = 2816
EPS = 1e-6
NEG = -1e30
PAGE_SIZE = 128

N_GATE_COLS = 2 * H_M
COL_M_END = 4 * MW_M
COL_IF_END = COL_M_END + N_GATE_COLS
COL_Q_END = COL_IF_END + MW_A
COL_K_END = COL_Q_END + MW_A
COL_V_END = COL_K_END + MW_A
LOG_K_SCALE = math.log(DH_M ** -0.5)
Q_SCALE = DH_A ** -0.5
LOG2E = math.log2(math.e)

N_NEAR = (REL_MAX_DIST + MOBA_BLOCK - 1) // MOBA_BLOCK + 1
assert (N_NEAR * MOBA_BLOCK - (MOBA_BLOCK - 1)) >= REL_MAX_DIST
TILE_ZERO = N_NEAR
TILE_MASKED = N_NEAR + 1
N_TILES = N_NEAR + 2
MOBA_GROUP = 8

PAIR = 2 * DH_A
K_AUG = 2 * PAIR
SEL_HI, SEL_LO = PAIR, PAIR + 32
V_ROWS = DH_A + 16
MAX_BLOCKS = 32

VMEM_LIMIT = 56 * 1024 * 1024

TM_PROJ = 512
TM_FFN = 512
L_CHUNK = 256
FF_CHUNK = 1408
PAGES_PER_STEP = 16
ATTEND_HEADS = 2
MLSTM_SAMPLES_PER_STEP = 1


def _params(*sem):
    return pltpu.CompilerParams(dimension_semantics=sem, vmem_limit_bytes=VMEM_LIMIT)


def _rms(x, g):
    return x * lax.rsqrt(jnp.mean(x * x, axis=-1, keepdims=True) + EPS) * g


def _log_sigmoid(x):
    return jnp.minimum(x, 0.0) - jnp.log(1.0 + jnp.exp(-jnp.abs(x)))


def _dot(a, b, precision=None):
    return jnp.dot(a, b, precision=precision, preferred_element_type=F32)


def _dot_nt(a, b, precision=None):
    return lax.dot_general(a, b, (((1,), (1,)), ((), ())), precision=precision,
                           preferred_element_type=F32)


def _dot_tn(a, b):
    return lax.dot_general(a, b, (((0,), (0,)), ((), ())), preferred_element_type=F32)


def _inproj_prompt_kernel(x_ref, g_ref, wm_ref, wif_ref, wq_ref, wk_ref, wv_ref, wg_ref, *refs):
    (m_ref, gc_ref, gr_ref, qt_ref, ka_ref, kt32_ref, vt32_ref, vte_ref, km_ref, sg_ref) = refs[-10:]
    tm = x_ref.shape[0]
    xn = _rms(x_ref[...], g_ref[...])
    xb = xn.astype(BF16)
    m_ref[...] = _dot_nt(xb, wm_ref[...]).astype(BF16)
    gates = jnp.concatenate(
        [jnp.sum(xn * wif_ref[c:c + 1, :], axis=1, keepdims=True) for c in range(N_GATE_COLS)],
        axis=1)
    gc_ref[...] = gates
    gr_ref[...] = gates.T
    qt_ref[...] = (_dot_nt(wq_ref[...], xb) * (Q_SCALE * LOG2E)).astype(BF16)

    k = _dot_nt(xb, wk_ref[...])
    kt = k.T
    for d in range(kt32_ref.shape[0]):
        kt32_ref[d, 0] = kt
    km_ref[0] = jnp.mean(k.reshape(tm // MOBA_BLOCK, MOBA_BLOCK, MW_A), axis=1)
    row = lax.broadcasted_iota(jnp.int32, (tm, PAIR), 0)
    lane = lax.broadcasted_iota(jnp.int32, (tm, PAIR), 1)
    blk = (pl.program_id(1) * tm + row) // MOBA_BLOCK
    onehot = jnp.where(jnp.logical_and(lane < 2 * MAX_BLOCKS, lane % MAX_BLOCKS == blk),
                       1.0, 0.0).astype(BF16)
    kb = k.astype(BF16)
    for p in range(H_A // 2):
        ka_ref[:, p * K_AUG:p * K_AUG + PAIR] = kb[:, p * PAIR:(p + 1) * PAIR]
        ka_ref[:, p * K_AUG + PAIR:(p + 1) * K_AUG] = onehot

    vt = _dot_nt(wv_ref[...], xb)
    for d in range(vt32_ref.shape[0]):
        vt32_ref[d, 0] = vt
    ones_rows = jnp.where(lax.broadcasted_iota(jnp.int32, (V_ROWS - DH_A, tm), 0) == 0,
                          1.0, 0.0).astype(BF16)
    for h in range(H_A):
        vte_ref[h * V_ROWS:h * V_ROWS + DH_A, :] = vt[h * DH_A:(h + 1) * DH_A].astype(BF16)
        vte_ref[h * V_ROWS + DH_A:(h + 1) * V_ROWS, :] = ones_rows
    sg_ref[...] = jax.nn.sigmoid(_dot_nt(xb, wg_ref[...])).astype(BF16)


def _inproj_prompt(x, g, wm, wif, wq, wk, wv, wg, batch, layer, depth, kv_prev):
    m = x.shape[0]
    seq = m // batch
    tm = TM_PROJ
    nt = seq // tm
    n_pairs = H_A // 2
    row = lambda w: pl.BlockSpec((tm, w), lambda b, t: (b * nt + t, 0))
    col = lambda h: pl.BlockSpec((h, tm), lambda b, t: (0, b * nt + t))
    full = lambda a: pl.BlockSpec(a.shape, lambda b, t: (0,) * a.ndim)
    kv_spec = (pl.BlockSpec((depth, 1, MW_A, tm), lambda b, t: (0, b, 0, t)) if kv_prev is None
               else pl.BlockSpec((1, 1, MW_A, tm), lambda b, t: (layer, b, 0, t)))
    operands = [x, g, wm, wif, wq, wk, wv, wg]
    in_specs = [row(D_MODEL), full(g), full(wm), full(wif), full(wq), full(wk), full(wv), full(wg)]
    aliases = {}
    if kv_prev is not None:
        aliases = {len(operands): 5, len(operands) + 1: 6}
        operands += list(kv_prev)
        in_specs += [pl.BlockSpec(memory_space=pl.ANY)] * 2
    return pl.pallas_call(
        _inproj_prompt_kernel,
        grid=(batch, nt),
        in_specs=in_specs,
        input_output_aliases=aliases,
        out_specs=[row(COL_M_END), row(N_GATE_COLS), col(N_GATE_COLS), col(MW_A),
                   row(n_pairs * K_AUG), kv_spec, kv_spec,
                   col(H_A * V_ROWS),
                   pl.BlockSpec((1, tm // MOBA_BLOCK, MW_A), lambda b, t: (b * nt + t, 0, 0)),
                   row(2 * D_MODEL)],
        out_shape=[jax.ShapeDtypeStruct((m, COL_M_END), BF16),
                   jax.ShapeDtypeStruct((m, N_GATE_COLS), F32),
                   jax.ShapeDtypeStruct((N_GATE_COLS, m), F32),
                   jax.ShapeDtypeStruct((MW_A, m), BF16),
                   jax.ShapeDtypeStruct((m, n_pairs * K_AUG), BF16),
                   jax.ShapeDtypeStruct((depth, batch, MW_A, seq), F32),
                   jax.ShapeDtypeStruct((depth, batch, MW_A, seq), F32),
                   jax.ShapeDtypeStruct((H_A * V_ROWS, m), BF16),
                   jax.ShapeDtypeStruct((m // tm, tm // MOBA_BLOCK, MW_A), F32),
                   jax.ShapeDtypeStruct((m, 2 * D_MODEL), BF16)],
        compiler_params=_params("parallel", "parallel"),
        name="inproj_prompt",
    )(*operands)


def _inproj_sample_kernel(x_ref, g_ref, wm_ref, wif_ref, wq_ref, wk_ref, wv_ref, wg_ref,
                          m_ref, mt_ref, gc_ref, qkv_ref, qt_ref, sg_ref):
    xn = _rms(x_ref[...], g_ref[...])
    xb = xn.astype(BF16)
    m_ref[...] = _dot_nt(xb, wm_ref[...])
    mt_ref[...] = _dot_nt(wm_ref[:2 * MW_M, :], xb)
    gc_ref[...] = _dot_nt(xn, wif_ref[...], HIGHEST)
    qkv_ref[0] = _dot_nt(xb, wq_ref[...])
    qkv_ref[1] = _dot_nt(xb, wk_ref[...])
    qkv_ref[2] = _dot_nt(xb, wv_ref[...])
    qt_ref[...] = _dot_nt(wq_ref[...], xb)
    sg_ref[...] = jax.nn.sigmoid(_dot_nt(xb, wg_ref[...])).astype(BF16)


def _inproj_sample(x, g, wm, wif, wq, wk, wv, wg):
    m = x.shape[0]
    return pl.pallas_call(
        _inproj_sample_kernel,
        out_shape=[jax.ShapeDtypeStruct((m, COL_M_END), F32),
                   jax.ShapeDtypeStruct((2 * MW_M, m), F32),
                   jax.ShapeDtypeStruct((m, N_GATE_COLS), F32),
                   jax.ShapeDtypeStruct((3, m, MW_A), F32),
                   jax.ShapeDtypeStruct((MW_A, m), F32),
                   jax.ShapeDtypeStruct((m, 2 * D_MODEL), BF16)],
        compiler_params=pltpu.CompilerParams(vmem_limit_bytes=VMEM_LIMIT),
        name="inproj_sample",
    )(x, g, wm, wif, wq, wk, wv, wg)


def _mlstm_prompt_kernel(m_ref, gc_ref, gr_ref, bc_ref, br_ref, gm_ref,
                         h_ref, c_ref, ms_ref):
    L = m_ref.shape[0]

    @pl.when(pl.program_id(1) == 0)
    def _():
        c_ref[...] = jnp.zeros_like(c_ref)
        ms_ref[...] = jnp.zeros_like(ms_ref)

    row = lax.broadcasted_iota(jnp.int32, (L, L), 0)
    col = lax.broadcasted_iota(jnp.int32, (L, L), 1)
    causal = col <= row
    lower = causal.astype(F32)
    upper = (row <= col).astype(F32)

    gcol = gc_ref[...] + br_ref[...]
    grow = gr_ref[...] + bc_ref[...]
    bcol = _dot(lower, _log_sigmoid(gcol), HIGHEST)
    brow = _dot(_log_sigmoid(grow), upper, HIGHEST)

    lane = lax.broadcasted_iota(jnp.int32, (L, DH_M), 1)
    ones_blk = jnp.where(lane == 0, 1.0, 0.0).astype(BF16)

    def operands(h):
        return (m_ref[:, h * DH_M:(h + 1) * DH_M],
                m_ref[:, MW_M + h * DH_M:MW_M + (h + 1) * DH_M],
                m_ref[:, 2 * MW_M + h * DH_M:2 * MW_M + (h + 1) * DH_M])

    qk_all = [_dot_nt(operands(h)[0], operands(h)[1]) for h in range(H_M)]
    qc_all = [_dot(operands(h)[0], c_ref[0, h].astype(BF16)) for h in range(H_M)]

    heads = range(H_M)
    i_col = [gcol[:, h:h + 1] for h in heads]
    b_col = [bcol[:, H_M + h:H_M + h + 1] for h in heads]
    i_row = [grow[h:h + 1, :] for h in heads]
    b_row = [brow[H_M + h:H_M + h + 1, :] for h in heads]
    b_last = [b_col[h][L - 1:L, :] for h in heads]
    m_prev = [ms_ref[0, h][0:1, 0:1] for h in heads]
    v_ext = [jnp.concatenate([operands(h)[2], ones_blk], axis=1) for h in heads]

    mt, a, sv = [], [], []
    for h in heads:
        u_row = i_row[h] - b_row[h]
        u_max = jnp.max(jnp.where(causal, u_row, NEG), axis=1, keepdims=True)
        inter = b_col[h] + m_prev[h]
        mt.append(jnp.maximum(inter, b_col[h] + u_max))
        a.append(jnp.exp(inter - mt[h]))
        expo = jnp.where(causal, (b_col[h] - mt[h] + LOG_K_SCALE) + u_row, NEG)
        s = qk_all[h] * jnp.exp(expo)
        sv.append(_dot(s.astype(BF16), v_ext[h]))

    for h in heads:
        k = operands(h)[1]
        g_row = b_last[h] - b_row[h] + i_row[h]
        e = b_last[h] + m_prev[h]
        m_new = jnp.maximum(e, jnp.max(g_row, axis=1, keepdims=True))
        w_col = jnp.exp(b_last[h] - b_col[h] + i_col[h] - (m_new - LOG_K_SCALE))
        kw = (k.astype(F32) * w_col).astype(BF16)
        c_ref[0, h] = jnp.exp(e - m_new) * c_ref[0, h] + _dot_tn(kw, v_ext[h])
        ms_ref[0, h] = jnp.broadcast_to(m_new, ms_ref.shape[2:])

    for h in heads:
        o = m_ref[:, 3 * MW_M + h * DH_M:3 * MW_M + (h + 1) * DH_M]
        nd = a[h] * qc_all[h] + sv[h]
        num = nd[:, :DH_M]
        denom = jnp.maximum(jnp.abs(nd[:, DH_M:DH_M + 1]), jnp.exp(-mt[h]))
        nc = num - jnp.mean(num, axis=1, keepdims=True)
        var = jnp.mean(nc * nc, axis=1, keepdims=True)
        hn = nc * lax.rsqrt(var + EPS * denom * denom) * gm_ref[:, h * DH_M:(h + 1) * DH_M]
        h_ref[:, h * DH_M:(h + 1) * DH_M] = (hn * jax.nn.sigmoid(o.astype(F32))).astype(BF16)


def _mlstm_prompt(mproj, gcol, grow, b_if, g_mlstm, batch):
    m = mproj.shape[0]
    L = L_CHUNK
    nc = m // batch // L
    bias_row = b_if.reshape(1, N_GATE_COLS)
    bias_col = b_if.reshape(N_GATE_COLS, 1)
    gm = g_mlstm.reshape(1, MW_M)
    return pl.pallas_call(
        _mlstm_prompt_kernel,
        grid=(batch, nc),
        in_specs=[pl.BlockSpec((L, COL_M_END), lambda b, c: (b * nc + c, 0)),
                  pl.BlockSpec((L, N_GATE_COLS), lambda b, c: (b * nc + c, 0)),
                  pl.BlockSpec((N_GATE_COLS, L), lambda b, c: (0, b * nc + c)),
                  pl.BlockSpec((N_GATE_COLS, 1), lambda b, c: (0, 0)),
                  pl.BlockSpec((1, N_GATE_COLS), lambda b, c: (0, 0)),
                  pl.BlockSpec((1, MW_M), lambda b, c: (0, 0))],
        out_specs=[pl.BlockSpec((L, MW_M), lambda b, c: (b * nc + c, 0)),
                   pl.BlockSpec((1, H_M, DH_M, 2 * DH_M), lambda b, c: (b, 0, 0, 0)),
                   pl.BlockSpec((1, H_M, 8, 128), lambda b, c: (b, 0, 0, 0))],
        out_shape=[jax.ShapeDtypeStruct((m, MW_M), BF16),
                   jax.ShapeDtypeStruct((batch, H_M, DH_M, 2 * DH_M), F32),
                   jax.ShapeDtypeStruct((batch, H_M, 8, 128), F32)],
        compiler_params=_params("parallel", "arbitrary"),
        name="mlstm_prompt",
    )(mproj, gcol, grow, bias_col, bias_row, gm)


def _mlstm_sample_kernel(m_ref, mt_ref, gc_ref, bi_ref, gm_ref, c_ref, n_ref, ms_ref,
                         h_ref, co_ref, no_ref, mo_ref):
    nb = mt_ref.shape[1]
    for j in range(m_ref.shape[0]):
        b = pl.program_id(0) * m_ref.shape[0] + j
        onehot = lax.broadcasted_iota(jnp.int32, (1, nb), 1) == b
        g = gc_ref[j] + bi_ref[...]
        for h in range(H_M):
            sl = slice(h * DH_M, (h + 1) * DH_M)
            q_row = m_ref[j, :, h * DH_M:(h + 1) * DH_M]
            k_row = m_ref[j, :, MW_M + h * DH_M:MW_M + (h + 1) * DH_M]
            v_row = m_ref[j, :, 2 * MW_M + h * DH_M:2 * MW_M + (h + 1) * DH_M]
            o_row = m_ref[j, :, 3 * MW_M + h * DH_M:3 * MW_M + (h + 1) * DH_M]
            q_col = jnp.sum(jnp.where(onehot, mt_ref[sl, :], 0.0), axis=1, keepdims=True)
            k_col = jnp.sum(jnp.where(onehot, mt_ref[MW_M + h * DH_M:MW_M + (h + 1) * DH_M, :],
                                      0.0), axis=1, keepdims=True)
            i_pre = g[:, h:h + 1]
            log_f = _log_sigmoid(g[:, H_M + h:H_M + h + 1])
            m_prev = ms_ref[j, :, h:h + 1]
            inter = log_f + m_prev
            m_new = jnp.maximum(inter, i_pre)
            a = jnp.exp(inter - m_new)
            w = jnp.exp(i_pre - (m_new - LOG_K_SCALE))
            c = c_ref[j, h]
            n = n_ref[j, h:h + 1, :]
            s = jnp.sum(q_row * k_row, axis=1, keepdims=True) * w
            num = a * jnp.sum(c * q_col, axis=0, keepdims=True) + s * v_row
            den = a * jnp.sum(q_row * n, axis=1, keepdims=True) + s
            hh = num / jnp.maximum(jnp.abs(den), jnp.exp(-m_new))
            mu = jnp.mean(hh, axis=1, keepdims=True)
            hc = hh - mu
            var = jnp.mean(hc * hc, axis=1, keepdims=True)
            hn = hc * lax.rsqrt(var + EPS) * gm_ref[:, sl]
            h_ref[j, :, sl] = (hn * jax.nn.sigmoid(o_row)).astype(BF16)
            co_ref[j, h] = a * c + (w * k_col) * v_row
            no_ref[j, h:h + 1, :] = a * n + w * k_row
            mo_ref[j, :, h:h + 1] = m_new


def _mlstm_sample(mproj, mproj_t, gcol, b_if, g_mlstm, c0, n0, m0):
    nb = mproj.shape[0]
    ns = MLSTM_SAMPLES_PER_STEP
    return pl.pallas_call(
        _mlstm_sample_kernel,
        grid=(nb // ns,),
        in_specs=[pl.BlockSpec((ns, 1, COL_M_END), lambda b: (b, 0, 0)),
                  pl.BlockSpec((2 * MW_M, nb), lambda b: (0, 0)),
                  pl.BlockSpec((ns, 1, N_GATE_COLS), lambda b: (b, 0, 0)),
                  pl.BlockSpec((1, N_GATE_COLS), lambda b: (0, 0)),
                  pl.BlockSpec((1, MW_M), lambda b: (0, 0)),
                  pl.BlockSpec((ns, H_M, DH_M, DH_M), lambda b: (b, 0, 0, 0)),
                  pl.BlockSpec((ns, H_M, DH_M), lambda b: (b, 0, 0)),
                  pl.BlockSpec((ns, 1, H_M), lambda b: (b, 0, 0))],
        out_specs=[pl.BlockSpec((ns, 1, MW_M), lambda b: (b, 0, 0)),
                   pl.BlockSpec((ns, H_M, DH_M, DH_M), lambda b: (b, 0, 0, 0)),
                   pl.BlockSpec((ns, H_M, DH_M), lambda b: (b, 0, 0)),
                   pl.BlockSpec((ns, 1, H_M), lambda b: (b, 0, 0))],
        out_shape=[jax.ShapeDtypeStruct((nb, 1, MW_M), BF16),
                   jax.ShapeDtypeStruct((nb, H_M, DH_M, DH_M), F32),
                   jax.ShapeDtypeStruct((nb, H_M, DH_M), F32),
                   jax.ShapeDtypeStruct((nb, 1, H_M), F32)],
        compiler_params=_params("parallel"),
        name="mlstm_sample",
    )(mproj.reshape(nb, 1, COL_M_END), mproj_t, gcol.reshape(nb, 1, N_GATE_COLS),
      b_if.reshape(1, N_GATE_COLS), g_mlstm.reshape(1, MW_M), c0, n0, m0.reshape(nb, 1, H_M))


def _t5_bucket(dist):
    n = jnp.maximum(dist, 0)
    max_exact = N_BUCKETS // 2
    nf = jnp.maximum(n, 1).astype(F32)
    large = max_exact + (jnp.log(nf / max_exact) / math.log(REL_MAX_DIST / max_exact)
                         * (N_BUCKETS - max_exact)).astype(jnp.int32)
    large = jnp.minimum(large, N_BUCKETS - 1)
    return jnp.where(n < max_exact, n, large)


def _t5_bias(dist, rel_ref, h):
    bucket = _t5_bucket(dist)
    bias = jnp.zeros(dist.shape, F32)
    for kb in range(N_BUCKETS):
        bias = jnp.where(bucket == kb, rel_ref[kb, h], bias)
    return bias


def _bias_table_kernel(rel_ref, o_ref):
    delta = pl.program_id(0)
    c = lax.broadcasted_iota(jnp.int32, (MOBA_BLOCK, MOBA_BLOCK), 0)
    r = lax.broadcasted_iota(jnp.int32, (MOBA_BLOCK, MOBA_BLOCK), 1)
    dist = delta * MOBA_BLOCK + r - c
    bucket = _t5_bucket(dist)
    bits = [(bucket >> i) & 1 == 1 for i in range(N_BUCKETS.bit_length() - 1)]
    for h in range(H_A):
        level = [rel_ref[kb, h] * LOG2E for kb in range(N_BUCKETS)]
        for bit in bits:
            level = [jnp.where(bit, level[2 * i + 1], level[2 * i]) for i in range(len(level) // 2)]
        o_ref[h, 0] = jnp.where(dist >= 0, level[0], NEG)


def _bias_table(rel_bias):
    return pl.pallas_call(
        _bias_table_kernel,
        grid=(N_NEAR,),
        in_specs=[pl.BlockSpec(memory_space=pltpu.SMEM)],
        out_specs=pl.BlockSpec((H_A, 1, MOBA_BLOCK, MOBA_BLOCK), lambda d: (0, d, 0, 0)),
        out_shape=jax.ShapeDtypeStruct((H_A, N_NEAR, MOBA_BLOCK, MOBA_BLOCK), F32),
        compiler_params=_params("parallel"),
        name="bias_table",
    )(rel_bias)


def _select_topk_t(score, n_valid_rows, k):
    nb = score.shape[0]
    row = lax.broadcasted_iota(jnp.int32, score.shape, 0)
    past = row < n_valid_rows
    sc = jnp.where(past, score, -jnp.inf)
    sel = jnp.full(score.shape, NEG, F32)
    for _ in range(k):
        mx = jnp.max(sc, axis=0, keepdims=True)
        idx = jnp.min(jnp.where(sc == mx, row, nb), axis=0, keepdims=True)
        hit = row == idx
        sel = jnp.where(hit, 0.0, sel)
        sc = jnp.where(hit, -jnp.inf, sc)
    return jnp.where(past, sel, NEG)


def _moba_prompt_kernel(qi_ref, grp_ref, first_ref, last_ref, far_item_ref, far_ref,
                        qt_ref, ka_ref, vte_ref, km_ref, tab_ref, o_ref,
                        qp_scr, m_scr, acc_scr, sa_scr, sb_scr, gma_scr, gmb_scr):
    hp = pl.program_id(1)
    seq = qt_ref.shape[1]
    nb = km_ref.shape[1]
    tq = MOBA_BLOCK
    n_items = qi_ref.shape[0] - 1
    setup_w = 4 * MOBA_BLOCK

    def setup(c, carry):
        cols = pl.ds(pl.multiple_of(c * setup_w, setup_w), setup_w)
        blk_row = lax.broadcasted_iota(jnp.int32, (nb, setup_w), 0)
        q_blk = (c * setup_w + lax.broadcasted_iota(jnp.int32, (1, setup_w), 1)) // MOBA_BLOCK
        sub = lax.broadcasted_iota(jnp.int32, (PAIR, setup_w), 0)
        qpair = qt_ref[:, cols]
        for h in range(2):
            qth = qt_ref[h * DH_A:(h + 1) * DH_A, cols].astype(F32)
            score = _dot(km_ref[0, :, h * DH_A:(h + 1) * DH_A], qth, HIGHEST)
            sel = _select_topk_t(score, q_blk, MOBA_TOPK)
            sel = jnp.where(blk_row == q_blk, 0.0, sel)
            sel = sel + jnp.where(q_blk - blk_row >= N_NEAR, far_ref[2 * hp + h], 0.0)
            hi = sel.astype(BF16)
            qp_scr[h, 0:PAIR, cols] = jnp.where(sub // DH_A == h, qpair, jnp.zeros_like(qpair))
            qp_scr[h, PAIR:, cols] = jnp.zeros((K_AUG - PAIR, setup_w), BF16)
            qp_scr[h, SEL_HI:SEL_HI + nb, cols] = hi
            qp_scr[h, SEL_LO:SEL_LO + nb, cols] = (sel - hi.astype(F32)).astype(BF16)
        return carry

    lax.fori_loop(0, seq // setup_w, setup, 0)
    m_scr[...] = jnp.full(m_scr.shape, NEG, F32)
    acc_scr[...] = jnp.zeros(acc_scr.shape, F32)

    gk = MOBA_GROUP * MOBA_BLOCK

    def score_block(w, t, h, far):
        qi = qi_ref[w]
        g = grp_ref[w]
        kblk = ka_ref[pl.ds(pl.multiple_of((g * MOBA_GROUP + t) * MOBA_BLOCK, MOBA_BLOCK),
                            MOBA_BLOCK), :]
        qcols = pl.ds(pl.multiple_of(qi * tq, tq), tq)
        s = _dot(kblk, qp_scr[h, :, qcols])
        if far:
            return s
        delta = qi - (g * MOBA_GROUP + t)
        tile = jnp.where(delta < 0, TILE_MASKED, jnp.minimum(delta, TILE_ZERO))
        return s + tab_ref[h, tile]

    def step(w, s_cur, gm_cur, s_nxt, gm_nxt, far):
        qi = qi_ref[w]
        start = pl.multiple_of(grp_ref[w] * gk, gk)
        first = first_ref[w] == 1
        m_old = [jnp.where(first, NEG, m_scr[h]) for h in range(2)]
        m_new = [jnp.maximum(m_old[h], gm_cur[h]) for h in range(2)]
        pv = [None, None]
        gmax = [None, None]

        def next_scores(t):
            rows = slice(t * MOBA_BLOCK, (t + 1) * MOBA_BLOCK)
            for h in range(2):
                s = score_block(w + 1, t, h, far)
                s_nxt[h, rows, :] = s
                cmax = jnp.max(s, axis=0, keepdims=True)
                gmax[h] = cmax if t == 0 else jnp.maximum(gmax[h], cmax)

        def attend(t):
            rows = slice(t * MOBA_BLOCK, (t + 1) * MOBA_BLOCK)
            for h in range(2):
                p = jnp.exp2(s_cur[h, rows, :] - m_new[h]).astype(BF16)
                vblk = vte_ref[h * V_ROWS:(h + 1) * V_ROWS,
                               pl.ds(start + t * MOBA_BLOCK, MOBA_BLOCK)]
                d = _dot(vblk, p)
                pv[h] = d if t == 0 else pv[h] + d

        for t in range(MOBA_GROUP):
            next_scores(t)
            attend(t)
        for h in range(2):
            acc_old = jnp.where(first, 0.0, acc_scr[h])
            acc_scr[h] = jnp.exp2(m_old[h] - m_new[h]) * acc_old + pv[h]
            m_scr[h] = m_new[h]
            gm_nxt[h] = gmax[h]

        @pl.when(last_ref[w] == 1)
        def _():
            outs = [acc_scr[h, 0:DH_A, :] / acc_scr[h, DH_A:DH_A + 1, :] for h in range(2)]
            orows = pl.ds(pl.multiple_of(qi * tq, tq), tq)
            o_ref[orows, :] = jnp.concatenate(outs, axis=0).T.astype(BF16)

    for h in range(2):
        gmax = None
        for t in range(MOBA_GROUP):
            s = score_block(0, t, h, False)
            sa_scr[h, t * MOBA_BLOCK:(t + 1) * MOBA_BLOCK, :] = s
            cmax = jnp.max(s, axis=0, keepdims=True)
            gmax = cmax if t == 0 else jnp.maximum(gmax, cmax)
        gma_scr[h] = gmax

    def step_by_kind(w, *buffers):
        next_is_far = far_item_ref[w + 1] == 1
        pl.when(next_is_far)(lambda: step(w, *buffers, True))
        pl.when(jnp.logical_not(next_is_far))(lambda: step(w, *buffers, False))

    def pair_of_items(i, carry):
        w = 2 * i
        step_by_kind(w, sa_scr, gma_scr, sb_scr, gmb_scr)
        step_by_kind(w + 1, sb_scr, gmb_scr, sa_scr, gma_scr)
        return carry

    lax.fori_loop(0, n_items // 2, pair_of_items, 0)


def _moba_work_items(nb):
    qi, grp, first, last, far = [], [], [], [], []
    for q in range(nb):
        n_groups = q // MOBA_GROUP + 1
        for g in range(n_groups):
            qi.append(q)
            grp.append(g)
            first.append(int(g == 0))
            last.append(int(g == n_groups - 1))
            far.append(int(q - (g * MOBA_GROUP + MOBA_GROUP - 1) >= N_NEAR))
    assert len(qi) % 2 == 0
    tables = [qi + qi[-1:], grp + grp[-1:], first + [0], last + [0], far + [0]]
    return [jnp.asarray(t, jnp.int32) for t in tables]


def _moba_prompt(qt, ka, vte, kmean, table, far, batch):
    m = qt.shape[1]
    seq = m // batch
    nb = seq // MOBA_BLOCK
    assert nb <= MAX_BLOCKS and nb % MOBA_GROUP == 0
    tq = MOBA_BLOCK
    n_pairs = H_A // 2
    gk = MOBA_GROUP * MOBA_BLOCK
    smem = pl.BlockSpec(memory_space=pltpu.SMEM)
    return pl.pallas_call(
        _moba_prompt_kernel,
        grid=(batch, n_pairs),
        in_specs=[smem, smem, smem, smem, smem, smem,
                  pl.BlockSpec((PAIR, seq), lambda b, hp: (hp, b)),
                  pl.BlockSpec((seq, K_AUG), lambda b, hp: (b, hp)),
                  pl.BlockSpec((2 * V_ROWS, seq), lambda b, hp: (hp, b)),
                  pl.BlockSpec((1, nb, PAIR), lambda b, hp: (b, 0, hp)),
                  pl.BlockSpec((2, N_TILES, tq, tq), lambda b, hp: (hp, 0, 0, 0),
                               pipeline_mode=pl.Buffered(1))],
        out_specs=pl.BlockSpec((seq, PAIR), lambda b, hp: (b, hp)),
        out_shape=jax.ShapeDtypeStruct((m, MW_A), BF16),
        scratch_shapes=[pltpu.VMEM((2, K_AUG, seq), BF16),
                        pltpu.VMEM((2, 1, tq), F32),
                        pltpu.VMEM((2, V_ROWS, tq), F32),
                        pltpu.VMEM((2, gk, tq), F32),
                        pltpu.VMEM((2, gk, tq), F32),
                        pltpu.VMEM((2, 1, tq), F32),
                        pltpu.VMEM((2, 1, tq), F32)],
        compiler_params=_params("parallel", "arbitrary"),
        name="moba_prompt",
    )(*_moba_work_items(nb), far, qt, ka, vte, kmean.reshape(batch, nb, MW_A), table)


def _logits_sample_kernel(pt_ref, qt_ref, *refs):
    o_ref, qb_scr = refs[-2], refs[-1]
    b = pl.program_id(0)
    nb_s = qt_ref.shape[1]

    @pl.when(pl.program_id(1) == 0)
    def _():
        onehot = lax.broadcasted_iota(jnp.int32, (1, nb_s), 1) == b
        qcol = jnp.sum(jnp.where(onehot, qt_ref[...], 0.0), axis=1, keepdims=True)
        qb_scr[...] = jnp.broadcast_to(qcol * Q_SCALE, qb_scr.shape)

    per_block = MOBA_BLOCK // PAGE_SIZE
    for i in range(PAGES_PER_STEP):
        for h in range(H_A):
            kt = refs[i][0, 0, h]
            lg = jnp.sum(kt * qb_scr[h * DH_A:(h + 1) * DH_A, :], axis=0, keepdims=True)
            r, half = i // per_block, i % per_block
            o_ref[0, h, r:r + 1, half * PAGE_SIZE:(half + 1) * PAGE_SIZE] = lg


def _logits_sample(cache_kt, page_table, qt, layer):
    nb_s, n_pages = page_table.shape
    steps = n_pages // PAGES_PER_STEP
    rows = PAGES_PER_STEP * PAGE_SIZE // MOBA_BLOCK

    def page_spec(i):
        return pl.BlockSpec((1, 1, H_A, DH_A, PAGE_SIZE),
                            lambda b, s, pt: (layer, pt[b, s * PAGES_PER_STEP + i], 0, 0, 0))

    return pl.pallas_call(
        _logits_sample_kernel,
        grid_spec=pltpu.PrefetchScalarGridSpec(
            num_scalar_prefetch=1,
            grid=(nb_s, steps),
            in_specs=[pl.BlockSpec(qt.shape, lambda b, s, pt: (0, 0))]
                     + [page_spec(i) for i in range(PAGES_PER_STEP)],
            out_specs=pl.BlockSpec((1, H_A, rows, MOBA_BLOCK), lambda b, s, pt: (b, 0, s, 0)),
            scratch_shapes=[pltpu.VMEM((MW_A, PAGE_SIZE), F32)]),
        out_shape=jax.ShapeDtypeStruct((nb_s, H_A, steps * rows, MOBA_BLOCK), F32),
        compiler_params=_params("parallel", "arbitrary"),
        name="logits_sample",
    )(page_table, qt, *([cache_kt] * PAGES_PER_STEP))


def _select_sample_kernel(lg_ref, o_ref):
    sc = jnp.sum(lg_ref[...], axis=-1)
    nblk = sc.shape[-1]
    blk = lax.broadcasted_iota(jnp.int32, sc.shape, 2)
    for t in range(MOBA_TOPK):
        mx = jnp.max(sc, axis=2, keepdims=True)
        idx = jnp.min(jnp.where(sc == mx, blk, nblk), axis=2, keepdims=True)
        o_ref[:, :, t:t + 1] = idx
        sc = jnp.where(blk == idx, -jnp.inf, sc)


def _select_sample(logits):
    nb_s = logits.shape[0]
    return pl.pallas_call(
        _select_sample_kernel,
        out_shape=jax.ShapeDtypeStruct((nb_s, H_A, MOBA_TOPK), jnp.int32),
        compiler_params=pltpu.CompilerParams(vmem_limit_bytes=VMEM_LIMIT),
        name="select_sample",
    )(logits)


def _bias_sample_kernel(rel_ref, o_ref, *, past_len):
    h = pl.program_id(0)
    nblk = o_ref.shape[1]
    kpos = (lax.broadcasted_iota(jnp.int32, (nblk, MOBA_BLOCK), 0) * MOBA_BLOCK
            + lax.broadcasted_iota(jnp.int32, (nblk, MOBA_BLOCK), 1))
    o_ref[0] = _t5_bias(past_len - kpos, rel_ref, h)


def _bias_sample(rel_bias, past_len):
    nblk = past_len // MOBA_BLOCK
    return pl.pallas_call(
        functools.partial(_bias_sample_kernel, past_len=past_len),
        grid=(H_A,),
        in_specs=[pl.BlockSpec(memory_space=pltpu.SMEM)],
        out_specs=pl.BlockSpec((1, nblk, MOBA_BLOCK), lambda h: (h, 0, 0)),
        out_shape=jax.ShapeDtypeStruct((H_A, nblk, MOBA_BLOCK), F32),
        compiler_params=_params("parallel"),
        name="bias_sample",
    )(rel_bias)


def _attend_sample_kernel(pt_ref, sel_ref, rel_ref, lg_ref, bias_ref, qkv_ref, *refs):
    o_ref = refs[-1]
    v_refs = refs[:-1]
    per_block = MOBA_BLOCK // PAGE_SIZE
    n_sel_pages = MOBA_TOPK * per_block
    b = pl.program_id(0)

    for j in range(ATTEND_HEADS):
        h = pl.program_id(1) * ATTEND_HEADS + j
        q = qkv_ref[0, 0, pl.ds(h, 1), :] * Q_SCALE
        k_new = qkv_ref[1, 0, pl.ds(h, 1), :]
        v_new = qkv_ref[2, 0, pl.ds(h, 1), :]
        own = jnp.sum(k_new * q, axis=1, keepdims=True) + rel_ref[0, h]
        logits = []
        mx = own
        for t in range(MOBA_TOPK):
            blk = sel_ref[b, h, t]
            lg = lg_ref[0, j, pl.ds(blk, 1), :] + bias_ref[j, pl.ds(blk, 1), :]
            logits.append(lg)
            mx = jnp.maximum(mx, jnp.max(lg, axis=1, keepdims=True))
        p_own = jnp.exp(own - mx)
        den = p_own
        acc = p_own * v_new
        for t in range(MOBA_TOPK):
            p = jnp.exp(logits[t] - mx)
            den = den + jnp.sum(p, axis=1, keepdims=True)
            pb = p.astype(BF16)
            for i in range(per_block):
                vt = v_refs[j * n_sel_pages + t * per_block + i][0, 0, 0].astype(BF16)
                acc = acc + _dot_nt(pb[:, i * PAGE_SIZE:(i + 1) * PAGE_SIZE], vt)
        o_ref[0, pl.ds(h, 1), :] = acc / den


def _attend_sample(cache_vt, page_table, sel, rel_bias, bias_s, logits, qkv, layer):
    nb_s = page_table.shape[0]
    per_block = MOBA_BLOCK // PAGE_SIZE
    n_sel_pages = MOBA_TOPK * per_block
    nblk = logits.shape[2]
    nh = ATTEND_HEADS

    def page_spec(j, t):
        def imap(b, hg, pt, sl, rel):
            h = hg * nh + j
            page = sl[b, h, t // per_block] * per_block + (t % per_block)
            return (layer, pt[b, page], h, 0, 0)
        return pl.BlockSpec((1, 1, 1, DH_A, PAGE_SIZE), imap)

    pages = [page_spec(j, t) for j in range(nh) for t in range(n_sel_pages)]
    return pl.pallas_call(
        _attend_sample_kernel,
        grid_spec=pltpu.PrefetchScalarGridSpec(
            num_scalar_prefetch=3,
            grid=(nb_s, H_A // nh),
            in_specs=[pl.BlockSpec((1, nh, nblk, MOBA_BLOCK), lambda b, hg, pt, sl, rel: (b, hg, 0, 0)),
                      pl.BlockSpec((nh, nblk, MOBA_BLOCK), lambda b, hg, pt, sl, rel: (hg, 0, 0)),
                      pl.BlockSpec((3, 1, H_A, DH_A), lambda b, hg, pt, sl, rel: (0, b, 0, 0))]
                     + pages,
            out_specs=pl.BlockSpec((1, H_A, DH_A), lambda b, hg, pt, sl, rel: (b, 0, 0))),
        out_shape=jax.ShapeDtypeStruct((nb_s, H_A, DH_A), F32),
        compiler_params=_params("parallel", "arbitrary"),
        name="attend_sample",
    )(page_table, sel, rel_bias, logits, bias_s, qkv, *([cache_vt] * len(pages)))


def _mix_ffn_kernel(x_ref, hm_ref, ha_ref, sg_ref, wbm_ref, wba_ref, wo_ref, gmix_ref,
                    gpre_ref, gpost_ref, wg_ref, wu_ref, wd_ref, o_ref):
    sg = sg_ref[...].astype(F32)
    mixed = (sg[:, :D_MODEL] * _dot(hm_ref[...], wbm_ref[...])
             + sg[:, D_MODEL:] * _dot(ha_ref[...], wba_ref[...]))
    x = x_ref[...] + _rms(_dot(mixed.astype(BF16), wo_ref[...]), gmix_ref[...])
    hf = _rms(x, gpre_ref[...]).astype(BF16)
    acc = jnp.zeros(x.shape, F32)
    for c in range(D_FF // FF_CHUNK):
        sl = slice(c * FF_CHUNK, (c + 1) * FF_CHUNK)
        gate = _dot(hf, wg_ref[:, sl])
        up = _dot(hf, wu_ref[:, sl])
        mid = (gate * jax.nn.sigmoid(gate) * up).astype(BF16)
        acc = acc + _dot(mid, wd_ref[sl, :])
    o_ref[...] = x + _rms(acc, gpost_ref[...])


def _mix_ffn(x, hm, ha, sg, wbm, wba, wo, gmix, gpre, gpost, wg, wu, wd):
    m = x.shape[0]
    tm = min(TM_FFN, m)
    row = lambda w: pl.BlockSpec((tm, w), lambda i: (i, 0))
    full = lambda a: pl.BlockSpec(a.shape, lambda i: (0,) * a.ndim, pipeline_mode=pl.Buffered(1))
    weights = (wbm, wba, wo, gmix, gpre, gpost, wg, wu, wd)
    return pl.pallas_call(
        _mix_ffn_kernel,
        grid=(m // tm,),
        in_specs=[row(D_MODEL), row(MW_M), row(MW_A), row(2 * D_MODEL)] + [full(w) for w in weights],
        out_specs=row(D_MODEL),
        out_shape=jax.ShapeDtypeStruct((m, D_MODEL), F32),
        compiler_params=_params("parallel"),
        name="mix_ffn",
    )(x, hm, ha, sg, *weights)


def kernel(x_prompt, x_sample, cache_k, cache_v, state_C, state_n, state_m, page_table,
           norm_mix_pre, norm_mix_post, norm_ffn_pre, norm_ffn_post, w_in, b_igate, b_fgate,
           g_mlstm, w_branch_m, w_branch_a, w_out, w_ffn_gate, w_ffn_up, w_ffn_down, rel_bias):
    bp, seq, _ = x_prompt.shape
    bs = x_sample.shape[0]
    depth = w_in.shape[0]
    xp = x_prompt.reshape(bp * seq, D_MODEL)
    xs = x_sample.reshape(bs, D_MODEL)
    cache_kt = jnp.transpose(cache_k, (0, 1, 3, 4, 2))
    cache_vt = jnp.transpose(cache_v, (0, 1, 3, 4, 2))
    w_in_t = jnp.swapaxes(w_in, 1, 2)

    tile = (H_A, 1, MOBA_BLOCK, MOBA_BLOCK)
    table = jnp.concatenate([_bias_table(rel_bias), jnp.zeros(tile, F32), jnp.full(tile, NEG, F32)],
                            axis=1)
    far = rel_bias[N_BUCKETS - 1] * LOG2E
    bias_s = _bias_sample(rel_bias, page_table.shape[1] * PAGE_SIZE)

    outs = {k: [] for k in ("cp", "np", "mp", "ks", "vs", "cs", "ns", "ms")}
    kv_prompt = None
    for l in range(depth):
        row = lambda a: a[l].reshape(1, -1)
        wt = w_in_t[l]
        wm = wt[:COL_M_END].astype(BF16)
        wif = wt[COL_M_END:COL_IF_END]
        wq = wt[COL_IF_END:COL_Q_END].astype(BF16)
        wk = wt[COL_Q_END:COL_K_END].astype(BF16)
        wv = wt[COL_K_END:COL_V_END].astype(BF16)
        wgt = wt[COL_V_END:].astype(BF16)
        b_if = jnp.concatenate([b_igate[l], b_fgate[l]])
        wbm, wba, wo = (w_branch_m[l].astype(BF16), w_branch_a[l].astype(BF16), w_out[l].astype(BF16))
        wfg, wfu, wfd = (w_ffn_gate[l].astype(BF16), w_ffn_up[l].astype(BF16), w_ffn_down[l].astype(BF16))

        mproj, gcol, grow, qt, ka, kt32, vt32, vte, kmean, sg = _inproj_prompt(
            xp, row(norm_mix_pre), wm, wif, wq, wk, wv, wgt, bp, l, depth, kv_prompt)
        kv_prompt = (kt32, vt32)
        hm, cext, mstate = _mlstm_prompt(mproj, gcol, grow, b_if, g_mlstm[l], bp)
        ha = _moba_prompt(qt, ka, vte, kmean, table, far, bp)
        post = (wbm, wba, wo, row(norm_mix_post), row(norm_ffn_pre), row(norm_ffn_post),
                wfg, wfu, wfd)
        xp = _mix_ffn(xp, hm, ha, sg, *post)
        outs["cp"].append(cext[..., :DH_M])
        outs["np"].append(cext[..., DH_M])
        outs["mp"].append(mstate[:, :, 0, 0])

        mproj_s, mproj_t, gcol_s, qkv_s, qt_s, sg_s = _inproj_sample(
            xs, row(norm_mix_pre), wm, wif, wq, wk, wv, wgt)
        hm_s, c_s, n_s, m_s = _mlstm_sample(mproj_s, mproj_t, gcol_s, b_if, g_mlstm[l],
                                            state_C[l], state_n[l], state_m[l])
        logits_s = _logits_sample(cache_kt, page_table, qt_s, l)
        sel = _select_sample(logits_s)
        ha_s = _attend_sample(cache_vt, page_table, sel, rel_bias, bias_s, logits_s,
                              qkv_s.reshape(3, bs, H_A, DH_A), l)
        xs = _mix_ffn(xs, hm_s.reshape(bs, MW_M), ha_s.reshape(bs, MW_A).astype(BF16), sg_s, *post)
        outs["ks"].append(qkv_s[1].reshape(bs, 1, H_A, DH_A))
        outs["vs"].append(qkv_s[2].reshape(bs, 1, H_A, DH_A))
        outs["cs"].append(c_s)
        outs["ns"].append(n_s)
        outs["ms"].append(m_s.reshape(bs, H_M))

    st = lambda k: jnp.stack(outs[k])
    kv_out = lambda a: jnp.transpose(a.reshape(depth, bp, H_A, DH_A, seq), (0, 1, 4, 2, 3))
    return (xp.reshape(bp, seq, D_MODEL), xs.reshape(bs, 1, D_MODEL),
            kv_out(kv_prompt[0]), kv_out(kv_prompt[1]), st("cp"), st("np"), st("mp"),
            st("ks"), st("vs"), st("cs"), st("ns"), st("ms"))
```

```python
import functools
import math

import jax
import jax.numpy as jnp
from jax import lax
from jax.experimental import pallas as pl
from jax.experimental.pallas import tpu as pltpu

F32 = jnp.float32
BF16 = jnp.bfloat16
HIGHEST = lax.Precision.HIGHEST

D_MODEL = 1024
H_M, DH_M = 4, 128
MW_M = H_M * DH_M
H_A, DH_A = 8, 64
MW_A = H_A * DH_A
MOBA_BLOCK = 256
MOBA_TOPK = 3
N_BUCKETS = 32
REL_MAX_DIST = 2048
D_FF = 2816
EPS = 1e-6
NEG = -1e30
PAGE_SIZE = 128

N_GATE_COLS = 2 * H_M
COL_M_END = 4 * MW_M
COL_IF_END = COL_M_END + N_GATE_COLS
COL_Q_END = COL_IF_END + MW_A
COL_K_END = COL_Q_END + MW_A
COL_V_END = COL_K_END + MW_A
LOG_K_SCALE = math.log(DH_M ** -0.5)
Q_SCALE = DH_A ** -0.5
LOG2E = math.log2(math.e)

N_NEAR = (REL_MAX_DIST + MOBA_BLOCK - 1) // MOBA_BLOCK + 1
assert (N_NEAR * MOBA_BLOCK - (MOBA_BLOCK - 1)) >= REL_MAX_DIST
TILE_ZERO = N_NEAR
TILE_MASKED = N_NEAR + 1
N_TILES = N_NEAR + 2
MOBA_GROUP = 8

PAIR = 2 * DH_A
K_AUG = 2 * PAIR
SEL_HI, SEL_LO = PAIR, PAIR + 32
V_ROWS = DH_A + 16
MAX_BLOCKS = 32

VMEM_LIMIT = 56 * 1024 * 1024

TM_PROJ = 512
TM_FFN = 512
L_CHUNK = 256
FF_CHUNK = 1408
PAGES_PER_STEP = 16
ATTEND_HEADS = 2


def _params(*sem):
    return pltpu.CompilerParams(dimension_semantics=sem, vmem_limit_bytes=VMEM_LIMIT)


def _rms(x, g):
    return x * lax.rsqrt(jnp.mean(x * x, axis=-1, keepdims=True) + EPS) * g


def _log_sigmoid(x):
    return jnp.minimum(x, 0.0) - jnp.log(1.0 + jnp.exp(-jnp.abs(x)))


def _dot(a, b, precision=None):
    return jnp.dot(a, b, precision=precision, preferred_element_type=F32)


def _dot_nt(a, b, precision=None):
    return lax.dot_general(a, b, (((1,), (1,)), ((), ())), precision=precision,
                           preferred_element_type=F32)


def _dot_tn(a, b):
    return lax.dot_general(a, b, (((0,), (0,)), ((), ())), preferred_element_type=F32)


def _inproj_prompt_kernel(x_ref, g_ref, wm_ref, wif_ref, wq_ref, wk_ref, wv_ref, wg_ref, *refs):
    (m_ref, gc_ref, gr_ref, qt_ref, ka_ref, kt32_ref, vt32_ref, vte_ref, km_ref, sg_ref) = refs[-10:]
    tm = x_ref.shape[0]
    xn = _rms(x_ref[...], g_ref[...])
    xb = xn.astype(BF16)
    m_ref[...] = _dot_nt(xb, wm_ref[...]).astype(BF16)
    gates = jnp.concatenate(
        [jnp.sum(xn * wif_ref[c:c + 1, :], axis=1, keepdims=True) for c in range(N_GATE_COLS)],
        axis=1)
    gc_ref[...] = gates
    gr_ref[...] = gates.T
    qt_ref[...] = (_dot_nt(wq_ref[...], xb) * (Q_SCALE * LOG2E)).astype(BF16)

    k = _dot_nt(xb, wk_ref[...])
    kt = k.T
    for d in range(kt32_ref.shape[0]):
        kt32_ref[d, 0] = kt
    km_ref[0] = jnp.mean(k.reshape(tm // MOBA_BLOCK, MOBA_BLOCK, MW_A), axis=1)
    row = lax.broadcasted_iota(jnp.int32, (tm, PAIR), 0)
    lane = lax.broadcasted_iota(jnp.int32, (tm, PAIR), 1)
    blk = (pl.program_id(1) * tm + row) // MOBA_BLOCK
    onehot = jnp.where(jnp.logical_and(lane < 2 * MAX_BLOCKS, lane % MAX_BLOCKS == blk),
                       1.0, 0.0).astype(BF16)
    kb = k.astype(BF16)
    for p in range(H_A // 2):
        ka_ref[:, p * K_AUG:p * K_AUG + PAIR] = kb[:, p * PAIR:(p + 1) * PAIR]
        ka_ref[:, p * K_AUG + PAIR:(p + 1) * K_AUG] = onehot

    vt = _dot_nt(wv_ref[...], xb)
    for d in range(vt32_ref.shape[0]):
        vt32_ref[d, 0] = vt
    ones_rows = jnp.where(lax.broadcasted_iota(jnp.int32, (V_ROWS - DH_A, tm), 0) == 0,
                          1.0, 0.0).astype(BF16)
    for h in range(H_A):
        vte_ref[h * V_ROWS:h * V_ROWS + DH_A, :] = vt[h * DH_A:(h + 1) * DH_A].astype(BF16)
        vte_ref[h * V_ROWS + DH_A:(h + 1) * V_ROWS, :] = ones_rows
    sg_ref[...] = jax.nn.sigmoid(_dot_nt(xb, wg_ref[...])).astype(BF16)


def _inproj_prompt(x, g, wm, wif, wq, wk, wv, wg, batch, layer, depth, kv_prev):
    m = x.shape[0]
    seq = m // batch
    tm = TM_PROJ
    nt = seq // tm
    n_pairs = H_A // 2
    row = lambda w: pl.BlockSpec((tm, w), lambda b, t: (b * nt + t, 0))
    col = lambda h: pl.BlockSpec((h, tm), lambda b, t: (0, b * nt + t))
    full = lambda a: pl.BlockSpec(a.shape, lambda b, t: (0,) * a.ndim)
    kv_spec = (pl.BlockSpec((depth, 1, MW_A, tm), lambda b, t: (0, b, 0, t)) if kv_prev is None
               else pl.BlockSpec((1, 1, MW_A, tm), lambda b, t: (layer, b, 0, t)))
    operands = [x, g, wm, wif, wq, wk, wv, wg]
    in_specs = [row(D_MODEL), full(g), full(wm), full(wif), full(wq), full(wk), full(wv), full(wg)]
    aliases = {}
    if kv_prev is not None:
        aliases = {len(operands): 5, len(operands) + 1: 6}
        operands += list(kv_prev)
        in_specs += [pl.BlockSpec(memory_space=pl.ANY)] * 2
    return pl.pallas_call(
        _inproj_prompt_kernel,
        grid=(batch, nt),
        in_specs=in_specs,
        input_output_aliases=aliases,
        out_specs=[row(COL_M_END), row(N_GATE_COLS), col(N_GATE_COLS), col(MW_A),
                   row(n_pairs * K_AUG), kv_spec, kv_spec,
                   col(H_A * V_ROWS),
                   pl.BlockSpec((1, tm // MOBA_BLOCK, MW_A), lambda b, t: (b * nt + t, 0, 0)),
                   row(2 * D_MODEL)],
        out_shape=[jax.ShapeDtypeStruct((m, COL_M_END), BF16),
                   jax.ShapeDtypeStruct((m, N_GATE_COLS), F32),
                   jax.ShapeDtypeStruct((N_GATE_COLS, m), F32),
                   jax.ShapeDtypeStruct((MW_A, m), BF16),
                   jax.ShapeDtypeStruct((m, n_pairs * K_AUG), BF16),
                   jax.ShapeDtypeStruct((depth, batch, MW_A, seq), F32),
                   jax.ShapeDtypeStruct((depth, batch, MW_A, seq), F32),
                   jax.ShapeDtypeStruct((H_A * V_ROWS, m), BF16),
                   jax.ShapeDtypeStruct((m // tm, tm // MOBA_BLOCK, MW_A), F32),
                   jax.ShapeDtypeStruct((m, 2 * D_MODEL), BF16)],
        compiler_params=_params("parallel", "parallel"),
        name="inproj_prompt",
    )(*operands)


def _inproj_sample_kernel(x_ref, g_ref, wm_ref, wif_ref, wq_ref, wk_ref, wv_ref, wg_ref,
                          m_ref, mt_ref, gc_ref, qkv_ref, qt_ref, sg_ref):
    xn = _rms(x_ref[...], g_ref[...])
    xb = xn.astype(BF16)
    m_ref[...] = _dot_nt(xb, wm_ref[...])
    mt_ref[...] = _dot_nt(wm_ref[:2 * MW_M, :], xb)
    gc_ref[...] = _dot_nt(xn, wif_ref[...], HIGHEST)
    qkv_ref[0] = _dot_nt(xb, wq_ref[...])
    qkv_ref[1] = _dot_nt(xb, wk_ref[...])
    qkv_ref[2] = _dot_nt(xb, wv_ref[...])
    qt_ref[...] = _dot_nt(wq_ref[...], xb)
    sg_ref[...] = jax.nn.sigmoid(_dot_nt(xb, wg_ref[...])).astype(BF16)


def _inproj_sample(x, g, wm, wif, wq, wk, wv, wg):
    m = x.shape[0]
    return pl.pallas_call(
        _inproj_sample_kernel,
        out_shape=[jax.ShapeDtypeStruct((m, COL_M_END), F32),
                   jax.ShapeDtypeStruct((2 * MW_M, m), F32),
                   jax.ShapeDtypeStruct((m, N_GATE_COLS), F32),
                   jax.ShapeDtypeStruct((3, m, MW_A), F32),
                   jax.ShapeDtypeStruct((MW_A, m), F32),
                   jax.ShapeDtypeStruct((m, 2 * D_MODEL), BF16)],
        compiler_params=pltpu.CompilerParams(vmem_limit_bytes=VMEM_LIMIT),
        name="inproj_sample",
    )(x, g, wm, wif, wq, wk, wv, wg)


def _mlstm_prompt_kernel(m_ref, gc_ref, gr_ref, bc_ref, br_ref, gm_ref,
                         h_ref, c_ref, ms_ref):
    L = m_ref.shape[0]

    @pl.when(pl.program_id(1) == 0)
    def _():
        c_ref[...] = jnp.zeros_like(c_ref)
        ms_ref[...] = jnp.zeros_like(ms_ref)

    row = lax.broadcasted_iota(jnp.int32, (L, L), 0)
    col = lax.broadcasted_iota(jnp.int32, (L, L), 1)
    causal = col <= row
    lower = causal.astype(F32)
    upper = (row <= col).astype(F32)

    gcol = gc_ref[...] + br_ref[...]
    grow = gr_ref[...] + bc_ref[...]
    bcol = _dot(lower, _log_sigmoid(gcol), HIGHEST)
    brow = _dot(_log_sigmoid(grow), upper, HIGHEST)

    lane = lax.broadcasted_iota(jnp.int32, (L, DH_M), 1)
    ones_blk = jnp.where(lane == 0, 1.0, 0.0).astype(BF16)

    def operands(h):
        return (m_ref[:, h * DH_M:(h + 1) * DH_M],
                m_ref[:, MW_M + h * DH_M:MW_M + (h + 1) * DH_M],
                m_ref[:, 2 * MW_M + h * DH_M:2 * MW_M + (h + 1) * DH_M])

    qk_all = [_dot_nt(operands(h)[0], operands(h)[1]) for h in range(H_M)]
    qc_all = [_dot(operands(h)[0], c_ref[0, h].astype(BF16)) for h in range(H_M)]

    heads = range(H_M)
    i_col = [gcol[:, h:h + 1] for h in heads]
    b_col = [bcol[:, H_M + h:H_M + h + 1] for h in heads]
    i_row = [grow[h:h + 1, :] for h in heads]
    b_row = [brow[H_M + h:H_M + h + 1, :] for h in heads]
    b_last = [b_col[h][L - 1:L, :] for h in heads]
    m_prev = [ms_ref[0, h][0:1, 0:1] for h in heads]
    v_ext = [jnp.concatenate([operands(h)[2], ones_blk], axis=1) for h in heads]

    mt, a, sv = [], [], []
    for h in heads:
        u_row = i_row[h] - b_row[h]
        u_max = jnp.max(jnp.where(causal, u_row, NEG), axis=1, keepdims=True)
        inter = b_col[h] + m_prev[h]
        mt.append(jnp.maximum(inter, b_col[h] + u_max))
        a.append(jnp.exp(inter - mt[h]))
        expo = jnp.where(causal, (b_col[h] - mt[h] + LOG_K_SCALE) + u_row, NEG)
        s = qk_all[h] * jnp.exp(expo)
        sv.append(_dot(s.astype(BF16), v_ext[h]))

    for h in heads:
        k = operands(h)[1]
        g_row = b_last[h] - b_row[h] + i_row[h]
        e = b_last[h] + m_prev[h]
        m_new = jnp.maximum(e, jnp.max(g_row, axis=1, keepdims=True))
        w_col = jnp.exp(b_last[h] - b_col[h] + i_col[h] - (m_new - LOG_K_SCALE))
        kw = (k.astype(F32) * w_col).astype(BF16)
        c_ref[0, h] = jnp.exp(e - m_new) * c_ref[0, h] + _dot_tn(kw, v_ext[h])
        ms_ref[0, h] = jnp.broadcast_to(m_new, ms_ref.shape[2:])

    for h in heads:
        o = m_ref[:, 3 * MW_M + h * DH_M:3 * MW_M + (h + 1) * DH_M]
        nd = a[h] * qc_all[h] + sv[h]
        num = nd[:, :DH_M]
        denom = jnp.maximum(jnp.abs(nd[:, DH_M:DH_M + 1]), jnp.exp(-mt[h]))
        nc = num - jnp.mean(num, axis=1, keepdims=True)
        var = jnp.mean(nc * nc, axis=1, keepdims=True)
        hn = nc * lax.rsqrt(var + EPS * denom * denom) * gm_ref[:, h * DH_M:(h + 1) * DH_M]
        h_ref[:, h * DH_M:(h + 1) * DH_M] = (hn * jax.nn.sigmoid(o.astype(F32))).astype(BF16)


def _mlstm_prompt(mproj, gcol, grow, b_if, g_mlstm, batch):
    m = mproj.shape[0]
    L = L_CHUNK
    nc = m // batch // L
    bias_row = b_if.reshape(1, N_GATE_COLS)
    bias_col = b_if.reshape(N_GATE_COLS, 1)
    gm = g_mlstm.reshape(1, MW_M)
    return pl.pallas_call(
        _mlstm_prompt_kernel,
        grid=(batch, nc),
        in_specs=[pl.BlockSpec((L, COL_M_END), lambda b, c: (b * nc + c, 0)),
                  pl.BlockSpec((L, N_GATE_COLS), lambda b, c: (b * nc + c, 0)),
                  pl.BlockSpec((N_GATE_COLS, L), lambda b, c: (0, b * nc + c)),
                  pl.BlockSpec((N_GATE_COLS, 1), lambda b, c: (0, 0)),
                  pl.BlockSpec((1, N_GATE_COLS), lambda b, c: (0, 0)),
                  pl.BlockSpec((1, MW_M), lambda b, c: (0, 0))],
        out_specs=[pl.BlockSpec((L, MW_M), lambda b, c: (b * nc + c, 0)),
                   pl.BlockSpec((1, H_M, DH_M, 2 * DH_M), lambda b, c: (b, 0, 0, 0)),
                   pl.BlockSpec((1, H_M, 8, 128), lambda b, c: (b, 0, 0, 0))],
        out_shape=[jax.ShapeDtypeStruct((m, MW_M), BF16),
                   jax.ShapeDtypeStruct((batch, H_M, DH_M, 2 * DH_M), F32),
                   jax.ShapeDtypeStruct((batch, H_M, 8, 128), F32)],
        compiler_params=_params("parallel", "arbitrary"),
        name="mlstm_prompt",
    )(mproj, gcol, grow, bias_col, bias_row, gm)


def _mlstm_sample_kernel(m_ref, mt_ref, gc_ref, bi_ref, gm_ref, c_ref, n_ref, ms_ref,
                         h_ref, co_ref, no_ref, mo_ref):
    nb = mt_ref.shape[1]
    heads = range(H_M)
    onehot = lax.broadcasted_iota(jnp.int32, (1, nb), 1) == pl.program_id(0)
    g = gc_ref[0] + bi_ref[...]

    def part(p, h):
        return m_ref[0, :, p * MW_M + h * DH_M:p * MW_M + (h + 1) * DH_M]

    def column(p, h):
        rows = mt_ref[p * MW_M + h * DH_M:p * MW_M + (h + 1) * DH_M, :]
        return jnp.sum(jnp.where(onehot, rows, 0.0), axis=1, keepdims=True)

    q_col = [column(0, h) for h in heads]
    k_col = [column(1, h) for h in heads]
    m_new, a, w = [], [], []
    for h in heads:
        inter = _log_sigmoid(g[:, H_M + h:H_M + h + 1]) + ms_ref[0, :, h:h + 1]
        i_pre = g[:, h:h + 1]
        m_new.append(jnp.maximum(inter, i_pre))
        a.append(jnp.exp(inter - m_new[h]))
        w.append(jnp.exp(i_pre - (m_new[h] - LOG_K_SCALE)))
    s = [jnp.sum(part(0, h) * part(1, h), axis=1, keepdims=True) * w[h] for h in heads]
    qc = [jnp.sum(c_ref[0, h] * q_col[h], axis=0, keepdims=True) for h in heads]
    qn = [jnp.sum(part(0, h) * n_ref[0, h:h + 1, :], axis=1, keepdims=True) for h in heads]
    for h in heads:
        co_ref[0, h] = a[h] * c_ref[0, h] + (w[h] * k_col[h]) * part(2, h)
        no_ref[0, h:h + 1, :] = a[h] * n_ref[0, h:h + 1, :] + w[h] * part(1, h)
        mo_ref[0, :, h:h + 1] = m_new[h]
    for h in heads:
        sl = slice(h * DH_M, (h + 1) * DH_M)
        num = a[h] * qc[h] + s[h] * part(2, h)
        den = a[h] * qn[h] + s[h]
        hh = num / jnp.maximum(jnp.abs(den), jnp.exp(-m_new[h]))
        hc = hh - jnp.mean(hh, axis=1, keepdims=True)
        var = jnp.mean(hc * hc, axis=1, keepdims=True)
        hn = hc * lax.rsqrt(var + EPS) * gm_ref[:, sl]
        h_ref[0, :, sl] = (hn * jax.nn.sigmoid(part(3, h))).astype(BF16)


def _mlstm_sample(mproj, mproj_t, gcol, b_if, g_mlstm, c0, n0, m0):
    nb = mproj.shape[0]
    return pl.pallas_call(
        _mlstm_sample_kernel,
        grid=(nb,),
        in_specs=[pl.BlockSpec((1, 1, COL_M_END), lambda b: (b, 0, 0)),
                  pl.BlockSpec((2 * MW_M, nb), lambda b: (0, 0)),
                  pl.BlockSpec((1, 1, N_GATE_COLS), lambda b: (b, 0, 0)),
                  pl.BlockSpec((1, N_GATE_COLS), lambda b: (0, 0)),
                  pl.BlockSpec((1, MW_M), lambda b: (0, 0)),
                  pl.BlockSpec((1, H_M, DH_M, DH_M), lambda b: (b, 0, 0, 0)),
                  pl.BlockSpec((1, H_M, DH_M), lambda b: (b, 0, 0)),
                  pl.BlockSpec((1, 1, H_M), lambda b: (b, 0, 0))],
        out_specs=[pl.BlockSpec((1, 1, MW_M), lambda b: (b, 0, 0)),
                   pl.BlockSpec((1, H_M, DH_M, DH_M), lambda b: (b, 0, 0, 0)),
                   pl.BlockSpec((1, H_M, DH_M), lambda b: (b, 0, 0)),
                   pl.BlockSpec((1, 1, H_M), lambda b: (b, 0, 0))],
        out_shape=[jax.ShapeDtypeStruct((nb, 1, MW_M), BF16),
                   jax.ShapeDtypeStruct((nb, H_M, DH_M, DH_M), F32),
                   jax.ShapeDtypeStruct((nb, H_M, DH_M), F32),
                   jax.ShapeDtypeStruct((nb, 1, H_M), F32)],
        compiler_params=_params("parallel"),
        name="mlstm_sample",
    )(mproj.reshape(nb, 1, COL_M_END), mproj_t, gcol.reshape(nb, 1, N_GATE_COLS),
      b_if.reshape(1, N_GATE_COLS), g_mlstm.reshape(1, MW_M), c0, n0, m0.reshape(nb, 1, H_M))


def _t5_bucket(dist):
    n = jnp.maximum(dist, 0)
    max_exact = N_BUCKETS // 2
    nf = jnp.maximum(n, 1).astype(F32)
    large = max_exact + (jnp.log(nf / max_exact) / math.log(REL_MAX_DIST / max_exact)
                         * (N_BUCKETS - max_exact)).astype(jnp.int32)
    large = jnp.minimum(large, N_BUCKETS - 1)
    return jnp.where(n < max_exact, n, large)


def _t5_bias(dist, rel_ref, h):
    bucket = _t5_bucket(dist)
    bias = jnp.zeros(dist.shape, F32)
    for kb in range(N_BUCKETS):
        bias = jnp.where(bucket == kb, rel_ref[kb, h], bias)
    return bias


def _bias_table_kernel(rel_ref, o_ref):
    delta = pl.program_id(0)

    @pl.when(delta >= N_NEAR)
    def _():
        o_ref[...] = jnp.full(o_ref.shape, jnp.where(delta == TILE_MASKED, NEG, 0.0), F32)

    pl.when(delta < N_NEAR)(lambda: _bias_tiles(delta, rel_ref, o_ref))


def _bias_tiles(delta, rel_ref, o_ref):
    c = lax.broadcasted_iota(jnp.int32, (MOBA_BLOCK, MOBA_BLOCK), 0)
    r = lax.broadcasted_iota(jnp.int32, (MOBA_BLOCK, MOBA_BLOCK), 1)
    dist = delta * MOBA_BLOCK + r - c
    bucket = _t5_bucket(dist)
    bits = [(bucket >> i) & 1 == 1 for i in range(N_BUCKETS.bit_length() - 1)]
    for h in range(H_A):
        level = [rel_ref[kb, h] * LOG2E for kb in range(N_BUCKETS)]
        for bit in bits:
            level = [jnp.where(bit, level[2 * i + 1], level[2 * i]) for i in range(len(level) // 2)]
        o_ref[h, 0] = jnp.where(dist >= 0, level[0], NEG)


def _bias_table(rel_bias):
    return pl.pallas_call(
        _bias_table_kernel,
        grid=(N_TILES,),
        in_specs=[pl.BlockSpec(memory_space=pltpu.SMEM)],
        out_specs=pl.BlockSpec((H_A, 1, MOBA_BLOCK, MOBA_BLOCK), lambda d: (0, d, 0, 0)),
        out_shape=jax.ShapeDtypeStruct((H_A, N_TILES, MOBA_BLOCK, MOBA_BLOCK), F32),
        compiler_params=_params("parallel"),
        name="bias_table",
    )(rel_bias)


def _select_topk_t(score, n_valid_rows, k):
    nb = score.shape[0]
    row = lax.broadcasted_iota(jnp.int32, score.shape, 0)
    past = row < n_valid_rows
    sc = jnp.where(past, score, -jnp.inf)
    sel = jnp.full(score.shape, NEG, F32)
    for _ in range(k):
        mx = jnp.max(sc, axis=0, keepdims=True)
        idx = jnp.min(jnp.where(sc == mx, row, nb), axis=0, keepdims=True)
        hit = row == idx
        sel = jnp.where(hit, 0.0, sel)
        sc = jnp.where(hit, -jnp.inf, sc)
    return jnp.where(past, sel, NEG)


def _moba_prompt_kernel(qi_ref, grp_ref, first_ref, last_ref, far_item_ref, far_ref,
                        qt_ref, ka_ref, vte_ref, km_ref, tab_ref, o_ref,
                        qp_scr, m_scr, acc_scr, sa_scr, sb_scr, gma_scr, gmb_scr):
    hp = pl.program_id(1)
    seq = qt_ref.shape[1]
    nb = km_ref.shape[1]
    tq = MOBA_BLOCK
    n_items = qi_ref.shape[0] - 1
    setup_w = 4 * MOBA_BLOCK

    def setup(c, carry):
        cols = pl.ds(pl.multiple_of(c * setup_w, setup_w), setup_w)
        blk_row = lax.broadcasted_iota(jnp.int32, (nb, setup_w), 0)
        q_blk = (c * setup_w + lax.broadcasted_iota(jnp.int32, (1, setup_w), 1)) // MOBA_BLOCK
        sub = lax.broadcasted_iota(jnp.int32, (PAIR, setup_w), 0)
        qpair = qt_ref[:, cols]
        for h in range(2):
            qth = qt_ref[h * DH_A:(h + 1) * DH_A, cols].astype(F32)
            score = _dot(km_ref[0, :, h * DH_A:(h + 1) * DH_A], qth, HIGHEST)
            sel = _select_topk_t(score, q_blk, MOBA_TOPK)
            sel = jnp.where(blk_row == q_blk, 0.0, sel)
            sel = sel + jnp.where(q_blk - blk_row >= N_NEAR, far_ref[2 * hp + h], 0.0)
            hi = sel.astype(BF16)
            qp_scr[h, 0:PAIR, cols] = jnp.where(sub // DH_A == h, qpair, jnp.zeros_like(qpair))
            qp_scr[h, PAIR:, cols] = jnp.zeros((K_AUG - PAIR, setup_w), BF16)
            qp_scr[h, SEL_HI:SEL_HI + nb, cols] = hi
            qp_scr[h, SEL_LO:SEL_LO + nb, cols] = (sel - hi.astype(F32)).astype(BF16)
        return carry

    lax.fori_loop(0, seq // setup_w, setup, 0)
    m_scr[...] = jnp.full(m_scr.shape, NEG, F32)
    acc_scr[...] = jnp.zeros(acc_scr.shape, F32)

    gk = MOBA_GROUP * MOBA_BLOCK

    def score_block(w, t, h, far):
        qi = qi_ref[w]
        g = grp_ref[w]
        kblk = ka_ref[pl.ds(pl.multiple_of((g * MOBA_GROUP + t) * MOBA_BLOCK, MOBA_BLOCK),
                            MOBA_BLOCK), :]
        qcols = pl.ds(pl.multiple_of(qi * tq, tq), tq)
        s = _dot(kblk, qp_scr[h, :, qcols])
        if far:
            return s
        delta = qi - (g * MOBA_GROUP + t)
        tile = jnp.where(delta < 0, TILE_MASKED, jnp.minimum(delta, TILE_ZERO))
        return s + tab_ref[h, tile]

    def step(w, s_cur, gm_cur, s_nxt, gm_nxt, far):
        qi = qi_ref[w]
        start = pl.multiple_of(grp_ref[w] * gk, gk)
        first = first_ref[w] == 1
        m_old = [jnp.where(first, NEG, m_scr[h]) for h in range(2)]
        m_new = [jnp.maximum(m_old[h], gm_cur[h]) for h in range(2)]
        pv = [None, None]
        gmax = [None, None]

        def next_scores(t):
            rows = slice(t * MOBA_BLOCK, (t + 1) * MOBA_BLOCK)
            for h in range(2):
                s = score_block(w + 1, t, h, far)
                s_nxt[h, rows, :] = s
                cmax = jnp.max(s, axis=0, keepdims=True)
                gmax[h] = cmax if t == 0 else jnp.maximum(gmax[h], cmax)

        def attend(t):
            rows = slice(t * MOBA_BLOCK, (t + 1) * MOBA_BLOCK)
            for h in range(2):
                p = jnp.exp2(s_cur[h, rows, :] - m_new[h]).astype(BF16)
                vblk = vte_ref[h * V_ROWS:(h + 1) * V_ROWS,
                               pl.ds(start + t * MOBA_BLOCK, MOBA_BLOCK)]
                d = _dot(vblk, p)
                pv[h] = d if t == 0 else pv[h] + d

        for t in range(MOBA_GROUP):
            next_scores(t)
            attend(t)
        for h in range(2):
            acc_old = jnp.where(first, 0.0, acc_scr[h])
            acc_scr[h] = jnp.exp2(m_old[h] - m_new[h]) * acc_old + pv[h]
            m_scr[h] = m_new[h]
            gm_nxt[h] = gmax[h]

        @pl.when(last_ref[w] == 1)
        def _():
            outs = [acc_scr[h, 0:DH_A, :] / acc_scr[h, DH_A:DH_A + 1, :] for h in range(2)]
            orows = pl.ds(pl.multiple_of(qi * tq, tq), tq)
            o_ref[orows, :] = jnp.concatenate(outs, axis=0).T.astype(BF16)

    for h in range(2):
        gmax = None
        for t in range(MOBA_GROUP):
            s = score_block(0, t, h, False)
            sa_scr[h, t * MOBA_BLOCK:(t + 1) * MOBA_BLOCK, :] = s
            cmax = jnp.max(s, axis=0, keepdims=True)
            gmax = cmax if t == 0 else jnp.maximum(gmax, cmax)
        gma_scr[h] = gmax

    def step_by_kind(w, *buffers):
        next_is_far = far_item_ref[w + 1] == 1
        pl.when(next_is_far)(lambda: step(w, *buffers, True))
        pl.when(jnp.logical_not(next_is_far))(lambda: step(w, *buffers, False))

    def pair_of_items(i, carry):
        w = 2 * i
        step_by_kind(w, sa_scr, gma_scr, sb_scr, gmb_scr)
        step_by_kind(w + 1, sb_scr, gmb_scr, sa_scr, gma_scr)
        return carry

    lax.fori_loop(0, n_items // 2, pair_of_items, 0)


def _moba_work_items(nb):
    qi, grp, first, last, far = [], [], [], [], []
    for q in range(nb):
        n_groups = q // MOBA_GROUP + 1
        for g in range(n_groups):
            qi.append(q)
            grp.append(g)
            first.append(int(g == 0))
            last.append(int(g == n_groups - 1))
            far.append(int(q - (g * MOBA_GROUP + MOBA_GROUP - 1) >= N_NEAR))
    assert len(qi) % 2 == 0
    tables = [qi + qi[-1:], grp + grp[-1:], first + [0], last + [0], far + [0]]
    return [jnp.asarray(t, jnp.int32) for t in tables]


def _moba_prompt(qt, ka, vte, kmean, table, far, batch):
    m = qt.shape[1]
    seq = m // batch
    nb = seq // MOBA_BLOCK
    assert nb <= MAX_BLOCKS and nb % MOBA_GROUP == 0
    tq = MOBA_BLOCK
    n_pairs = H_A // 2
    gk = MOBA_GROUP * MOBA_BLOCK
    smem = pl.BlockSpec(memory_space=pltpu.SMEM)
    return pl.pallas_call(
        _moba_prompt_kernel,
        grid=(batch, n_pairs),
        in_specs=[smem, smem, smem, smem, smem, smem,
                  pl.BlockSpec((PAIR, seq), lambda b, hp: (hp, b)),
                  pl.BlockSpec((seq, K_AUG), lambda b, hp: (b, hp)),
                  pl.BlockSpec((2 * V_ROWS, seq), lambda b, hp: (hp, b)),
                  pl.BlockSpec((1, nb, PAIR), lambda b, hp: (b, 0, hp)),
                  pl.BlockSpec((2, N_TILES, tq, tq), lambda b, hp: (hp, 0, 0, 0),
                               pipeline_mode=pl.Buffered(1))],
        out_specs=pl.BlockSpec((seq, PAIR), lambda b, hp: (b, hp)),
        out_shape=jax.ShapeDtypeStruct((m, MW_A), BF16),
        scratch_shapes=[pltpu.VMEM((2, K_AUG, seq), BF16),
                        pltpu.VMEM((2, 1, tq), F32),
                        pltpu.VMEM((2, V_ROWS, tq), F32),
                        pltpu.VMEM((2, gk, tq), F32),
                        pltpu.VMEM((2, gk, tq), F32),
                        pltpu.VMEM((2, 1, tq), F32),
                        pltpu.VMEM((2, 1, tq), F32)],
        compiler_params=_params("parallel", "arbitrary"),
        name="moba_prompt",
    )(*_moba_work_items(nb), far, qt, ka, vte, kmean.reshape(batch, nb, MW_A), table)


def _logits_sample_kernel(pt_ref, qt_ref, *refs):
    o_ref, qb_scr = refs[-2], refs[-1]
    b = pl.program_id(0)
    nb_s = qt_ref.shape[1]

    @pl.when(pl.program_id(1) == 0)
    def _():
        onehot = lax.broadcasted_iota(jnp.int32, (1, nb_s), 1) == b
        qcol = jnp.sum(jnp.where(onehot, qt_ref[...], 0.0), axis=1, keepdims=True)
        qb_scr[...] = jnp.broadcast_to(qcol * Q_SCALE, qb_scr.shape)

    per_block = MOBA_BLOCK // PAGE_SIZE
    for i in range(PAGES_PER_STEP):
        for h in range(H_A):
            kt = refs[i][0, 0, h]
            lg = jnp.sum(kt * qb_scr[h * DH_A:(h + 1) * DH_A, :], axis=0, keepdims=True)
            r, half = i // per_block, i % per_block
            o_ref[0, h, r:r + 1, half * PAGE_SIZE:(half + 1) * PAGE_SIZE] = lg


def _logits_sample(cache_kt, page_table, qt, layer):
    nb_s, n_pages = page_table.shape
    steps = n_pages // PAGES_PER_STEP
    rows = PAGES_PER_STEP * PAGE_SIZE // MOBA_BLOCK

    def page_spec(i):
        return pl.BlockSpec((1, 1, H_A, DH_A, PAGE_SIZE),
                            lambda b, s, pt: (layer, pt[b, s * PAGES_PER_STEP + i], 0, 0, 0))

    return pl.pallas_call(
        _logits_sample_kernel,
        grid_spec=pltpu.PrefetchScalarGridSpec(
            num_scalar_prefetch=1,
            grid=(nb_s, steps),
            in_specs=[pl.BlockSpec(qt.shape, lambda b, s, pt: (0, 0))]
                     + [page_spec(i) for i in range(PAGES_PER_STEP)],
            out_specs=pl.BlockSpec((1, H_A, rows, MOBA_BLOCK), lambda b, s, pt: (b, 0, s, 0)),
            scratch_shapes=[pltpu.VMEM((MW_A, PAGE_SIZE), F32)]),
        out_shape=jax.ShapeDtypeStruct((nb_s, H_A, steps * rows, MOBA_BLOCK), F32),
        compiler_params=_params("parallel", "arbitrary"),
        name="logits_sample",
    )(page_table, qt, *([cache_kt] * PAGES_PER_STEP))


def _select_sample_kernel(lg_ref, o_ref):
    sc = jnp.sum(lg_ref[...], axis=-1)
    nblk = sc.shape[-1]
    blk = lax.broadcasted_iota(jnp.int32, sc.shape, 2)
    for t in range(MOBA_TOPK):
        mx = jnp.max(sc, axis=2, keepdims=True)
        idx = jnp.min(jnp.where(sc == mx, blk, nblk), axis=2, keepdims=True)
        o_ref[:, :, t:t + 1] = idx
        sc = jnp.where(blk == idx, -jnp.inf, sc)


def _select_sample(logits):
    nb_s = logits.shape[0]
    return pl.pallas_call(
        _select_sample_kernel,
        out_shape=jax.ShapeDtypeStruct((nb_s, H_A, MOBA_TOPK), jnp.int32),
        compiler_params=pltpu.CompilerParams(vmem_limit_bytes=VMEM_LIMIT),
        name="select_sample",
    )(logits)


def _bias_sample_kernel(rel_ref, o_ref, *, past_len):
    h = pl.program_id(0)
    nblk = o_ref.shape[1]
    kpos = (lax.broadcasted_iota(jnp.int32, (nblk, MOBA_BLOCK), 0) * MOBA_BLOCK
            + lax.broadcasted_iota(jnp.int32, (nblk, MOBA_BLOCK), 1))
    o_ref[0] = _t5_bias(past_len - kpos, rel_ref, h)


def _bias_sample(rel_bias, past_len):
    nblk = past_len // MOBA_BLOCK
    return pl.pallas_call(
        functools.partial(_bias_sample_kernel, past_len=past_len),
        grid=(H_A,),
        in_specs=[pl.BlockSpec(memory_space=pltpu.SMEM)],
        out_specs=pl.BlockSpec((1, nblk, MOBA_BLOCK), lambda h: (h, 0, 0)),
        out_shape=jax.ShapeDtypeStruct((H_A, nblk, MOBA_BLOCK), F32),
        compiler_params=_params("parallel"),
        name="bias_sample",
    )(rel_bias)


def _attend_sample_kernel(pt_ref, sel_ref, rel_ref, lg_ref, bias_ref, qkv_ref, *refs):
    o_ref = refs[-1]
    v_refs = refs[:-1]
    per_block = MOBA_BLOCK // PAGE_SIZE
    n_sel_pages = MOBA_TOPK * per_block
    b = pl.program_id(0)

    for j in range(ATTEND_HEADS):
        h = pl.program_id(1) * ATTEND_HEADS + j
        q = qkv_ref[0, 0, pl.ds(h, 1), :] * Q_SCALE
        k_new = qkv_ref[1, 0, pl.ds(h, 1), :]
        v_new = qkv_ref[2, 0, pl.ds(h, 1), :]
        own = jnp.sum(k_new * q, axis=1, keepdims=True) + rel_ref[0, h]
        logits = []
        mx = own
        for t in range(MOBA_TOPK):
            blk = sel_ref[b, h, t]
            lg = lg_ref[0, j, pl.ds(blk, 1), :] + bias_ref[j, pl.ds(blk, 1), :]
            logits.append(lg)
            mx = jnp.maximum(mx, jnp.max(lg, axis=1, keepdims=True))
        p_own = jnp.exp(own - mx)
        den = p_own
        acc = p_own * v_new
        for t in range(MOBA_TOPK):
            p = jnp.exp(logits[t] - mx)
            den = den + jnp.sum(p, axis=1, keepdims=True)
            pb = p.astype(BF16)
            for i in range(per_block):
                vt = v_refs[j * n_sel_pages + t * per_block + i][0, 0, 0].astype(BF16)
                acc = acc + _dot_nt(pb[:, i * PAGE_SIZE:(i + 1) * PAGE_SIZE], vt)
        o_ref[0, pl.ds(h, 1), :] = acc / den


def _attend_sample(cache_vt, page_table, sel, rel_bias, bias_s, logits, qkv, layer):
    nb_s = page_table.shape[0]
    per_block = MOBA_BLOCK // PAGE_SIZE
    n_sel_pages = MOBA_TOPK * per_block
    nblk = logits.shape[2]
    nh = ATTEND_HEADS

    def page_spec(j, t):
        def imap(b, hg, pt, sl, rel):
            h = hg * nh + j
            page = sl[b, h, t // per_block] * per_block + (t % per_block)
            return (layer, pt[b, page], h, 0, 0)
        return pl.BlockSpec((1, 1, 1, DH_A, PAGE_SIZE), imap)

    pages = [page_spec(j, t) for j in range(nh) for t in range(n_sel_pages)]
    return pl.pallas_call(
        _attend_sample_kernel,
        grid_spec=pltpu.PrefetchScalarGridSpec(
            num_scalar_prefetch=3,
            grid=(nb_s, H_A // nh),
            in_specs=[pl.BlockSpec((1, nh, nblk, MOBA_BLOCK), lambda b, hg, pt, sl, rel: (b, hg, 0, 0)),
                      pl.BlockSpec((nh, nblk, MOBA_BLOCK), lambda b, hg, pt, sl, rel: (hg, 0, 0)),
                      pl.BlockSpec((3, 1, H_A, DH_A), lambda b, hg, pt, sl, rel: (0, b, 0, 0))]
                     + pages,
            out_specs=pl.BlockSpec((1, H_A, DH_A), lambda b, hg, pt, sl, rel: (b, 0, 0))),
        out_shape=jax.ShapeDtypeStruct((nb_s, H_A, DH_A), F32),
        compiler_params=_params("parallel", "arbitrary"),
        name="attend_sample",
    )(page_table, sel, rel_bias, logits, bias_s, qkv, *([cache_vt] * len(pages)))


def _mix_ffn_kernel(x_ref, hm_ref, ha_ref, sg_ref, wbm_ref, wba_ref, wo_ref, gmix_ref,
                    gpre_ref, gpost_ref, wg_ref, wu_ref, wd_ref, o_ref):
    sg = sg_ref[...].astype(F32)
    mixed = (sg[:, :D_MODEL] * _dot(hm_ref[...], wbm_ref[...])
             + sg[:, D_MODEL:] * _dot(ha_ref[...], wba_ref[...]))
    x = x_ref[...] + _rms(_dot(mixed.astype(BF16), wo_ref[...]), gmix_ref[...])
    hf = _rms(x, gpre_ref[...]).astype(BF16)
    acc = jnp.zeros(x.shape, F32)
    for c in range(D_FF // FF_CHUNK):
        sl = slice(c * FF_CHUNK, (c + 1) * FF_CHUNK)
        gate = _dot(hf, wg_ref[:, sl])
        up = _dot(hf, wu_ref[:, sl])
        mid = (gate * jax.nn.sigmoid(gate) * up).astype(BF16)
        acc = acc + _dot(mid, wd_ref[sl, :])
    o_ref[...] = x + _rms(acc, gpost_ref[...])


def _mix_ffn(x, hm, ha, sg, wbm, wba, wo, gmix, gpre, gpost, wg, wu, wd):
    m = x.shape[0]
    tm = min(TM_FFN, m)
    row = lambda w: pl.BlockSpec((tm, w), lambda i: (i, 0))
    full = lambda a: pl.BlockSpec(a.shape, lambda i: (0,) * a.ndim, pipeline_mode=pl.Buffered(1))
    weights = (wbm, wba, wo, gmix, gpre, gpost, wg, wu, wd)
    return pl.pallas_call(
        _mix_ffn_kernel,
        grid=(m // tm,),
        in_specs=[row(D_MODEL), row(MW_M), row(MW_A), row(2 * D_MODEL)] + [full(w) for w in weights],
        out_specs=row(D_MODEL),
        out_shape=jax.ShapeDtypeStruct((m, D_MODEL), F32),
        compiler_params=_params("parallel"),
        name="mix_ffn",
    )(x, hm, ha, sg, *weights)


def kernel(x_prompt, x_sample, cache_k, cache_v, state_C, state_n, state_m, page_table,
           norm_mix_pre, norm_mix_post, norm_ffn_pre, norm_ffn_post, w_in, b_igate, b_fgate,
           g_mlstm, w_branch_m, w_branch_a, w_out, w_ffn_gate, w_ffn_up, w_ffn_down, rel_bias):
    bp, seq, _ = x_prompt.shape
    bs = x_sample.shape[0]
    depth = w_in.shape[0]
    xp = x_prompt.reshape(bp * seq, D_MODEL)
    xs = x_sample.reshape(bs, D_MODEL)
    cache_kt = jnp.transpose(cache_k, (0, 1, 3, 4, 2))
    cache_vt = jnp.transpose(cache_v, (0, 1, 3, 4, 2))
    w_in_t = jnp.swapaxes(w_in, 1, 2)

    table = _bias_table(rel_bias)
    far = rel_bias[N_BUCKETS - 1] * LOG2E
    bias_s = _bias_sample(rel_bias, page_table.shape[1] * PAGE_SIZE)

    outs = {k: [] for k in ("cp", "np", "mp", "ks", "vs", "cs", "ns", "ms")}
    kv_prompt = None
    for l in range(depth):
        row = lambda a: a[l].reshape(1, -1)
        wt = w_in_t[l]
        wm = wt[:COL_M_END].astype(BF16)
        wif = wt[COL_M_END:COL_IF_END]
        wq = wt[COL_IF_END:COL_Q_END].astype(BF16)
        wk = wt[COL_Q_END:COL_K_END].astype(BF16)
        wv = wt[COL_K_END:COL_V_END].astype(BF16)
        wgt = wt[COL_V_END:].astype(BF16)
        b_if = jnp.concatenate([b_igate[l], b_fgate[l]])
        wbm, wba, wo = (w_branch_m[l].astype(BF16), w_branch_a[l].astype(BF16), w_out[l].astype(BF16))
        wfg, wfu, wfd = (w_ffn_gate[l].astype(BF16), w_ffn_up[l].astype(BF16), w_ffn_down[l].astype(BF16))

        mproj, gcol, grow, qt, ka, kt32, vt32, vte, kmean, sg = _inproj_prompt(
            xp, row(norm_mix_pre), wm, wif, wq, wk, wv, wgt, bp, l, depth, kv_prompt)
        kv_prompt = (kt32, vt32)
        hm, cext, mstate = _mlstm_prompt(mproj, gcol, grow, b_if, g_mlstm[l], bp)
        ha = _moba_prompt(qt, ka, vte, kmean, table, far, bp)
        post = (wbm, wba, wo, row(norm_mix_post), row(norm_ffn_pre), row(norm_ffn_post),
                wfg, wfu, wfd)
        xp = _mix_ffn(xp, hm, ha, sg, *post)
        outs["cp"].append(cext[..., :DH_M])
        outs["np"].append(cext[..., DH_M])
        outs["mp"].append(mstate[:, :, 0, 0])

        mproj_s, mproj_t, gcol_s, qkv_s, qt_s, sg_s = _inproj_sample(
            xs, row(norm_mix_pre), wm, wif, wq, wk, wv, wgt)
        hm_s, c_s, n_s, m_s = _mlstm_sample(mproj_s, mproj_t, gcol_s, b_if, g_mlstm[l],
                                            state_C[l], state_n[l], state_m[l])
        logits_s = _logits_sample(cache_kt, page_table, qt_s, l)
        sel = _select_sample(logits_s)
        ha_s = _attend_sample(cache_vt, page_table, sel, rel_bias, bias_s, logits_s,
                              qkv_s.reshape(3, bs, H_A, DH_A), l)
        xs = _mix_ffn(xs, hm_s.reshape(bs, MW_M), ha_s.reshape(bs, MW_A).astype(BF16), sg_s, *post)
        outs["ks"].append(qkv_s[1].reshape(bs, 1, H_A, DH_A))
        outs["vs"].append(qkv_s[2].reshape(bs, 1, H_A, DH_A))
        outs["cs"].append(c_s)
        outs["ns"].append(n_s)
        outs["ms"].append(m_s.reshape(bs, H_M))

    st = lambda k: jnp.stack(outs[k])
    kv_out = lambda a: jnp.transpose(a.reshape(depth, bp, H_A, DH_A, seq), (0, 1, 4, 2, 3))
    return (xp.reshape(bp, seq, D_MODEL), xs.reshape(bs, 1, D_MODEL),
            kv_out(kv_prompt[0]), kv_out(kv_prompt[1]), st("cp"), st("np"), st("mp"),
            st("ks"), st("vs"), st("cs"), st("ns"), st("ms"))
```

```python
import functools
import math

import jax
import jax.numpy as jnp
from jax import lax
from jax.experimental import pallas as pl
from jax.experimental.pallas import tpu as pltpu

F32 = jnp.float32
BF16 = jnp.bfloat16
HIGHEST = lax.Precision.HIGHEST

D_MODEL = 1024
H_M, DH_M = 4, 128
MW_M = H_M * DH_M
H_A, DH_A = 8, 64
MW_A = H_A * DH_A
MOBA_BLOCK = 256
MOBA_TOPK = 3
N_BUCKETS = 32
REL_MAX_DIST = 2048
D_FF = 2816
EPS = 1e-6
NEG = -1e30
PAGE_SIZE = 128

N_GATE_COLS = 2 * H_M
COL_M_END = 4 * MW_M
COL_IF_END = COL_M_END + N_GATE_COLS
COL_Q_END = COL_IF_END + MW_A
COL_K_END = COL_Q_END + MW_A
COL_V_END = COL_K_END + MW_A
LOG_K_SCALE = math.log(DH_M ** -0.5)
Q_SCALE = DH_A ** -0.5
LOG2E = math.log2(math.e)

N_NEAR = (REL_MAX_DIST + MOBA_BLOCK - 1) // MOBA_BLOCK + 1
assert (N_NEAR * MOBA_BLOCK - (MOBA_BLOCK - 1)) >= REL_MAX_DIST
TILE_ZERO = N_NEAR
TILE_MASKED = N_NEAR + 1
N_TILES = N_NEAR + 2
MOBA_GROUP = 8

PAIR = 2 * DH_A
K_AUG = 2 * PAIR
SEL_HI, SEL_LO = PAIR, PAIR + 32
V_ROWS = DH_A + 16
MAX_BLOCKS = 32

VMEM_LIMIT = 56 * 1024 * 1024

TM_PROJ = 512
TM_FFN = 512
L_CHUNK = 256
FF_CHUNK = 1408
PAGES_PER_STEP = 16
ATTEND_HEADS = 2


def _params(*sem):
    return pltpu.CompilerParams(dimension_semantics=sem, vmem_limit_bytes=VMEM_LIMIT)


def _rms(x, g):
    return x * lax.rsqrt(jnp.mean(x * x, axis=-1, keepdims=True) + EPS) * g


def _log_sigmoid(x):
    return jnp.minimum(x, 0.0) - jnp.log(1.0 + jnp.exp(-jnp.abs(x)))


def _dot(a, b, precision=None):
    return jnp.dot(a, b, precision=precision, preferred_element_type=F32)


def _dot_nt(a, b, precision=None):
    return lax.dot_general(a, b, (((1,), (1,)), ((), ())), precision=precision,
                           preferred_element_type=F32)


def _dot_tn(a, b):
    return lax.dot_general(a, b, (((0,), (0,)), ((), ())), preferred_element_type=F32)


def _inproj_prompt_kernel(x_ref, g_ref, wm_ref, wif_ref, wq_ref, wk_ref, wv_ref, wg_ref, *refs):
    (m_ref, gc_ref, gr_ref, qt_ref, ka_ref, kt32_ref, vt32_ref, vte_ref, km_ref, sg_ref) = refs[-10:]
    tm = x_ref.shape[0]
    xn = _rms(x_ref[...], g_ref[...])
    xb = xn.astype(BF16)
    m_ref[...] = _dot_nt(xb, wm_ref[...]).astype(BF16)
    gates = jnp.concatenate(
        [jnp.sum(xn * wif_ref[c:c + 1, :], axis=1, keepdims=True) for c in range(N_GATE_COLS)],
        axis=1)
    gc_ref[...] = gates
    gr_ref[...] = gates.T
    qt_ref[...] = (_dot_nt(wq_ref[...], xb) * (Q_SCALE * LOG2E)).astype(BF16)

    k = _dot_nt(xb, wk_ref[...])
    kt = k.T
    for d in range(kt32_ref.shape[0]):
        kt32_ref[d, 0] = kt
    km_ref[0] = jnp.mean(k.reshape(tm // MOBA_BLOCK, MOBA_BLOCK, MW_A), axis=1)
    row = lax.broadcasted_iota(jnp.int32, (tm, PAIR), 0)
    lane = lax.broadcasted_iota(jnp.int32, (tm, PAIR), 1)
    blk = (pl.program_id(1) * tm + row) // MOBA_BLOCK
    onehot = jnp.where(jnp.logical_and(lane < 2 * MAX_BLOCKS, lane % MAX_BLOCKS == blk),
                       1.0, 0.0).astype(BF16)
    kb = k.astype(BF16)
    for p in range(H_A // 2):
        ka_ref[:, p * K_AUG:p * K_AUG + PAIR] = kb[:, p * PAIR:(p + 1) * PAIR]
        ka_ref[:, p * K_AUG + PAIR:(p + 1) * K_AUG] = onehot

    vt = _dot_nt(wv_ref[...], xb)
    for d in range(vt32_ref.shape[0]):
        vt32_ref[d, 0] = vt
    ones_rows = jnp.where(lax.broadcasted_iota(jnp.int32, (V_ROWS - DH_A, tm), 0) == 0,
                          1.0, 0.0).astype(BF16)
    for h in range(H_A):
        vte_ref[h * V_ROWS:h * V_ROWS + DH_A, :] = vt[h * DH_A:(h + 1) * DH_A].astype(BF16)
        vte_ref[h * V_ROWS + DH_A:(h + 1) * V_ROWS, :] = ones_rows
    sg_ref[...] = jax.nn.sigmoid(_dot_nt(xb, wg_ref[...])).astype(BF16)


def _inproj_prompt(x, g, wm, wif, wq, wk, wv, wg, batch, layer, depth, kv_prev):
    m = x.shape[0]
    seq = m // batch
    tm = TM_PROJ
    nt = seq // tm
    n_pairs = H_A // 2
    row = lambda w: pl.BlockSpec((tm, w), lambda b, t: (b * nt + t, 0))
    col = lambda h: pl.BlockSpec((h, tm), lambda b, t: (0, b * nt + t))
    full = lambda a: pl.BlockSpec(a.shape, lambda b, t: (0,) * a.ndim)
    kv_spec = (pl.BlockSpec((depth, 1, MW_A, tm), lambda b, t: (0, b, 0, t)) if kv_prev is None
               else pl.BlockSpec((1, 1, MW_A, tm), lambda b, t: (layer, b, 0, t)))
    operands = [x, g, wm, wif, wq, wk, wv, wg]
    in_specs = [row(D_MODEL), full(g), full(wm), full(wif), full(wq), full(wk), full(wv), full(wg)]
    aliases = {}
    if kv_prev is not None:
        aliases = {len(operands): 5, len(operands) + 1: 6}
        operands += list(kv_prev)
        in_specs += [pl.BlockSpec(memory_space=pl.ANY)] * 2
    return pl.pallas_call(
        _inproj_prompt_kernel,
        grid=(batch, nt),
        in_specs=in_specs,
        input_output_aliases=aliases,
        out_specs=[row(COL_M_END), row(N_GATE_COLS), col(N_GATE_COLS), col(MW_A),
                   row(n_pairs * K_AUG), kv_spec, kv_spec,
                   col(H_A * V_ROWS),
                   pl.BlockSpec((1, tm // MOBA_BLOCK, MW_A), lambda b, t: (b * nt + t, 0, 0)),
                   row(2 * D_MODEL)],
        out_shape=[jax.ShapeDtypeStruct((m, COL_M_END), BF16),
                   jax.ShapeDtypeStruct((m, N_GATE_COLS), F32),
                   jax.ShapeDtypeStruct((N_GATE_COLS, m), F32),
                   jax.ShapeDtypeStruct((MW_A, m), BF16),
                   jax.ShapeDtypeStruct((m, n_pairs * K_AUG), BF16),
                   jax.ShapeDtypeStruct((depth, batch, MW_A, seq), F32),
                   jax.ShapeDtypeStruct((depth, batch, MW_A, seq), F32),
                   jax.ShapeDtypeStruct((H_A * V_ROWS, m), BF16),
                   jax.ShapeDtypeStruct((m // tm, tm // MOBA_BLOCK, MW_A), F32),
                   jax.ShapeDtypeStruct((m, 2 * D_MODEL), BF16)],
        compiler_params=_params("parallel", "parallel"),
        name="inproj_prompt",
    )(*operands)


def _inproj_sample_kernel(x_ref, g_ref, wm_ref, wif_ref, wq_ref, wk_ref, wv_ref, wg_ref,
                          m_ref, mt_ref, gc_ref, qkv_ref, qt_ref, sg_ref):
    xn = _rms(x_ref[...], g_ref[...])
    xb = xn.astype(BF16)
    m_ref[...] = _dot_nt(xb, wm_ref[...])
    mt_ref[...] = _dot_nt(wm_ref[:2 * MW_M, :], xb)
    gc_ref[...] = _dot_nt(xn, wif_ref[...], HIGHEST)
    qkv_ref[0] = _dot_nt(xb, wq_ref[...])
    qkv_ref[1] = _dot_nt(xb, wk_ref[...])
    qkv_ref[2] = _dot_nt(xb, wv_ref[...])
    qt_ref[...] = _dot_nt(wq_ref[...], xb)
    sg_ref[...] = jax.nn.sigmoid(_dot_nt(xb, wg_ref[...])).astype(BF16)


def _inproj_sample(x, g, wm, wif, wq, wk, wv, wg):
    m = x.shape[0]
    return pl.pallas_call(
        _inproj_sample_kernel,
        out_shape=[jax.ShapeDtypeStruct((m, COL_M_END), F32),
                   jax.ShapeDtypeStruct((2 * MW_M, m), F32),
                   jax.ShapeDtypeStruct((m, N_GATE_COLS), F32),
                   jax.ShapeDtypeStruct((3, m, MW_A), F32),
                   jax.ShapeDtypeStruct((MW_A, m), F32),
                   jax.ShapeDtypeStruct((m, 2 * D_MODEL), BF16)],
        compiler_params=pltpu.CompilerParams(vmem_limit_bytes=VMEM_LIMIT),
        name="inproj_sample",
    )(x, g, wm, wif, wq, wk, wv, wg)


def _mlstm_prompt_kernel(m_ref, gc_ref, gr_ref, bc_ref, br_ref, gm_ref,
                         h_ref, c_ref, ms_ref):
    L = m_ref.shape[0]

    @pl.when(pl.program_id(1) == 0)
    def _():
        c_ref[...] = jnp.zeros_like(c_ref)
        ms_ref[...] = jnp.zeros_like(ms_ref)

    row = lax.broadcasted_iota(jnp.int32, (L, L), 0)
    col = lax.broadcasted_iota(jnp.int32, (L, L), 1)
    causal = col <= row
    lower = causal.astype(F32)
    upper = (row <= col).astype(F32)

    gcol = gc_ref[...] + br_ref[...]
    grow = gr_ref[...] + bc_ref[...]
    bcol = _dot(lower, _log_sigmoid(gcol), HIGHEST)
    brow = _dot(_log_sigmoid(grow), upper, HIGHEST)

    lane = lax.broadcasted_iota(jnp.int32, (L, DH_M), 1)
    ones_blk = jnp.where(lane == 0, 1.0, 0.0).astype(BF16)

    def operands(h):
        return (m_ref[:, h * DH_M:(h + 1) * DH_M],
                m_ref[:, MW_M + h * DH_M:MW_M + (h + 1) * DH_M],
                m_ref[:, 2 * MW_M + h * DH_M:2 * MW_M + (h + 1) * DH_M])

    qk_all = [_dot_nt(operands(h)[0], operands(h)[1]) for h in range(H_M)]
    qc_all = [_dot(operands(h)[0], c_ref[0, h].astype(BF16)) for h in range(H_M)]

    heads = range(H_M)
    i_col = [gcol[:, h:h + 1] for h in heads]
    b_col = [bcol[:, H_M + h:H_M + h + 1] for h in heads]
    i_row = [grow[h:h + 1, :] for h in heads]
    b_row = [brow[H_M + h:H_M + h + 1, :] for h in heads]
    b_last = [b_col[h][L - 1:L, :] for h in heads]
    m_prev = [ms_ref[0, h][0:1, 0:1] for h in heads]
    v_ext = [jnp.concatenate([operands(h)[2], ones_blk], axis=1) for h in heads]

    mt, a, sv = [], [], []
    for h in heads:
        u_row = i_row[h] - b_row[h]
        u_max = jnp.max(jnp.where(causal, u_row, NEG), axis=1, keepdims=True)
        inter = b_col[h] + m_prev[h]
        mt.append(jnp.maximum(inter, b_col[h] + u_max))
        a.append(jnp.exp(inter - mt[h]))
        expo = jnp.where(causal, (b_col[h] - mt[h] + LOG_K_SCALE) + u_row, NEG)
        s = qk_all[h] * jnp.exp(expo)
        sv.append(_dot(s.astype(BF16), v_ext[h]))

    for h in heads:
        k = operands(h)[1]
        g_row = b_last[h] - b_row[h] + i_row[h]
        e = b_last[h] + m_prev[h]
        m_new = jnp.maximum(e, jnp.max(g_row, axis=1, keepdims=True))
        w_col = jnp.exp(b_last[h] - b_col[h] + i_col[h] - (m_new - LOG_K_SCALE))
        kw = (k.astype(F32) * w_col).astype(BF16)
        c_ref[0, h] = jnp.exp(e - m_new) * c_ref[0, h] + _dot_tn(kw, v_ext[h])
        ms_ref[0, h] = jnp.broadcast_to(m_new, ms_ref.shape[2:])

    for h in heads:
        o = m_ref[:, 3 * MW_M + h * DH_M:3 * MW_M + (h + 1) * DH_M]
        nd = a[h] * qc_all[h] + sv[h]
        num = nd[:, :DH_M]
        denom = jnp.maximum(jnp.abs(nd[:, DH_M:DH_M + 1]), jnp.exp(-mt[h]))
        nc = num - jnp.mean(num, axis=1, keepdims=True)
        var = jnp.mean(nc * nc, axis=1, keepdims=True)
        hn = nc * lax.rsqrt(var + EPS * denom * denom) * gm_ref[:, h * DH_M:(h + 1) * DH_M]
        h_ref[:, h * DH_M:(h + 1) * DH_M] = (hn * jax.nn.sigmoid(o.astype(F32))).astype(BF16)


def _mlstm_prompt(mproj, gcol, grow, b_if, g_mlstm, batch):
    m = mproj.shape[0]
    L = L_CHUNK
    nc = m // batch // L
    bias_row = b_if.reshape(1, N_GATE_COLS)
    bias_col = b_if.reshape(N_GATE_COLS, 1)
    gm = g_mlstm.reshape(1, MW_M)
    return pl.pallas_call(
        _mlstm_prompt_kernel,
        grid=(batch, nc),
        in_specs=[pl.BlockSpec((L, COL_M_END), lambda b, c: (b * nc + c, 0)),
                  pl.BlockSpec((L, N_GATE_COLS), lambda b, c: (b * nc + c, 0)),
                  pl.BlockSpec((N_GATE_COLS, L), lambda b, c: (0, b * nc + c)),
                  pl.BlockSpec((N_GATE_COLS, 1), lambda b, c: (0, 0)),
                  pl.BlockSpec((1, N_GATE_COLS), lambda b, c: (0, 0)),
                  pl.BlockSpec((1, MW_M), lambda b, c: (0, 0))],
        out_specs=[pl.BlockSpec((L, MW_M), lambda b, c: (b * nc + c, 0)),
                   pl.BlockSpec((1, H_M, DH_M, 2 * DH_M), lambda b, c: (b, 0, 0, 0)),
                   pl.BlockSpec((1, H_M, 8, 128), lambda b, c: (b, 0, 0, 0))],
        out_shape=[jax.ShapeDtypeStruct((m, MW_M), BF16),
                   jax.ShapeDtypeStruct((batch, H_M, DH_M, 2 * DH_M), F32),
                   jax.ShapeDtypeStruct((batch, H_M, 8, 128), F32)],
        compiler_params=_params("parallel", "arbitrary"),
        name="mlstm_prompt",
    )(mproj, gcol, grow, bias_col, bias_row, gm)


def _mlstm_sample_kernel(m_ref, mt_ref, gc_ref, bi_ref, gm_ref, c_ref, n_ref, ms_ref,
                         h_ref, co_ref, no_ref, mo_ref):
    nb = mt_ref.shape[1]
    heads = range(H_M)
    onehot = lax.broadcasted_iota(jnp.int32, (1, nb), 1) == pl.program_id(0)
    g = gc_ref[0] + bi_ref[...]

    def part(p, h):
        return m_ref[0, :, p * MW_M + h * DH_M:p * MW_M + (h + 1) * DH_M]

    def column(p, h):
        rows = mt_ref[p * MW_M + h * DH_M:p * MW_M + (h + 1) * DH_M, :]
        return jnp.sum(jnp.where(onehot, rows, 0.0), axis=1, keepdims=True)

    q_col = [column(0, h) for h in heads]
    k_col = [column(1, h) for h in heads]
    m_new, a, w = [], [], []
    for h in heads:
        inter = _log_sigmoid(g[:, H_M + h:H_M + h + 1]) + ms_ref[0, :, h:h + 1]
        i_pre = g[:, h:h + 1]
        m_new.append(jnp.maximum(inter, i_pre))
        a.append(jnp.exp(inter - m_new[h]))
        w.append(jnp.exp(i_pre - (m_new[h] - LOG_K_SCALE)))
    s = [jnp.sum(part(0, h) * part(1, h), axis=1, keepdims=True) * w[h] for h in heads]
    qc = [jnp.sum(c_ref[0, h] * q_col[h], axis=0, keepdims=True) for h in heads]
    qn = [jnp.sum(part(0, h) * n_ref[0, h:h + 1, :], axis=1, keepdims=True) for h in heads]
    for h in heads:
        co_ref[0, h] = a[h] * c_ref[0, h] + (w[h] * k_col[h]) * part(2, h)
        no_ref[0, h:h + 1, :] = a[h] * n_ref[0, h:h + 1, :] + w[h] * part(1, h)
        mo_ref[0, :, h:h + 1] = m_new[h]
    for h in heads:
        sl = slice(h * DH_M, (h + 1) * DH_M)
        num = a[h] * qc[h] + s[h] * part(2, h)
        den = a[h] * qn[h] + s[h]
        hh = num / jnp.maximum(jnp.abs(den), jnp.exp(-m_new[h]))
        hc = hh - jnp.mean(hh, axis=1, keepdims=True)
        var = jnp.mean(hc * hc, axis=1, keepdims=True)
        hn = hc * lax.rsqrt(var + EPS) * gm_ref[:, sl]
        h_ref[0, :, sl] = (hn * jax.nn.sigmoid(part(3, h))).astype(BF16)


def _mlstm_sample(mproj, mproj_t, gcol, b_if, g_mlstm, c0, n0, m0):
    nb = mproj.shape[0]
    return pl.pallas_call(
        _mlstm_sample_kernel,
        grid=(nb,),
        in_specs=[pl.BlockSpec((1, 1, COL_M_END), lambda b: (b, 0, 0)),
                  pl.BlockSpec((2 * MW_M, nb), lambda b: (0, 0)),
                  pl.BlockSpec((1, 1, N_GATE_COLS), lambda b: (b, 0, 0)),
                  pl.BlockSpec((1, N_GATE_COLS), lambda b: (0, 0)),
                  pl.BlockSpec((1, MW_M), lambda b: (0, 0)),
                  pl.BlockSpec((1, H_M, DH_M, DH_M), lambda b: (b, 0, 0, 0)),
                  pl.BlockSpec((1, H_M, DH_M), lambda b: (b, 0, 0)),
                  pl.BlockSpec((1, 1, H_M), lambda b: (b, 0, 0))],
        out_specs=[pl.BlockSpec((1, 1, MW_M), lambda b: (b, 0, 0)),
                   pl.BlockSpec((1, H_M, DH_M, DH_M), lambda b: (b, 0, 0, 0)),
                   pl.BlockSpec((1, H_M, DH_M), lambda b: (b, 0, 0)),
                   pl.BlockSpec((1, 1, H_M), lambda b: (b, 0, 0))],
        out_shape=[jax.ShapeDtypeStruct((nb, 1, MW_M), BF16),
                   jax.ShapeDtypeStruct((nb, H_M, DH_M, DH_M), F32),
                   jax.ShapeDtypeStruct((nb, H_M, DH_M), F32),
                   jax.ShapeDtypeStruct((nb, 1, H_M), F32)],
        compiler_params=_params("parallel"),
        name="mlstm_sample",
    )(mproj.reshape(nb, 1, COL_M_END), mproj_t, gcol.reshape(nb, 1, N_GATE_COLS),
      b_if.reshape(1, N_GATE_COLS), g_mlstm.reshape(1, MW_M), c0, n0, m0.reshape(nb, 1, H_M))


def _t5_bucket(dist):
    n = jnp.maximum(dist, 0)
    max_exact = N_BUCKETS // 2
    nf = jnp.maximum(n, 1).astype(F32)
    large = max_exact + (jnp.log(nf / max_exact) / math.log(REL_MAX_DIST / max_exact)
                         * (N_BUCKETS - max_exact)).astype(jnp.int32)
    large = jnp.minimum(large, N_BUCKETS - 1)
    return jnp.where(n < max_exact, n, large)


def _t5_bias(dist, rel_ref, h):
    bucket = _t5_bucket(dist)
    bias = jnp.zeros(dist.shape, F32)
    for kb in range(N_BUCKETS):
        bias = jnp.where(bucket == kb, rel_ref[kb, h], bias)
    return bias


def _bias_table_kernel(rel_ref, o_ref):
    delta = pl.program_id(0)

    @pl.when(delta >= N_NEAR)
    def _():
        o_ref[...] = jnp.full(o_ref.shape, jnp.where(delta == TILE_MASKED, NEG, 0.0), F32)

    pl.when(delta < N_NEAR)(lambda: _bias_tiles(delta, rel_ref, o_ref))


def _bias_tiles(delta, rel_ref, o_ref):
    c = lax.broadcasted_iota(jnp.int32, (MOBA_BLOCK, MOBA_BLOCK), 0)
    r = lax.broadcasted_iota(jnp.int32, (MOBA_BLOCK, MOBA_BLOCK), 1)
    dist = delta * MOBA_BLOCK + r - c
    bucket = _t5_bucket(dist)
    bits = [(bucket >> i) & 1 == 1 for i in range(N_BUCKETS.bit_length() - 1)]
    for h in range(H_A):
        level = [rel_ref[kb, h] * LOG2E for kb in range(N_BUCKETS)]
        for bit in bits:
            level = [jnp.where(bit, level[2 * i + 1], level[2 * i]) for i in range(len(level) // 2)]
        o_ref[h, 0] = jnp.where(dist >= 0, level[0], NEG)


def _bias_table(rel_bias):
    return pl.pallas_call(
        _bias_table_kernel,
        grid=(N_TILES,),
        in_specs=[pl.BlockSpec(memory_space=pltpu.SMEM)],
        out_specs=pl.BlockSpec((H_A, 1, MOBA_BLOCK, MOBA_BLOCK), lambda d: (0, d, 0, 0)),
        out_shape=jax.ShapeDtypeStruct((H_A, N_TILES, MOBA_BLOCK, MOBA_BLOCK), F32),
        compiler_params=_params("parallel"),
        name="bias_table",
    )(rel_bias)


def _select_topk_t(score, n_valid_rows, k):
    nb = score.shape[0]
    row = lax.broadcasted_iota(jnp.int32, score.shape, 0)
    past = row < n_valid_rows
    sc = jnp.where(past, score, -jnp.inf)
    sel = jnp.full(score.shape, NEG, F32)
    for _ in range(k):
        mx = jnp.max(sc, axis=0, keepdims=True)
        idx = jnp.min(jnp.where(sc == mx, row, nb), axis=0, keepdims=True)
        hit = row == idx
        sel = jnp.where(hit, 0.0, sel)
        sc = jnp.where(hit, -jnp.inf, sc)
    return jnp.where(past, sel, NEG)


def _moba_prompt_kernel(qi_ref, grp_ref, first_ref, last_ref, far_item_ref, far_ref,
                        qt_ref, ka_ref, vte_ref, km_ref, tab_ref, o_ref,
                        qp_scr, m_scr, acc_scr, sa_scr, sb_scr, gma_scr, gmb_scr):
    hp = pl.program_id(1)
    seq = qt_ref.shape[1]
    nb = km_ref.shape[1]
    tq = MOBA_BLOCK
    n_items = qi_ref.shape[0] - 1
    setup_w = 4 * MOBA_BLOCK

    def setup(c, carry):
        cols = pl.ds(pl.multiple_of(c * setup_w, setup_w), setup_w)
        blk_row = lax.broadcasted_iota(jnp.int32, (nb, setup_w), 0)
        q_blk = (c * setup_w + lax.broadcasted_iota(jnp.int32, (1, setup_w), 1)) // MOBA_BLOCK
        sub = lax.broadcasted_iota(jnp.int32, (PAIR, setup_w), 0)
        qpair = qt_ref[:, cols]
        for h in range(2):
            qth = qt_ref[h * DH_A:(h + 1) * DH_A, cols].astype(F32)
            score = _dot(km_ref[0, :, h * DH_A:(h + 1) * DH_A], qth, HIGHEST)
            sel = _select_topk_t(score, q_blk, MOBA_TOPK)
            sel = jnp.where(blk_row == q_blk, 0.0, sel)
            sel = sel + jnp.where(q_blk - blk_row >= N_NEAR, far_ref[2 * hp + h], 0.0)
            hi = sel.astype(BF16)
            qp_scr[h, 0:PAIR, cols] = jnp.where(sub // DH_A == h, qpair, jnp.zeros_like(qpair))
            qp_scr[h, PAIR:, cols] = jnp.zeros((K_AUG - PAIR, setup_w), BF16)
            qp_scr[h, SEL_HI:SEL_HI + nb, cols] = hi
            qp_scr[h, SEL_LO:SEL_LO + nb, cols] = (sel - hi.astype(F32)).astype(BF16)
        return carry

    lax.fori_loop(0, seq // setup_w, setup, 0)
    m_scr[...] = jnp.full(m_scr.shape, NEG, F32)
    acc_scr[...] = jnp.zeros(acc_scr.shape, F32)

    gk = MOBA_GROUP * MOBA_BLOCK

    def score_block(w, t, h, far):
        qi = qi_ref[w]
        g = grp_ref[w]
        kblk = ka_ref[pl.ds(pl.multiple_of((g * MOBA_GROUP + t) * MOBA_BLOCK, MOBA_BLOCK),
                            MOBA_BLOCK), :]
        qcols = pl.ds(pl.multiple_of(qi * tq, tq), tq)
        s = _dot(kblk, qp_scr[h, :, qcols])
        if far:
            return s
        delta = qi - (g * MOBA_GROUP + t)
        tile = jnp.where(delta < 0, TILE_MASKED, jnp.minimum(delta, TILE_ZERO))
        return s + tab_ref[h, tile]

    def step(w, s_cur, gm_cur, s_nxt, gm_nxt, far):
        qi = qi_ref[w]
        start = pl.multiple_of(grp_ref[w] * gk, gk)
        first = first_ref[w] == 1
        m_old = [jnp.where(first, NEG, m_scr[h]) for h in range(2)]
        m_new = [jnp.maximum(m_old[h], gm_cur[h]) for h in range(2)]
        pv = [None, None]
        gmax = [None, None]

        def next_scores(t):
            rows = slice(t * MOBA_BLOCK, (t + 1) * MOBA_BLOCK)
            for h in range(2):
                s = score_block(w + 1, t, h, far)
                s_nxt[h, rows, :] = s
                cmax = jnp.max(s, axis=0, keepdims=True)
                gmax[h] = cmax if t == 0 else jnp.maximum(gmax[h], cmax)

        def attend(t):
            rows = slice(t * MOBA_BLOCK, (t + 1) * MOBA_BLOCK)
            for h in range(2):
                p = jnp.exp2(s_cur[h, rows, :] - m_new[h]).astype(BF16)
                vblk = vte_ref[h * V_ROWS:(h + 1) * V_ROWS,
                               pl.ds(start + t * MOBA_BLOCK, MOBA_BLOCK)]
                d = _dot(vblk, p)
                pv[h] = d if t == 0 else pv[h] + d

        for t in range(MOBA_GROUP):
            attend(t)
            next_scores(t)
        for h in range(2):
            acc_old = jnp.where(first, 0.0, acc_scr[h])
            acc_scr[h] = jnp.exp2(m_old[h] - m_new[h]) * acc_old + pv[h]
            m_scr[h] = m_new[h]
            gm_nxt[h] = gmax[h]

        @pl.when(last_ref[w] == 1)
        def _():
            outs = [acc_scr[h, 0:DH_A, :] / acc_scr[h, DH_A:DH_A + 1, :] for h in range(2)]
            orows = pl.ds(pl.multiple_of(qi * tq, tq), tq)
            o_ref[orows, :] = jnp.concatenate(outs, axis=0).T.astype(BF16)

    for h in range(2):
        gmax = None
        for t in range(MOBA_GROUP):
            s = score_block(0, t, h, False)
            sa_scr[h, t * MOBA_BLOCK:(t + 1) * MOBA_BLOCK, :] = s
            cmax = jnp.max(s, axis=0, keepdims=True)
            gmax = cmax if t == 0 else jnp.maximum(gmax, cmax)
        gma_scr[h] = gmax

    def step_by_kind(w, *buffers):
        next_is_far = far_item_ref[w + 1] == 1
        pl.when(next_is_far)(lambda: step(w, *buffers, True))
        pl.when(jnp.logical_not(next_is_far))(lambda: step(w, *buffers, False))

    def pair_of_items(i, carry):
        w = 2 * i
        step_by_kind(w, sa_scr, gma_scr, sb_scr, gmb_scr)
        step_by_kind(w + 1, sb_scr, gmb_scr, sa_scr, gma_scr)
        return carry

    lax.fori_loop(0, n_items // 2, pair_of_items, 0)


def _moba_work_items(nb):
    qi, grp, first, last, far = [], [], [], [], []
    for q in range(nb):
        n_groups = q // MOBA_GROUP + 1
        for g in range(n_groups):
            qi.append(q)
            grp.append(g)
            first.append(int(g == 0))
            last.append(int(g == n_groups - 1))
            far.append(int(q - (g * MOBA_GROUP + MOBA_GROUP - 1) >= N_NEAR))
    assert len(qi) % 2 == 0
    tables = [qi + qi[-1:], grp + grp[-1:], first + [0], last + [0], far + [0]]
    return [jnp.asarray(t, jnp.int32) for t in tables]


def _moba_prompt(qt, ka, vte, kmean, table, far, batch):
    m = qt.shape[1]
    seq = m // batch
    nb = seq // MOBA_BLOCK
    assert nb <= MAX_BLOCKS and nb % MOBA_GROUP == 0
    tq = MOBA_BLOCK
    n_pairs = H_A // 2
    gk = MOBA_GROUP * MOBA_BLOCK
    smem = pl.BlockSpec(memory_space=pltpu.SMEM)
    return pl.pallas_call(
        _moba_prompt_kernel,
        grid=(batch, n_pairs),
        in_specs=[smem, smem, smem, smem, smem, smem,
                  pl.BlockSpec((PAIR, seq), lambda b, hp: (hp, b)),
                  pl.BlockSpec((seq, K_AUG), lambda b, hp: (b, hp)),
                  pl.BlockSpec((2 * V_ROWS, seq), lambda b, hp: (hp, b)),
                  pl.BlockSpec((1, nb, PAIR), lambda b, hp: (b, 0, hp)),
                  pl.BlockSpec((2, N_TILES, tq, tq), lambda b, hp: (hp, 0, 0, 0),
                               pipeline_mode=pl.Buffered(1))],
        out_specs=pl.BlockSpec((seq, PAIR), lambda b, hp: (b, hp)),
        out_shape=jax.ShapeDtypeStruct((m, MW_A), BF16),
        scratch_shapes=[pltpu.VMEM((2, K_AUG, seq), BF16),
                        pltpu.VMEM((2, 1, tq), F32),
                        pltpu.VMEM((2, V_ROWS, tq), F32),
                        pltpu.VMEM((2, gk, tq), F32),
                        pltpu.VMEM((2, gk, tq), F32),
                        pltpu.VMEM((2, 1, tq), F32),
                        pltpu.VMEM((2, 1, tq), F32)],
        compiler_params=_params("parallel", "arbitrary"),
        name="moba_prompt",
    )(*_moba_work_items(nb), far, qt, ka, vte, kmean.reshape(batch, nb, MW_A), table)


def _logits_sample_kernel(pt_ref, qt_ref, *refs):
    o_ref, qb_scr = refs[-2], refs[-1]
    b = pl.program_id(0)
    nb_s = qt_ref.shape[1]

    @pl.when(pl.program_id(1) == 0)
    def _():
        onehot = lax.broadcasted_iota(jnp.int32, (1, nb_s), 1) == b
        qcol = jnp.sum(jnp.where(onehot, qt_ref[...], 0.0), axis=1, keepdims=True)
        qb_scr[...] = jnp.broadcast_to(qcol * Q_SCALE, qb_scr.shape)

    per_block = MOBA_BLOCK // PAGE_SIZE
    for i in range(PAGES_PER_STEP):
        for h in range(H_A):
            kt = refs[i][0, 0, h]
            lg = jnp.sum(kt * qb_scr[h * DH_A:(h + 1) * DH_A, :], axis=0, keepdims=True)
            r, half = i // per_block, i % per_block
            o_ref[0, h, r:r + 1, half * PAGE_SIZE:(half + 1) * PAGE_SIZE] = lg


def _logits_sample(cache_kt, page_table, qt, layer):
    nb_s, n_pages = page_table.shape
    steps = n_pages // PAGES_PER_STEP
    rows = PAGES_PER_STEP * PAGE_SIZE // MOBA_BLOCK

    def page_spec(i):
        return pl.BlockSpec((1, 1, H_A, DH_A, PAGE_SIZE),
                            lambda b, s, pt: (layer, pt[b, s * PAGES_PER_STEP + i], 0, 0, 0))

    return pl.pallas_call(
        _logits_sample_kernel,
        grid_spec=pltpu.PrefetchScalarGridSpec(
            num_scalar_prefetch=1,
            grid=(nb_s, steps),
            in_specs=[pl.BlockSpec(qt.shape, lambda b, s, pt: (0, 0))]
                     + [page_spec(i) for i in range(PAGES_PER_STEP)],
            out_specs=pl.BlockSpec((1, H_A, rows, MOBA_BLOCK), lambda b, s, pt: (b, 0, s, 0)),
            scratch_shapes=[pltpu.VMEM((MW_A, PAGE_SIZE), F32)]),
        out_shape=jax.ShapeDtypeStruct((nb_s, H_A, steps * rows, MOBA_BLOCK), F32),
        compiler_params=_params("parallel", "arbitrary"),
        name="logits_sample",
    )(page_table, qt, *([cache_kt] * PAGES_PER_STEP))


def _select_sample_kernel(lg_ref, o_ref):
    sc = jnp.sum(lg_ref[...], axis=-1)
    nblk = sc.shape[-1]
    blk = lax.broadcasted_iota(jnp.int32, sc.shape, 2)
    for t in range(MOBA_TOPK):
        mx = jnp.max(sc, axis=2, keepdims=True)
        idx = jnp.min(jnp.where(sc == mx, blk, nblk), axis=2, keepdims=True)
        o_ref[:, :, t:t + 1] = idx
        sc = jnp.where(blk == idx, -jnp.inf, sc)


def _select_sample(logits):
    nb_s = logits.shape[0]
    return pl.pallas_call(
        _select_sample_kernel,
        out_shape=jax.ShapeDtypeStruct((nb_s, H_A, MOBA_TOPK), jnp.int32),
        compiler_params=pltpu.CompilerParams(vmem_limit_bytes=VMEM_LIMIT),
        name="select_sample",
    )(logits)


def _bias_sample_kernel(rel_ref, o_ref, *, past_len):
    h = pl.program_id(0)
    nblk = o_ref.shape[1]
    kpos = (lax.broadcasted_iota(jnp.int32, (nblk, MOBA_BLOCK), 0) * MOBA_BLOCK
            + lax.broadcasted_iota(jnp.int32, (nblk, MOBA_BLOCK), 1))
    o_ref[0] = _t5_bias(past_len - kpos, rel_ref, h)


def _bias_sample(rel_bias, past_len):
    nblk = past_len // MOBA_BLOCK
    return pl.pallas_call(
        functools.partial(_bias_sample_kernel, past_len=past_len),
        grid=(H_A,),
        in_specs=[pl.BlockSpec(memory_space=pltpu.SMEM)],
        out_specs=pl.BlockSpec((1, nblk, MOBA_BLOCK), lambda h: (h, 0, 0)),
        out_shape=jax.ShapeDtypeStruct((H_A, nblk, MOBA_BLOCK), F32),
        compiler_params=_params("parallel"),
        name="bias_sample",
    )(rel_bias)


def _attend_sample_kernel(pt_ref, sel_ref, rel_ref, lg_ref, bias_ref, qkv_ref, *refs):
    o_ref = refs[-1]
    v_refs = refs[:-1]
    per_block = MOBA_BLOCK // PAGE_SIZE
    n_sel_pages = MOBA_TOPK * per_block
    b = pl.program_id(0)

    for j in range(ATTEND_HEADS):
        h = pl.program_id(1) * ATTEND_HEADS + j
        q = qkv_ref[0, 0, pl.ds(h, 1), :] * Q_SCALE
        k_new = qkv_ref[1, 0, pl.ds(h, 1), :]
        v_new = qkv_ref[2, 0, pl.ds(h, 1), :]
        own = jnp.sum(k_new * q, axis=1, keepdims=True) + rel_ref[0, h]
        logits = []
        mx = own
        for t in range(MOBA_TOPK):
            blk = sel_ref[b, h, t]
            lg = lg_ref[0, j, pl.ds(blk, 1), :] + bias_ref[j, pl.ds(blk, 1), :]
            logits.append(lg)
            mx = jnp.maximum(mx, jnp.max(lg, axis=1, keepdims=True))
        p_own = jnp.exp(own - mx)
        den = p_own
        acc = p_own * v_new
        for t in range(MOBA_TOPK):
            p = jnp.exp(logits[t] - mx)
            den = den + jnp.sum(p, axis=1, keepdims=True)
            pb = p.astype(BF16)
            for i in range(per_block):
                vt = v_refs[j * n_sel_pages + t * per_block + i][0, 0, 0].astype(BF16)
                acc = acc + _dot_nt(pb[:, i * PAGE_SIZE:(i + 1) * PAGE_SIZE], vt)
        o_ref[0, pl.ds(h, 1), :] = acc / den


def _attend_sample(cache_vt, page_table, sel, rel_bias, bias_s, logits, qkv, layer):
    nb_s = page_table.shape[0]
    per_block = MOBA_BLOCK // PAGE_SIZE
    n_sel_pages = MOBA_TOPK * per_block
    nblk = logits.shape[2]
    nh = ATTEND_HEADS

    def page_spec(j, t):
        def imap(b, hg, pt, sl, rel):
            h = hg * nh + j
            page = sl[b, h, t // per_block] * per_block + (t % per_block)
            return (layer, pt[b, page], h, 0, 0)
        return pl.BlockSpec((1, 1, 1, DH_A, PAGE_SIZE), imap)

    pages = [page_spec(j, t) for j in range(nh) for t in range(n_sel_pages)]
    return pl.pallas_call(
        _attend_sample_kernel,
        grid_spec=pltpu.PrefetchScalarGridSpec(
            num_scalar_prefetch=3,
            grid=(nb_s, H_A // nh),
            in_specs=[pl.BlockSpec((1, nh, nblk, MOBA_BLOCK), lambda b, hg, pt, sl, rel: (b, hg, 0, 0)),
                      pl.BlockSpec((nh, nblk, MOBA_BLOCK), lambda b, hg, pt, sl, rel: (hg, 0, 0)),
                      pl.BlockSpec((3, 1, H_A, DH_A), lambda b, hg, pt, sl, rel: (0, b, 0, 0))]
                     + pages,
            out_specs=pl.BlockSpec((1, H_A, DH_A), lambda b, hg, pt, sl, rel: (b, 0, 0))),
        out_shape=jax.ShapeDtypeStruct((nb_s, H_A, DH_A), F32),
        compiler_params=_params("parallel", "arbitrary"),
        name="attend_sample",
    )(page_table, sel, rel_bias, logits, bias_s, qkv, *([cache_vt] * len(pages)))


def _mix_ffn_kernel(x_ref, hm_ref, ha_ref, sg_ref, wbm_ref, wba_ref, wo_ref, gmix_ref,
                    gpre_ref, gpost_ref, wg_ref, wu_ref, wd_ref, o_ref):
    sg = sg_ref[...].astype(F32)
    mixed = (sg[:, :D_MODEL] * _dot(hm_ref[...], wbm_ref[...])
             + sg[:, D_MODEL:] * _dot(ha_ref[...], wba_ref[...]))
    x = x_ref[...] + _rms(_dot(mixed.astype(BF16), wo_ref[...]), gmix_ref[...])
    hf = _rms(x, gpre_ref[...]).astype(BF16)
    acc = jnp.zeros(x.shape, F32)
    for c in range(D_FF // FF_CHUNK):
        sl = slice(c * FF_CHUNK, (c + 1) * FF_CHUNK)
        gate = _dot(hf, wg_ref[:, sl])
        up = _dot(hf, wu_ref[:, sl])
        mid = (gate * jax.nn.sigmoid(gate) * up).astype(BF16)
        acc = acc + _dot(mid, wd_ref[sl, :])
    o_ref[...] = x + _rms(acc, gpost_ref[...])


def _mix_ffn(x, hm, ha, sg, wbm, wba, wo, gmix, gpre, gpost, wg, wu, wd):
    m = x.shape[0]
    tm = min(TM_FFN, m)
    row = lambda w: pl.BlockSpec((tm, w), lambda i: (i, 0))
    full = lambda a: pl.BlockSpec(a.shape, lambda i: (0,) * a.ndim, pipeline_mode=pl.Buffered(1))
    weights = (wbm, wba, wo, gmix, gpre, gpost, wg, wu, wd)
    return pl.pallas_call(
        _mix_ffn_kernel,
        grid=(m // tm,),
        in_specs=[row(D_MODEL), row(MW_M), row(MW_A), row(2 * D_MODEL)] + [full(w) for w in weights],
        out_specs=row(D_MODEL),
        out_shape=jax.ShapeDtypeStruct((m, D_MODEL), F32),
        compiler_params=_params("parallel"),
        name="mix_ffn",
    )(x, hm, ha, sg, *weights)


def kernel(x_prompt, x_sample, cache_k, cache_v, state_C, state_n, state_m, page_table,
           norm_mix_pre, norm_mix_post, norm_ffn_pre, norm_ffn_post, w_in, b_igate, b_fgate,
           g_mlstm, w_branch_m, w_branch_a, w_out, w_ffn_gate, w_ffn_up, w_ffn_down, rel_bias):
    bp, seq, _ = x_prompt.shape
    bs = x_sample.shape[0]
    depth = w_in.shape[0]
    xp = x_prompt.reshape(bp * seq, D_MODEL)
    xs = x_sample.reshape(bs, D_MODEL)
    cache_kt = jnp.transpose(cache_k, (0, 1, 3, 4, 2))
    cache_vt = jnp.transpose(cache_v, (0, 1, 3, 4, 2))
    w_in_t = jnp.swapaxes(w_in, 1, 2)

    table = _bias_table(rel_bias)
    far = rel_bias[N_BUCKETS - 1] * LOG2E
    bias_s = _bias_sample(rel_bias, page_table.shape[1] * PAGE_SIZE)

    outs = {k: [] for k in ("cp", "np", "mp", "ks", "vs", "cs", "ns", "ms")}
    kv_prompt = None
    for l in range(depth):
        row = lambda a: a[l].reshape(1, -1)
        wt = w_in_t[l]
        wm = wt[:COL_M_END].astype(BF16)
        wif = wt[COL_M_END:COL_IF_END]
        wq = wt[COL_IF_END:COL_Q_END].astype(BF16)
        wk = wt[COL_Q_END:COL_K_END].astype(BF16)
        wv = wt[COL_K_END:COL_V_END].astype(BF16)
        wgt = wt[COL_V_END:].astype(BF16)
        b_if = jnp.concatenate([b_igate[l], b_fgate[l]])
        wbm, wba, wo = (w_branch_m[l].astype(BF16), w_branch_a[l].astype(BF16), w_out[l].astype(BF16))
        wfg, wfu, wfd = (w_ffn_gate[l].astype(BF16), w_ffn_up[l].astype(BF16), w_ffn_down[l].astype(BF16))

        mproj, gcol, grow, qt, ka, kt32, vt32, vte, kmean, sg = _inproj_prompt(
            xp, row(norm_mix_pre), wm, wif, wq, wk, wv, wgt, bp, l, depth, kv_prompt)
        kv_prompt = (kt32, vt32)
        hm, cext, mstate = _mlstm_prompt(mproj, gcol, grow, b_if, g_mlstm[l], bp)
        ha = _moba_prompt(qt, ka, vte, kmean, table, far, bp)
        post = (wbm, wba, wo, row(norm_mix_post), row(norm_ffn_pre), row(norm_ffn_post),
                wfg, wfu, wfd)
        xp = _mix_ffn(xp, hm, ha, sg, *post)
        outs["cp"].append(cext[..., :DH_M])
        outs["np"].append(cext[..., DH_M])
        outs["mp"].append(mstate[:, :, 0, 0])

        mproj_s, mproj_t, gcol_s, qkv_s, qt_s, sg_s = _inproj_sample(
            xs, row(norm_mix_pre), wm, wif, wq, wk, wv, wgt)
        hm_s, c_s, n_s, m_s = _mlstm_sample(mproj_s, mproj_t, gcol_s, b_if, g_mlstm[l],
                                            state_C[l], state_n[l], state_m[l])
        logits_s = _logits_sample(cache_kt, page_table, qt_s, l)
        sel = _select_sample(logits_s)
        ha_s = _attend_sample(cache_vt, page_table, sel, rel_bias, bias_s, logits_s,
                              qkv_s.reshape(3, bs, H_A, DH_A), l)
        xs = _mix_ffn(xs, hm_s.reshape(bs, MW_M), ha_s.reshape(bs, MW_A).astype(BF16), sg_s, *post)
        outs["ks"].append(qkv_s[1].reshape(bs, 1, H_A, DH_A))
        outs["vs"].append(qkv_s[2].reshape(bs, 1, H_A, DH_A))
        outs["cs"].append(c_s)
        outs["ns"].append(n_s)
        outs["ms"].append(m_s.reshape(bs, H_M))

    st = lambda k: jnp.stack(outs[k])
    kv_out = lambda a: jnp.transpose(a.reshape(depth, bp, H_A, DH_A, seq), (0, 1, 4, 2, 3))
    return (xp.reshape(bp, seq, D_MODEL), xs.reshape(bs, 1, D_MODEL),
            kv_out(kv_prompt[0]), kv_out(kv_prompt[1]), st("cp"), st("np"), st("mp"),
            st("ks"), st("vs"), st("cs"), st("ns"), st("ms"))
```

```python
import functools
import math

import jax
import jax.numpy as jnp
from jax import lax
from jax.experimental import pallas as pl
from jax.experimental.pallas import tpu as pltpu

F32 = jnp.float32
BF16 = jnp.bfloat16
HIGHEST = lax.Precision.HIGHEST

D_MODEL = 1024
H_M, DH_M = 4, 128
MW_M = H_M * DH_M
H_A, DH_A = 8, 64
MW_A = H_A * DH_A
MOBA_BLOCK = 256
MOBA_TOPK = 3
N_BUCKETS = 32
REL_MAX_DIST = 2048
D_FF = 2816
EPS = 1e-6
NEG = -1e30
PAGE_SIZE = 128

N_GATE_COLS = 2 * H_M
COL_M_END = 4 * MW_M
COL_IF_END = COL_M_END + N_GATE_COLS
COL_Q_END = COL_IF_END + MW_A
COL_K_END = COL_Q_END + MW_A
COL_V_END = COL_K_END + MW_A
LOG_K_SCALE = math.log(DH_M ** -0.5)
Q_SCALE = DH_A ** -0.5
LOG2E = math.log2(math.e)

N_NEAR = (REL_MAX_DIST + MOBA_BLOCK - 1) // MOBA_BLOCK + 1
assert (N_NEAR * MOBA_BLOCK - (MOBA_BLOCK - 1)) >= REL_MAX_DIST
TILE_ZERO = N_NEAR
TILE_MASKED = N_NEAR + 1
N_TILES = N_NEAR + 2
MOBA_GROUP = 8

PAIR = 2 * DH_A
K_AUG = 2 * PAIR
SEL_HI, SEL_LO = PAIR, PAIR + 32
V_ROWS = DH_A + 16
MAX_BLOCKS = 32

VMEM_LIMIT = 56 * 1024 * 1024

TM_PROJ = 512
TM_FFN = 512
L_CHUNK = 256
FF_CHUNK = D_FF
PAGES_PER_STEP = 16
ATTEND_HEADS = 2


def _params(*sem):
    return pltpu.CompilerParams(dimension_semantics=sem, vmem_limit_bytes=VMEM_LIMIT)


def _rms(x, g):
    return x * lax.rsqrt(jnp.mean(x * x, axis=-1, keepdims=True) + EPS) * g


def _log_sigmoid(x):
    return jnp.minimum(x, 0.0) - jnp.log(1.0 + jnp.exp(-jnp.abs(x)))


def _dot(a, b, precision=None):
    return jnp.dot(a, b, precision=precision, preferred_element_type=F32)


def _dot_nt(a, b, precision=None):
    return lax.dot_general(a, b, (((1,), (1,)), ((), ())), precision=precision,
                           preferred_element_type=F32)


def _dot_tn(a, b):
    return lax.dot_general(a, b, (((0,), (0,)), ((), ())), preferred_element_type=F32)


def _inproj_prompt_kernel(x_ref, g_ref, wm_ref, wif_ref, wq_ref, wk_ref, wv_ref, wg_ref, *refs):
    (m_ref, gc_ref, gr_ref, qt_ref, ka_ref, kt32_ref, vt32_ref, vte_ref, km_ref, sg_ref) = refs[-10:]
    tm = x_ref.shape[0]
    xn = _rms(x_ref[...], g_ref[...])
    xb = xn.astype(BF16)
    m_ref[...] = _dot_nt(xb, wm_ref[...]).astype(BF16)
    gates = jnp.concatenate(
        [jnp.sum(xn * wif_ref[c:c + 1, :], axis=1, keepdims=True) for c in range(N_GATE_COLS)],
        axis=1)
    gc_ref[...] = gates
    gr_ref[...] = gates.T
    qt_ref[...] = (_dot_nt(wq_ref[...], xb) * (Q_SCALE * LOG2E)).astype(BF16)

    k = _dot_nt(xb, wk_ref[...])
    kt = k.T
    for d in range(kt32_ref.shape[0]):
        kt32_ref[d, 0] = kt
    km_ref[0] = jnp.mean(k.reshape(tm // MOBA_BLOCK, MOBA_BLOCK, MW_A), axis=1)
    row = lax.broadcasted_iota(jnp.int32, (tm, PAIR), 0)
    lane = lax.broadcasted_iota(jnp.int32, (tm, PAIR), 1)
    blk = (pl.program_id(1) * tm + row) // MOBA_BLOCK
    onehot = jnp.where(jnp.logical_and(lane < 2 * MAX_BLOCKS, lane % MAX_BLOCKS == blk),
                       1.0, 0.0).astype(BF16)
    kb = k.astype(BF16)
    for p in range(H_A // 2):
        ka_ref[:, p * K_AUG:p * K_AUG + PAIR] = kb[:, p * PAIR:(p + 1) * PAIR]
        ka_ref[:, p * K_AUG + PAIR:(p + 1) * K_AUG] = onehot

    vt = _dot_nt(wv_ref[...], xb)
    for d in range(vt32_ref.shape[0]):
        vt32_ref[d, 0] = vt
    ones_rows = jnp.where(lax.broadcasted_iota(jnp.int32, (V_ROWS - DH_A, tm), 0) == 0,
                          1.0, 0.0).astype(BF16)
    for h in range(H_A):
        vte_ref[h * V_ROWS:h * V_ROWS + DH_A, :] = vt[h * DH_A:(h + 1) * DH_A].astype(BF16)
        vte_ref[h * V_ROWS + DH_A:(h + 1) * V_ROWS, :] = ones_rows
    sg_ref[...] = jax.nn.sigmoid(_dot_nt(xb, wg_ref[...])).astype(BF16)


def _inproj_prompt(x, g, wm, wif, wq, wk, wv, wg, batch, layer, depth, kv_prev):
    m = x.shape[0]
    seq = m // batch
    tm = TM_PROJ
    nt = seq // tm
    n_pairs = H_A // 2
    row = lambda w: pl.BlockSpec((tm, w), lambda b, t: (b * nt + t, 0))
    col = lambda h: pl.BlockSpec((h, tm), lambda b, t: (0, b * nt + t))
    full = lambda a: pl.BlockSpec(a.shape, lambda b, t: (0,) * a.ndim)
    kv_spec = (pl.BlockSpec((depth, 1, MW_A, tm), lambda b, t: (0, b, 0, t)) if kv_prev is None
               else pl.BlockSpec((1, 1, MW_A, tm), lambda b, t: (layer, b, 0, t)))
    operands = [x, g, wm, wif, wq, wk, wv, wg]
    in_specs = [row(D_MODEL), full(g), full(wm), full(wif), full(wq), full(wk), full(wv), full(wg)]
    aliases = {}
    if kv_prev is not None:
        aliases = {len(operands): 5, len(operands) + 1: 6}
        operands += list(kv_prev)
        in_specs += [pl.BlockSpec(memory_space=pl.ANY)] * 2
    return pl.pallas_call(
        _inproj_prompt_kernel,
        grid=(batch, nt),
        in_specs=in_specs,
        input_output_aliases=aliases,
        out_specs=[row(COL_M_END), row(N_GATE_COLS), col(N_GATE_COLS), col(MW_A),
                   row(n_pairs * K_AUG), kv_spec, kv_spec,
                   col(H_A * V_ROWS),
                   pl.BlockSpec((1, tm // MOBA_BLOCK, MW_A), lambda b, t: (b * nt + t, 0, 0)),
                   row(2 * D_MODEL)],
        out_shape=[jax.ShapeDtypeStruct((m, COL_M_END), BF16),
                   jax.ShapeDtypeStruct((m, N_GATE_COLS), F32),
                   jax.ShapeDtypeStruct((N_GATE_COLS, m), F32),
                   jax.ShapeDtypeStruct((MW_A, m), BF16),
                   jax.ShapeDtypeStruct((m, n_pairs * K_AUG), BF16),
                   jax.ShapeDtypeStruct((depth, batch, MW_A, seq), F32),
                   jax.ShapeDtypeStruct((depth, batch, MW_A, seq), F32),
                   jax.ShapeDtypeStruct((H_A * V_ROWS, m), BF16),
                   jax.ShapeDtypeStruct((m // tm, tm // MOBA_BLOCK, MW_A), F32),
                   jax.ShapeDtypeStruct((m, 2 * D_MODEL), BF16)],
        compiler_params=_params("parallel", "parallel"),
        name="inproj_prompt",
    )(*operands)


def _inproj_sample_kernel(x_ref, g_ref, wm_ref, wif_ref, wq_ref, wk_ref, wv_ref, wg_ref,
                          m_ref, mt_ref, gc_ref, qkv_ref, qt_ref, sg_ref):
    xn = _rms(x_ref[...], g_ref[...])
    xb = xn.astype(BF16)
    m_ref[...] = _dot_nt(xb, wm_ref[...])
    mt_ref[...] = _dot_nt(wm_ref[:2 * MW_M, :], xb)
    gc_ref[...] = _dot_nt(xn, wif_ref[...], HIGHEST)
    qkv_ref[0] = _dot_nt(xb, wq_ref[...])
    qkv_ref[1] = _dot_nt(xb, wk_ref[...])
    qkv_ref[2] = _dot_nt(xb, wv_ref[...])
    qt_ref[...] = _dot_nt(wq_ref[...], xb)
    sg_ref[...] = jax.nn.sigmoid(_dot_nt(xb, wg_ref[...])).astype(BF16)


def _inproj_sample(x, g, wm, wif, wq, wk, wv, wg):
    m = x.shape[0]
    return pl.pallas_call(
        _inproj_sample_kernel,
        out_shape=[jax.ShapeDtypeStruct((m, COL_M_END), F32),
                   jax.ShapeDtypeStruct((2 * MW_M, m), F32),
                   jax.ShapeDtypeStruct((m, N_GATE_COLS), F32),
                   jax.ShapeDtypeStruct((3, m, MW_A), F32),
                   jax.ShapeDtypeStruct((MW_A, m), F32),
                   jax.ShapeDtypeStruct((m, 2 * D_MODEL), BF16)],
        compiler_params=pltpu.CompilerParams(vmem_limit_bytes=VMEM_LIMIT),
        name="inproj_sample",
    )(x, g, wm, wif, wq, wk, wv, wg)


def _mlstm_prompt_kernel(m_ref, gc_ref, gr_ref, bc_ref, br_ref, gm_ref,
                         h_ref, c_ref, ms_ref):
    L = m_ref.shape[0]

    @pl.when(pl.program_id(1) == 0)
    def _():
        c_ref[...] = jnp.zeros_like(c_ref)
        ms_ref[...] = jnp.zeros_like(ms_ref)

    row = lax.broadcasted_iota(jnp.int32, (L, L), 0)
    col = lax.broadcasted_iota(jnp.int32, (L, L), 1)
    causal = col <= row
    lower = causal.astype(F32)
    upper = (row <= col).astype(F32)

    gcol = gc_ref[...] + br_ref[...]
    grow = gr_ref[...] + bc_ref[...]
    bcol = _dot(lower, _log_sigmoid(gcol), HIGHEST)
    brow = _dot(_log_sigmoid(grow), upper, HIGHEST)

    lane = lax.broadcasted_iota(jnp.int32, (L, DH_M), 1)
    ones_blk = jnp.where(lane == 0, 1.0, 0.0).astype(BF16)

    def operands(h):
        return (m_ref[:, h * DH_M:(h + 1) * DH_M],
                m_ref[:, MW_M + h * DH_M:MW_M + (h + 1) * DH_M],
                m_ref[:, 2 * MW_M + h * DH_M:2 * MW_M + (h + 1) * DH_M])

    qk_all = [_dot_nt(operands(h)[0], operands(h)[1]) for h in range(H_M)]
    qc_all = [_dot(operands(h)[0], c_ref[0, h].astype(BF16)) for h in range(H_M)]

    heads = range(H_M)
    i_col = [gcol[:, h:h + 1] for h in heads]
    b_col = [bcol[:, H_M + h:H_M + h + 1] for h in heads]
    i_row = [grow[h:h + 1, :] for h in heads]
    b_row = [brow[H_M + h:H_M + h + 1, :] for h in heads]
    b_last = [b_col[h][L - 1:L, :] for h in heads]
    m_prev = [ms_ref[0, h][0:1, 0:1] for h in heads]
    v_ext = [jnp.concatenate([operands(h)[2], ones_blk], axis=1) for h in heads]

    mt, a, sv = [], [], []
    for h in heads:
        u_row = i_row[h] - b_row[h]
        u_max = jnp.max(jnp.where(causal, u_row, NEG), axis=1, keepdims=True)
        inter = b_col[h] + m_prev[h]
        mt.append(jnp.maximum(inter, b_col[h] + u_max))
        a.append(jnp.exp(inter - mt[h]))
        expo = jnp.where(causal, (b_col[h] - mt[h] + LOG_K_SCALE) + u_row, NEG)
        s = qk_all[h] * jnp.exp(expo)
        sv.append(_dot(s.astype(BF16), v_ext[h]))

    for h in heads:
        k = operands(h)[1]
        g_row = b_last[h] - b_row[h] + i_row[h]
        e = b_last[h] + m_prev[h]
        m_new = jnp.maximum(e, jnp.max(g_row, axis=1, keepdims=True))
        w_col = jnp.exp(b_last[h] - b_col[h] + i_col[h] - (m_new - LOG_K_SCALE))
        kw = (k.astype(F32) * w_col).astype(BF16)
        c_ref[0, h] = jnp.exp(e - m_new) * c_ref[0, h] + _dot_tn(kw, v_ext[h])
        ms_ref[0, h] = jnp.broadcast_to(m_new, ms_ref.shape[2:])

    for h in heads:
        o = m_ref[:, 3 * MW_M + h * DH_M:3 * MW_M + (h + 1) * DH_M]
        nd = a[h] * qc_all[h] + sv[h]
        num = nd[:, :DH_M]
        denom = jnp.maximum(jnp.abs(nd[:, DH_M:DH_M + 1]), jnp.exp(-mt[h]))
        nc = num - jnp.mean(num, axis=1, keepdims=True)
        var = jnp.mean(nc * nc, axis=1, keepdims=True)
        hn = nc * lax.rsqrt(var + EPS * denom * denom) * gm_ref[:, h * DH_M:(h + 1) * DH_M]
        h_ref[:, h * DH_M:(h + 1) * DH_M] = (hn * jax.nn.sigmoid(o.astype(F32))).astype(BF16)


def _mlstm_prompt(mproj, gcol, grow, b_if, g_mlstm, batch):
    m = mproj.shape[0]
    L = L_CHUNK
    nc = m // batch // L
    bias_row = b_if.reshape(1, N_GATE_COLS)
    bias_col = b_if.reshape(N_GATE_COLS, 1)
    gm = g_mlstm.reshape(1, MW_M)
    return pl.pallas_call(
        _mlstm_prompt_kernel,
        grid=(batch, nc),
        in_specs=[pl.BlockSpec((L, COL_M_END), lambda b, c: (b * nc + c, 0)),
                  pl.BlockSpec((L, N_GATE_COLS), lambda b, c: (b * nc + c, 0)),
                  pl.BlockSpec((N_GATE_COLS, L), lambda b, c: (0, b * nc + c)),
                  pl.BlockSpec((N_GATE_COLS, 1), lambda b, c: (0, 0)),
                  pl.BlockSpec((1, N_GATE_COLS), lambda b, c: (0, 0)),
                  pl.BlockSpec((1, MW_M), lambda b, c: (0, 0))],
        out_specs=[pl.BlockSpec((L, MW_M), lambda b, c: (b * nc + c, 0)),
                   pl.BlockSpec((1, H_M, DH_M, 2 * DH_M), lambda b, c: (b, 0, 0, 0)),
                   pl.BlockSpec((1, H_M, 8, 128), lambda b, c: (b, 0, 0, 0))],
        out_shape=[jax.ShapeDtypeStruct((m, MW_M), BF16),
                   jax.ShapeDtypeStruct((batch, H_M, DH_M, 2 * DH_M), F32),
                   jax.ShapeDtypeStruct((batch, H_M, 8, 128), F32)],
        compiler_params=_params("parallel", "arbitrary"),
        name="mlstm_prompt",
    )(mproj, gcol, grow, bias_col, bias_row, gm)


def _mlstm_sample_kernel(m_ref, mt_ref, gc_ref, bi_ref, gm_ref, c_ref, n_ref, ms_ref,
                         h_ref, co_ref, no_ref, mo_ref):
    nb = mt_ref.shape[1]
    heads = range(H_M)
    onehot = lax.broadcasted_iota(jnp.int32, (1, nb), 1) == pl.program_id(0)
    g = gc_ref[0] + bi_ref[...]

    def part(p, h):
        return m_ref[0, :, p * MW_M + h * DH_M:p * MW_M + (h + 1) * DH_M]

    def column(p, h):
        rows = mt_ref[p * MW_M + h * DH_M:p * MW_M + (h + 1) * DH_M, :]
        return jnp.sum(jnp.where(onehot, rows, 0.0), axis=1, keepdims=True)

    q_col = [column(0, h) for h in heads]
    k_col = [column(1, h) for h in heads]
    m_new, a, w = [], [], []
    for h in heads:
        inter = _log_sigmoid(g[:, H_M + h:H_M + h + 1]) + ms_ref[0, :, h:h + 1]
        i_pre = g[:, h:h + 1]
        m_new.append(jnp.maximum(inter, i_pre))
        a.append(jnp.exp(inter - m_new[h]))
        w.append(jnp.exp(i_pre - (m_new[h] - LOG_K_SCALE)))
    s = [jnp.sum(part(0, h) * part(1, h), axis=1, keepdims=True) * w[h] for h in heads]
    qc = [jnp.sum(c_ref[0, h] * q_col[h], axis=0, keepdims=True) for h in heads]
    qn = [jnp.sum(part(0, h) * n_ref[0, h:h + 1, :], axis=1, keepdims=True) for h in heads]
    for h in heads:
        co_ref[0, h] = a[h] * c_ref[0, h] + (w[h] * k_col[h]) * part(2, h)
        no_ref[0, h:h + 1, :] = a[h] * n_ref[0, h:h + 1, :] + w[h] * part(1, h)
        mo_ref[0, :, h:h + 1] = m_new[h]
    for h in heads:
        sl = slice(h * DH_M, (h + 1) * DH_M)
        num = a[h] * qc[h] + s[h] * part(2, h)
        den = a[h] * qn[h] + s[h]
        hh = num / jnp.maximum(jnp.abs(den), jnp.exp(-m_new[h]))
        hc = hh - jnp.mean(hh, axis=1, keepdims=True)
        var = jnp.mean(hc * hc, axis=1, keepdims=True)
        hn = hc * lax.rsqrt(var + EPS) * gm_ref[:, sl]
        h_ref[0, :, sl] = (hn * jax.nn.sigmoid(part(3, h))).astype(BF16)


def _mlstm_sample(mproj, mproj_t, gcol, b_if, g_mlstm, c0, n0, m0):
    nb = mproj.shape[0]
    return pl.pallas_call(
        _mlstm_sample_kernel,
        grid=(nb,),
        in_specs=[pl.BlockSpec((1, 1, COL_M_END), lambda b: (b, 0, 0)),
                  pl.BlockSpec((2 * MW_M, nb), lambda b: (0, 0)),
                  pl.BlockSpec((1, 1, N_GATE_COLS), lambda b: (b, 0, 0)),
                  pl.BlockSpec((1, N_GATE_COLS), lambda b: (0, 0)),
                  pl.BlockSpec((1, MW_M), lambda b: (0, 0)),
                  pl.BlockSpec((1, H_M, DH_M, DH_M), lambda b: (b, 0, 0, 0)),
                  pl.BlockSpec((1, H_M, DH_M), lambda b: (b, 0, 0)),
                  pl.BlockSpec((1, 1, H_M), lambda b: (b, 0, 0))],
        out_specs=[pl.BlockSpec((1, 1, MW_M), lambda b: (b, 0, 0)),
                   pl.BlockSpec((1, H_M, DH_M, DH_M), lambda b: (b, 0, 0, 0)),
                   pl.BlockSpec((1, H_M, DH_M), lambda b: (b, 0, 0)),
                   pl.BlockSpec((1, 1, H_M), lambda b: (b, 0, 0))],
        out_shape=[jax.ShapeDtypeStruct((nb, 1, MW_M), BF16),
                   jax.ShapeDtypeStruct((nb, H_M, DH_M, DH_M), F32),
                   jax.ShapeDtypeStruct((nb, H_M, DH_M), F32),
                   jax.ShapeDtypeStruct((nb, 1, H_M), F32)],
        compiler_params=_params("parallel"),
        name="mlstm_sample",
    )(mproj.reshape(nb, 1, COL_M_END), mproj_t, gcol.reshape(nb, 1, N_GATE_COLS),
      b_if.reshape(1, N_GATE_COLS), g_mlstm.reshape(1, MW_M), c0, n0, m0.reshape(nb, 1, H_M))


def _t5_bucket(dist):
    n = jnp.maximum(dist, 0)
    max_exact = N_BUCKETS // 2
    nf = jnp.maximum(n, 1).astype(F32)
    large = max_exact + (jnp.log(nf / max_exact) / math.log(REL_MAX_DIST / max_exact)
                         * (N_BUCKETS - max_exact)).astype(jnp.int32)
    large = jnp.minimum(large, N_BUCKETS - 1)
    return jnp.where(n < max_exact, n, large)


def _t5_bias(dist, rel_ref, h):
    bucket = _t5_bucket(dist)
    bias = jnp.zeros(dist.shape, F32)
    for kb in range(N_BUCKETS):
        bias = jnp.where(bucket == kb, rel_ref[kb, h], bias)
    return bias


def _bias_table_kernel(rel_ref, o_ref):
    delta = pl.program_id(0)

    @pl.when(delta >= N_NEAR)
    def _():
        o_ref[...] = jnp.full(o_ref.shape, jnp.where(delta == TILE_MASKED, NEG, 0.0), F32)

    pl.when(delta < N_NEAR)(lambda: _bias_tiles(delta, rel_ref, o_ref))


def _bias_tiles(delta, rel_ref, o_ref):
    c = lax.broadcasted_iota(jnp.int32, (MOBA_BLOCK, MOBA_BLOCK), 0)
    r = lax.broadcasted_iota(jnp.int32, (MOBA_BLOCK, MOBA_BLOCK), 1)
    dist = delta * MOBA_BLOCK + r - c
    bucket = _t5_bucket(dist)
    bits = [(bucket >> i) & 1 == 1 for i in range(N_BUCKETS.bit_length() - 1)]
    for h in range(H_A):
        level = [rel_ref[kb, h] * LOG2E for kb in range(N_BUCKETS)]
        for bit in bits:
            level = [jnp.where(bit, level[2 * i + 1], level[2 * i]) for i in range(len(level) // 2)]
        o_ref[h, 0] = jnp.where(dist >= 0, level[0], NEG)


def _bias_table(rel_bias):
    return pl.pallas_call(
        _bias_table_kernel,
        grid=(N_TILES,),
        in_specs=[pl.BlockSpec(memory_space=pltpu.SMEM)],
        out_specs=pl.BlockSpec((H_A, 1, MOBA_BLOCK, MOBA_BLOCK), lambda d: (0, d, 0, 0)),
        out_shape=jax.ShapeDtypeStruct((H_A, N_TILES, MOBA_BLOCK, MOBA_BLOCK), F32),
        compiler_params=_params("parallel"),
        name="bias_table",
    )(rel_bias)


def _select_topk_t(score, n_valid_rows, k):
    nb = score.shape[0]
    row = lax.broadcasted_iota(jnp.int32, score.shape, 0)
    past = row < n_valid_rows
    sc = jnp.where(past, score, -jnp.inf)
    sel = jnp.full(score.shape, NEG, F32)
    for _ in range(k):
        mx = jnp.max(sc, axis=0, keepdims=True)
        idx = jnp.min(jnp.where(sc == mx, row, nb), axis=0, keepdims=True)
        hit = row == idx
        sel = jnp.where(hit, 0.0, sel)
        sc = jnp.where(hit, -jnp.inf, sc)
    return jnp.where(past, sel, NEG)


def _moba_prompt_kernel(qi_ref, grp_ref, first_ref, last_ref, far_item_ref, far_ref,
                        qt_ref, ka_ref, vte_ref, km_ref, tab_ref, o_ref,
                        qp_scr, m_scr, acc_scr, sa_scr, sb_scr, gma_scr, gmb_scr):
    hp = pl.program_id(1)
    seq = qt_ref.shape[1]
    nb = km_ref.shape[1]
    tq = MOBA_BLOCK
    n_items = qi_ref.shape[0] - 1
    setup_w = 4 * MOBA_BLOCK

    def setup(c, carry):
        cols = pl.ds(pl.multiple_of(c * setup_w, setup_w), setup_w)
        blk_row = lax.broadcasted_iota(jnp.int32, (nb, setup_w), 0)
        q_blk = (c * setup_w + lax.broadcasted_iota(jnp.int32, (1, setup_w), 1)) // MOBA_BLOCK
        sub = lax.broadcasted_iota(jnp.int32, (PAIR, setup_w), 0)
        qpair = qt_ref[:, cols]
        for h in range(2):
            qth = qt_ref[h * DH_A:(h + 1) * DH_A, cols].astype(F32)
            score = _dot(km_ref[0, :, h * DH_A:(h + 1) * DH_A], qth, HIGHEST)
            sel = _select_topk_t(score, q_blk, MOBA_TOPK)
            sel = jnp.where(blk_row == q_blk, 0.0, sel)
            sel = sel + jnp.where(q_blk - blk_row >= N_NEAR, far_ref[2 * hp + h], 0.0)
            hi = sel.astype(BF16)
            qp_scr[h, 0:PAIR, cols] = jnp.where(sub // DH_A == h, qpair, jnp.zeros_like(qpair))
            qp_scr[h, PAIR:, cols] = jnp.zeros((K_AUG - PAIR, setup_w), BF16)
            qp_scr[h, SEL_HI:SEL_HI + nb, cols] = hi
            qp_scr[h, SEL_LO:SEL_LO + nb, cols] = (sel - hi.astype(F32)).astype(BF16)
        return carry

    lax.fori_loop(0, seq // setup_w, setup, 0)
    m_scr[...] = jnp.full(m_scr.shape, NEG, F32)
    acc_scr[...] = jnp.zeros(acc_scr.shape, F32)

    gk = MOBA_GROUP * MOBA_BLOCK

    def score_block(w, t, h, far):
        qi = qi_ref[w]
        g = grp_ref[w]
        kblk = ka_ref[pl.ds(pl.multiple_of((g * MOBA_GROUP + t) * MOBA_BLOCK, MOBA_BLOCK),
                            MOBA_BLOCK), :]
        qcols = pl.ds(pl.multiple_of(qi * tq, tq), tq)
        s = _dot(kblk, qp_scr[h, :, qcols])
        if far:
            return s
        delta = qi - (g * MOBA_GROUP + t)
        tile = jnp.where(delta < 0, TILE_MASKED, jnp.minimum(delta, TILE_ZERO))
        return s + tab_ref[h, tile]

    def step(w, s_cur, gm_cur, s_nxt, gm_nxt, far):
        qi = qi_ref[w]
        start = pl.multiple_of(grp_ref[w] * gk, gk)
        first = first_ref[w] == 1
        m_old = [jnp.where(first, NEG, m_scr[h]) for h in range(2)]
        m_new = [jnp.maximum(m_old[h], gm_cur[h]) for h in range(2)]
        pv = [None, None]
        gmax = [None, None]

        def next_scores(t):
            rows = slice(t * MOBA_BLOCK, (t + 1) * MOBA_BLOCK)
            for h in range(2):
                s = score_block(w + 1, t, h, far)
                s_nxt[h, rows, :] = s
                cmax = jnp.max(s, axis=0, keepdims=True)
                gmax[h] = cmax if t == 0 else jnp.maximum(gmax[h], cmax)

        def attend(t):
            rows = slice(t * MOBA_BLOCK, (t + 1) * MOBA_BLOCK)
            for h in range(2):
                p = jnp.exp2(s_cur[h, rows, :] - m_new[h]).astype(BF16)
                vblk = vte_ref[h * V_ROWS:(h + 1) * V_ROWS,
                               pl.ds(start + t * MOBA_BLOCK, MOBA_BLOCK)]
                d = _dot(vblk, p)
                pv[h] = d if t == 0 else pv[h] + d

        for t in range(MOBA_GROUP):
            next_scores(t)
            attend(t)
        for h in range(2):
            acc_old = jnp.where(first, 0.0, acc_scr[h])
            acc_scr[h] = jnp.exp2(m_old[h] - m_new[h]) * acc_old + pv[h]
            m_scr[h] = m_new[h]
            gm_nxt[h] = gmax[h]

        @pl.when(last_ref[w] == 1)
        def _():
            outs = [acc_scr[h, 0:DH_A, :] / acc_scr[h, DH_A:DH_A + 1, :] for h in range(2)]
            orows = pl.ds(pl.multiple_of(qi * tq, tq), tq)
            o_ref[orows, :] = jnp.concatenate(outs, axis=0).T.astype(BF16)

    for h in range(2):
        gmax = None
        for t in range(MOBA_GROUP):
            s = score_block(0, t, h, False)
            sa_scr[h, t * MOBA_BLOCK:(t + 1) * MOBA_BLOCK, :] = s
            cmax = jnp.max(s, axis=0, keepdims=True)
            gmax = cmax if t == 0 else jnp.maximum(gmax, cmax)
        gma_scr[h] = gmax

    def step_by_kind(w, *buffers):
        next_is_far = far_item_ref[w + 1] == 1
        pl.when(next_is_far)(lambda: step(w, *buffers, True))
        pl.when(jnp.logical_not(next_is_far))(lambda: step(w, *buffers, False))

    def pair_of_items(i, carry):
        w = 2 * i
        step_by_kind(w, sa_scr, gma_scr, sb_scr, gmb_scr)
        step_by_kind(w + 1, sb_scr, gmb_scr, sa_scr, gma_scr)
        return carry

    lax.fori_loop(0, n_items // 2, pair_of_items, 0)


def _moba_work_items(nb):
    qi, grp, first, last, far = [], [], [], [], []
    for q in range(nb):
        n_groups = q // MOBA_GROUP + 1
        for g in range(n_groups):
            qi.append(q)
            grp.append(g)
            first.append(int(g == 0))
            last.append(int(g == n_groups - 1))
            far.append(int(q - (g * MOBA_GROUP + MOBA_GROUP - 1) >= N_NEAR))
    assert len(qi) % 2 == 0
    tables = [qi + qi[-1:], grp + grp[-1:], first + [0], last + [0], far + [0]]
    return [jnp.asarray(t, jnp.int32) for t in tables]


def _moba_prompt(qt, ka, vte, kmean, table, far, batch):
    m = qt.shape[1]
    seq = m // batch
    nb = seq // MOBA_BLOCK
    assert nb <= MAX_BLOCKS and nb % MOBA_GROUP == 0
    tq = MOBA_BLOCK
    n_pairs = H_A // 2
    gk = MOBA_GROUP * MOBA_BLOCK
    smem = pl.BlockSpec(memory_space=pltpu.SMEM)
    return pl.pallas_call(
        _moba_prompt_kernel,
        grid=(batch, n_pairs),
        in_specs=[smem, smem, smem, smem, smem, smem,
                  pl.BlockSpec((PAIR, seq), lambda b, hp: (hp, b)),
                  pl.BlockSpec((seq, K_AUG), lambda b, hp: (b, hp)),
                  pl.BlockSpec((2 * V_ROWS, seq), lambda b, hp: (hp, b)),
                  pl.BlockSpec((1, nb, PAIR), lambda b, hp: (b, 0, hp)),
                  pl.BlockSpec((2, N_TILES, tq, tq), lambda b, hp: (hp, 0, 0, 0),
                               pipeline_mode=pl.Buffered(1))],
        out_specs=pl.BlockSpec((seq, PAIR), lambda b, hp: (b, hp)),
        out_shape=jax.ShapeDtypeStruct((m, MW_A), BF16),
        scratch_shapes=[pltpu.VMEM((2, K_AUG, seq), BF16),
                        pltpu.VMEM((2, 1, tq), F32),
                        pltpu.VMEM((2, V_ROWS, tq), F32),
                        pltpu.VMEM((2, gk, tq), F32),
                        pltpu.VMEM((2, gk, tq), F32),
                        pltpu.VMEM((2, 1, tq), F32),
                        pltpu.VMEM((2, 1, tq), F32)],
        compiler_params=_params("parallel", "arbitrary"),
        name="moba_prompt",
    )(*_moba_work_items(nb), far, qt, ka, vte, kmean.reshape(batch, nb, MW_A), table)


def _logits_sample_kernel(pt_ref, qt_ref, *refs):
    o_ref, qb_scr = refs[-2], refs[-1]
    b = pl.program_id(0)
    nb_s = qt_ref.shape[1]

    @pl.when(pl.program_id(1) == 0)
    def _():
        onehot = lax.broadcasted_iota(jnp.int32, (1, nb_s), 1) == b
        qcol = jnp.sum(jnp.where(onehot, qt_ref[...], 0.0), axis=1, keepdims=True)
        qb_scr[...] = jnp.broadcast_to(qcol * Q_SCALE, qb_scr.shape)

    per_block = MOBA_BLOCK // PAGE_SIZE
    for i in range(PAGES_PER_STEP):
        for h in range(H_A):
            kt = refs[i][0, 0, h]
            lg = jnp.sum(kt * qb_scr[h * DH_A:(h + 1) * DH_A, :], axis=0, keepdims=True)
            r, half = i // per_block, i % per_block
            o_ref[0, h, r:r + 1, half * PAGE_SIZE:(half + 1) * PAGE_SIZE] = lg


def _logits_sample(cache_kt, page_table, qt, layer):
    nb_s, n_pages = page_table.shape
    steps = n_pages // PAGES_PER_STEP
    rows = PAGES_PER_STEP * PAGE_SIZE // MOBA_BLOCK

    def page_spec(i):
        return pl.BlockSpec((1, 1, H_A, DH_A, PAGE_SIZE),
                            lambda b, s, pt: (layer, pt[b, s * PAGES_PER_STEP + i], 0, 0, 0))

    return pl.pallas_call(
        _logits_sample_kernel,
        grid_spec=pltpu.PrefetchScalarGridSpec(
            num_scalar_prefetch=1,
            grid=(nb_s, steps),
            in_specs=[pl.BlockSpec(qt.shape, lambda b, s, pt: (0, 0))]
                     + [page_spec(i) for i in range(PAGES_PER_STEP)],
            out_specs=pl.BlockSpec((1, H_A, rows, MOBA_BLOCK), lambda b, s, pt: (b, 0, s, 0)),
            scratch_shapes=[pltpu.VMEM((MW_A, PAGE_SIZE), F32)]),
        out_shape=jax.ShapeDtypeStruct((nb_s, H_A, steps * rows, MOBA_BLOCK), F32),
        compiler_params=_params("parallel", "arbitrary"),
        name="logits_sample",
    )(page_table, qt, *([cache_kt] * PAGES_PER_STEP))


def _select_sample_kernel(lg_ref, o_ref):
    sc = jnp.sum(lg_ref[...], axis=-1)
    nblk = sc.shape[-1]
    blk = lax.broadcasted_iota(jnp.int32, sc.shape, 2)
    for t in range(MOBA_TOPK):
        mx = jnp.max(sc, axis=2, keepdims=True)
        idx = jnp.min(jnp.where(sc == mx, blk, nblk), axis=2, keepdims=True)
        o_ref[:, :, t:t + 1] = idx
        sc = jnp.where(blk == idx, -jnp.inf, sc)


def _select_sample(logits):
    nb_s = logits.shape[0]
    return pl.pallas_call(
        _select_sample_kernel,
        out_shape=jax.ShapeDtypeStruct((nb_s, H_A, MOBA_TOPK), jnp.int32),
        compiler_params=pltpu.CompilerParams(vmem_limit_bytes=VMEM_LIMIT),
        name="select_sample",
    )(logits)


def _bias_sample_kernel(rel_ref, o_ref, *, past_len):
    h = pl.program_id(0)
    nblk = o_ref.shape[1]
    kpos = (lax.broadcasted_iota(jnp.int32, (nblk, MOBA_BLOCK), 0) * MOBA_BLOCK
            + lax.broadcasted_iota(jnp.int32, (nblk, MOBA_BLOCK), 1))
    o_ref[0] = _t5_bias(past_len - kpos, rel_ref, h)


def _bias_sample(rel_bias, past_len):
    nblk = past_len // MOBA_BLOCK
    return pl.pallas_call(
        functools.partial(_bias_sample_kernel, past_len=past_len),
        grid=(H_A,),
        in_specs=[pl.BlockSpec(memory_space=pltpu.SMEM)],
        out_specs=pl.BlockSpec((1, nblk, MOBA_BLOCK), lambda h: (h, 0, 0)),
        out_shape=jax.ShapeDtypeStruct((H_A, nblk, MOBA_BLOCK), F32),
        compiler_params=_params("parallel"),
        name="bias_sample",
    )(rel_bias)


def _attend_sample_kernel(pt_ref, sel_ref, rel_ref, lg_ref, bias_ref, qkv_ref, *refs):
    o_ref = refs[-1]
    v_refs = refs[:-1]
    per_block = MOBA_BLOCK // PAGE_SIZE
    n_sel_pages = MOBA_TOPK * per_block
    b = pl.program_id(0)

    for j in range(ATTEND_HEADS):
        h = pl.program_id(1) * ATTEND_HEADS + j
        q = qkv_ref[0, 0, pl.ds(h, 1), :] * Q_SCALE
        k_new = qkv_ref[1, 0, pl.ds(h, 1), :]
        v_new = qkv_ref[2, 0, pl.ds(h, 1), :]
        own = jnp.sum(k_new * q, axis=1, keepdims=True) + rel_ref[0, h]
        logits = []
        mx = own
        for t in range(MOBA_TOPK):
            blk = sel_ref[b, h, t]
            lg = lg_ref[0, j, pl.ds(blk, 1), :] + bias_ref[j, pl.ds(blk, 1), :]
            logits.append(lg)
            mx = jnp.maximum(mx, jnp.max(lg, axis=1, keepdims=True))
        p_own = jnp.exp(own - mx)
        den = p_own
        acc = p_own * v_new
        for t in range(MOBA_TOPK):
            p = jnp.exp(logits[t] - mx)
            den = den + jnp.sum(p, axis=1, keepdims=True)
            pb = p.astype(BF16)
            for i in range(per_block):
                vt = v_refs[j * n_sel_pages + t * per_block + i][0, 0, 0].astype(BF16)
                acc = acc + _dot_nt(pb[:, i * PAGE_SIZE:(i + 1) * PAGE_SIZE], vt)
        o_ref[0, pl.ds(h, 1), :] = acc / den


def _attend_sample(cache_vt, page_table, sel, rel_bias, bias_s, logits, qkv, layer):
    nb_s = page_table.shape[0]
    per_block = MOBA_BLOCK // PAGE_SIZE
    n_sel_pages = MOBA_TOPK * per_block
    nblk = logits.shape[2]
    nh = ATTEND_HEADS

    def page_spec(j, t):
        def imap(b, hg, pt, sl, rel):
            h = hg * nh + j
            page = sl[b, h, t // per_block] * per_block + (t % per_block)
            return (layer, pt[b, page], h, 0, 0)
        return pl.BlockSpec((1, 1, 1, DH_A, PAGE_SIZE), imap)

    pages = [page_spec(j, t) for j in range(nh) for t in range(n_sel_pages)]
    return pl.pallas_call(
        _attend_sample_kernel,
        grid_spec=pltpu.PrefetchScalarGridSpec(
            num_scalar_prefetch=3,
            grid=(nb_s, H_A // nh),
            in_specs=[pl.BlockSpec((1, nh, nblk, MOBA_BLOCK), lambda b, hg, pt, sl, rel: (b, hg, 0, 0)),
                      pl.BlockSpec((nh, nblk, MOBA_BLOCK), lambda b, hg, pt, sl, rel: (hg, 0, 0)),
                      pl.BlockSpec((3, 1, H_A, DH_A), lambda b, hg, pt, sl, rel: (0, b, 0, 0))]
                     + pages,
            out_specs=pl.BlockSpec((1, H_A, DH_A), lambda b, hg, pt, sl, rel: (b, 0, 0))),
        out_shape=jax.ShapeDtypeStruct((nb_s, H_A, DH_A), F32),
        compiler_params=_params("parallel", "arbitrary"),
        name="attend_sample",
    )(page_table, sel, rel_bias, logits, bias_s, qkv, *([cache_vt] * len(pages)))


def _mix_ffn_kernel(x_ref, hm_ref, ha_ref, sg_ref, wbm_ref, wba_ref, wo_ref, gmix_ref,
                    gpre_ref, gpost_ref, wg_ref, wu_ref, wd_ref, o_ref):
    sg = sg_ref[...].astype(F32)
    mixed = (sg[:, :D_MODEL] * _dot(hm_ref[...], wbm_ref[...])
             + sg[:, D_MODEL:] * _dot(ha_ref[...], wba_ref[...]))
    x = x_ref[...] + _rms(_dot(mixed.astype(BF16), wo_ref[...]), gmix_ref[...])
    hf = _rms(x, gpre_ref[...]).astype(BF16)
    acc = jnp.zeros(x.shape, F32)
    for c in range(D_FF // FF_CHUNK):
        sl = slice(c * FF_CHUNK, (c + 1) * FF_CHUNK)
        gate = _dot(hf, wg_ref[:, sl])
        up = _dot(hf, wu_ref[:, sl])
        mid = (gate * jax.nn.sigmoid(gate) * up).astype(BF16)
        acc = acc + _dot(mid, wd_ref[sl, :])
    o_ref[...] = x + _rms(acc, gpost_ref[...])


def _mix_ffn(x, hm, ha, sg, wbm, wba, wo, gmix, gpre, gpost, wg, wu, wd):
    m = x.shape[0]
    tm = min(TM_FFN, m)
    row = lambda w: pl.BlockSpec((tm, w), lambda i: (i, 0))
    full = lambda a: pl.BlockSpec(a.shape, lambda i: (0,) * a.ndim, pipeline_mode=pl.Buffered(1))
    weights = (wbm, wba, wo, gmix, gpre, gpost, wg, wu, wd)
    return pl.pallas_call(
        _mix_ffn_kernel,
        grid=(m // tm,),
        in_specs=[row(D_MODEL), row(MW_M), row(MW_A), row(2 * D_MODEL)] + [full(w) for w in weights],
        out_specs=row(D_MODEL),
        out_shape=jax.ShapeDtypeStruct((m, D_MODEL), F32),
        compiler_params=_params("parallel"),
        name="mix_ffn",
    )(x, hm, ha, sg, *weights)


def kernel(x_prompt, x_sample, cache_k, cache_v, state_C, state_n, state_m, page_table,
           norm_mix_pre, norm_mix_post, norm_ffn_pre, norm_ffn_post, w_in, b_igate, b_fgate,
           g_mlstm, w_branch_m, w_branch_a, w_out, w_ffn_gate, w_ffn_up, w_ffn_down, rel_bias):
    bp, seq, _ = x_prompt.shape
    bs = x_sample.shape[0]
    depth = w_in.shape[0]
    xp = x_prompt.reshape(bp * seq, D_MODEL)
    xs = x_sample.reshape(bs, D_MODEL)
    cache_kt = jnp.transpose(cache_k, (0, 1, 3, 4, 2))
    cache_vt = jnp.transpose(cache_v, (0, 1, 3, 4, 2))
    w_in_t = jnp.swapaxes(w_in, 1, 2)

    table = _bias_table(rel_bias)
    far = rel_bias[N_BUCKETS - 1] * LOG2E
    bias_s = _bias_sample(rel_bias, page_table.shape[1] * PAGE_SIZE)

    outs = {k: [] for k in ("cp", "np", "mp", "ks", "vs", "cs", "ns", "ms")}
    kv_prompt = None
    for l in range(depth):
        row = lambda a: a[l].reshape(1, -1)
        wt = w_in_t[l]
        wm = wt[:COL_M_END].astype(BF16)
        wif = wt[COL_M_END:COL_IF_END]
        wq = wt[COL_IF_END:COL_Q_END].astype(BF16)
        wk = wt[COL_Q_END:COL_K_END].astype(BF16)
        wv = wt[COL_K_END:COL_V_END].astype(BF16)
        wgt = wt[COL_V_END:].astype(BF16)
        b_if = jnp.concatenate([b_igate[l], b_fgate[l]])
        wbm, wba, wo = (w_branch_m[l].astype(BF16), w_branch_a[l].astype(BF16), w_out[l].astype(BF16))
        wfg, wfu, wfd = (w_ffn_gate[l].astype(BF16), w_ffn_up[l].astype(BF16), w_ffn_down[l].astype(BF16))

        mproj, gcol, grow, qt, ka, kt32, vt32, vte, kmean, sg = _inproj_prompt(
            xp, row(norm_mix_pre), wm, wif, wq, wk, wv, wgt, bp, l, depth, kv_prompt)
        kv_prompt = (kt32, vt32)
        hm, cext, mstate = _mlstm_prompt(mproj, gcol, grow, b_if, g_mlstm[l], bp)
        ha = _moba_prompt(qt, ka, vte, kmean, table, far, bp)
        post = (wbm, wba, wo, row(norm_mix_post), row(norm_ffn_pre), row(norm_ffn_post),
                wfg, wfu, wfd)
        xp = _mix_ffn(xp, hm, ha, sg, *post)
        outs["cp"].append(cext[..., :DH_M])
        outs["np"].append(cext[..., DH_M])
        outs["mp"].append(mstate[:, :, 0, 0])

        mproj_s, mproj_t, gcol_s, qkv_s, qt_s, sg_s = _inproj_sample(
            xs, row(norm_mix_pre), wm, wif, wq, wk, wv, wgt)
        hm_s, c_s, n_s, m_s = _mlstm_sample(mproj_s, mproj_t, gcol_s, b_if, g_mlstm[l],
                                            state_C[l], state_n[l], state_m[l])
        logits_s = _logits_sample(cache_kt, page_table, qt_s, l)
        sel = _select_sample(logits_s)
        ha_s = _attend_sample(cache_vt, page_table, sel, rel_bias, bias_s, logits_s,
                              qkv_s.reshape(3, bs, H_A, DH_A), l)
        xs = _mix_ffn(xs, hm_s.reshape(bs, MW_M), ha_s.reshape(bs, MW_A).astype(BF16), sg_s, *post)
        outs["ks"].append(qkv_s[1].reshape(bs, 1, H_A, DH_A))
        outs["vs"].append(qkv_s[2].reshape(bs, 1, H_A, DH_A))
        outs["cs"].append(c_s)
        outs["ns"].append(n_s)
        outs["ms"].append(m_s.reshape(bs, H_M))

    st = lambda k: jnp.stack(outs[k])
    kv_out = lambda a: jnp.transpose(a.reshape(depth, bp, H_A, DH_A, seq), (0, 1, 4, 2, 3))
    return (xp.reshape(bp, seq, D_MODEL), xs.reshape(bs, 1, D_MODEL),
            kv_out(kv_prompt[0]), kv_out(kv_prompt[1]), st("cp"), st("np"), st("mp"),
            st("ks"), st("vs"), st("cs"), st("ns"), st("ms"))
```
